```python
import math
import jax, jax.numpy as jnp
from jax import lax
import numpy as np

D_MODEL = 1024
BATCH = 8
SEQ = 4096
DEPTH = 2
DEC_BATCH = 32
DEC_SEQ = 8
PAST_LEN = 16384
PAGE_SIZE = 128

HG_HEADS = 4
HG_DIM = 128
HG_WIDTH = HG_HEADS * HG_DIM
HG_CHUNK = 64
LB_FLOOR = 1e-30
S5_GROUP = 16
S5_GROUPS = 32
S5_WIDTH = S5_GROUP * S5_GROUPS
S5_STATE = 64
NSA_HEADS = 8
NSA_KV = 2
NSA_REP = NSA_HEADS // NSA_KV
HEAD_DIM = 64
NSA_WIDTH = NSA_HEADS * HEAD_DIM
CMP_LEN = 32
CMP_STRIDE = 16
CMP_HIDDEN = 128
SEL_BLOCK = 64
SEL_TOPN = 16
SEL_QBLK = 64
WINDOW = 512
WIN_QBLK = 128
N_BUCKETS = 32
MAX_DIST = 128
N_BRANCH = 3
BR_WIDTH = 512
D_FF = -(-8 * D_MODEL // (3 * 256)) * 256
KV_W = NSA_KV * HEAD_DIM
IN_SPLITS = (HG_WIDTH, HG_WIDTH, HG_WIDTH, HG_WIDTH, S5_WIDTH, NSA_WIDTH, KV_W, KV_W, KV_W, KV_W, KV_W, KV_W, 3 * NSA_HEADS, D_MODEL, D_MODEL, D_MODEL)
N_IN = sum(IN_SPLITS)
IN_OFFSETS = tuple(sum(IN_SPLITS[:i + 1]) for i in range(len(IN_SPLITS) - 1))
NEG = -1e30
FORCE = 1e9
F32 = jnp.float32

kernel_name = "hgrn2_s5_nsa_gated_hybrid_step"


def rms_norm(x, g, eps=1e-6):
    xf = x.astype(F32)
    y = xf * lax.rsqrt(jnp.mean(xf * xf, axis=-1, keepdims=True) + eps)
    return (y * g.astype(F32)).astype(x.dtype)


def rel_bucket(dist):
    n = jnp.maximum(dist, 0)
    exact = N_BUCKETS // 2
    nf = jnp.maximum(n, 1).astype(F32)
    large = exact + (jnp.log(nf / exact) / math.log(MAX_DIST / exact) * (N_BUCKETS - exact)).astype(jnp.int32)
    return jnp.where(n < exact, n, jnp.minimum(large, N_BUCKETS - 1))


def hgrn2_scan(q, log_f, k, v, s0):
    B, T, H, _ = q.shape
    C = math.gcd(T, HG_CHUNK)
    n = T // C

    def chunks(a):
        return a.reshape(B, n, C, H, a.shape[-1]).transpose(1, 0, 3, 2, 4)

    causal = jnp.tril(jnp.ones((C, C), dtype=bool))[:, :, None]

    def step(S, inp):
        qc, gc, kc, vc = inp
        G = jnp.cumsum(gc, axis=2)
        decay = jnp.exp(jnp.where(causal, G[:, :, :, None, :] - G[:, :, None, :, :], NEG))
        attn = jnp.einsum("bhtd,bhtsd,bhsd->bhts", qc, decay, kc)
        o = attn @ vc + jnp.einsum("bhtd,bhdv->bhtv", qc * jnp.exp(G), S)
        g_end = G[:, :, -1]
        S = jnp.exp(g_end)[..., None] * S + jnp.einsum("bhsd,bhsv->bhdv", kc * jnp.exp(g_end[:, :, None] - G), vc)
        return S, o

    S, o = lax.scan(step, s0, (chunks(q), chunks(log_f), chunks(k), chunks(v)))
    return o.transpose(1, 0, 3, 2, 4).reshape(B, T, H, -1), S


def _complex_affine_combine(e1, e2):
    ar1, ai1, br1, bi1 = e1
    ar2, ai2, br2, bi2 = e2
    return (ar1 * ar2 - ai1 * ai2, ar1 * ai2 + ai1 * ar2,
            ar2 * br1 - ai2 * bi1 + br2, ar2 * bi1 + ai2 * br1 + bi2)


def s5_layer(u, a_re, a_im, log_dt, b_re, b_im, c_re, c_im, d, w_glu, h0_re, h0_im):
    B, T, _ = u.shape
    uf = u.astype(F32)
    ug = uf.reshape(B, T, S5_GROUPS, S5_GROUP)
    a_re, a_im = a_re.astype(F32), a_im.astype(F32)
    step = jnp.exp(log_dt.astype(F32))[:, None]
    mag = jnp.exp(a_re * step)
    ab_re, ab_im = mag * jnp.cos(a_im * step), mag * jnp.sin(a_im * step)
    den = a_re * a_re + a_im * a_im
    z_re = ((ab_re - 1.0) * a_re + ab_im * a_im) / den
    z_im = (ab_im * a_re - (ab_re - 1.0) * a_im) / den
    b_re, b_im = b_re.astype(F32), b_im.astype(F32)
    bb_re = z_re[..., None] * b_re - z_im[..., None] * b_im
    bb_im = z_re[..., None] * b_im + z_im[..., None] * b_re
    x_re = jnp.einsum("gpn,btgn->btgp", bb_re, ug)
    x_im = jnp.einsum("gpn,btgn->btgp", bb_im, ug)
    a_re_t = jnp.broadcast_to(ab_re, (1, T) + ab_re.shape)
    a_im_t = jnp.broadcast_to(ab_im, (1, T) + ab_im.shape)
    cum_re, cum_im, h_re, h_im = lax.associative_scan(_complex_affine_combine, (a_re_t, a_im_t, x_re, x_im), axis=1)
    h0r, h0i = h0_re[:, None], h0_im[:, None]
    h_re, h_im = h_re + cum_re * h0r - cum_im * h0i, h_im + cum_re * h0i + cum_im * h0r
    y = jnp.einsum("gnp,btgp->btgn", c_re.astype(F32), h_re) - jnp.einsum("gnp,btgp->btgn", c_im.astype(F32), h_im)
    y = jax.nn.gelu(y.reshape(B, T, S5_WIDTH) + d.astype(F32) * uf)
    y = y * jax.nn.sigmoid(y @ w_glu.astype(F32))
    return y.astype(u.dtype), h_re[:, -1], h_im[:, -1]


def compress(rows, pe, w1, w2):
    B, L = rows.shape[:2]
    r = CMP_LEN // CMP_STRIDE
    nc = L // CMP_STRIDE
    nb = nc - r + 1
    ch = rows[:, :nc * CMP_STRIDE].reshape(B, nc, CMP_STRIDE, NSA_KV, HEAD_DIM)
    blk = jnp.concatenate([ch[:, j:j + nb] for j in range(r)], axis=2) + pe[:, None, :]
    flat = blk.transpose(0, 1, 3, 2, 4).reshape(B, nb, NSA_KV, CMP_LEN * HEAD_DIM)
    return jax.nn.gelu(flat @ w1) @ w2


def attend_dense(qg, k, v, dist, valid, rel_bias):
    T, N = dist.shape
    bias = rel_bias.astype(F32)[rel_bucket(dist)].reshape(T, N, NSA_KV, NSA_REP).transpose(2, 3, 0, 1)
    s = jnp.einsum("btgrd,bngd->bgrtn", qg, k, preferred_element_type=F32) + bias
    s = jnp.where(valid, s, NEG)
    p = jax.nn.softmax(s, axis=-1) * jnp.any(valid, axis=-1)[:, None].astype(F32)
    o = jnp.einsum("bgrtn,bngd->btgrd", p.astype(v.dtype), v)
    return o, p


def select_blocks(imp, qpos, n_sel):
    nb = imp.shape[-1]
    cs = jnp.arange(nb) * CMP_STRIDE
    ss = jnp.arange(n_sel) * SEL_BLOCK
    overlap = ((cs[:, None] < ss[None, :] + SEL_BLOCK) & (cs[:, None] + CMP_LEN > ss[None, :])).astype(F32)
    score = jnp.einsum("bgtn,nj->bgtj", imp, overlap)
    cur = (qpos // SEL_BLOCK)[:, None]
    j = jnp.arange(n_sel)[None, :]
    forced = (j == 0) | (j == cur) | (j == cur - 1)
    future = ss[None, :] > qpos[:, None]
    score = jnp.where(forced, FORCE, jnp.where(future, -FORCE, score))
    _, idx = lax.top_k(score, min(SEL_TOPN, n_sel))
    return idx


def sel_attend_chunk(qg, qpos, idx, kb, vb, rel_bias):
    B = qg.shape[0]
    bi = jnp.arange(B)[:, None, None, None]
    gi = jnp.arange(NSA_KV)[None, :, None, None]
    ks = kb[bi, gi, idx]
    vs = vb[bi, gi, idx]
    kpos = idx[..., None] * SEL_BLOCK + jnp.arange(SEL_BLOCK)
    dist = qpos[None, None, :, None, None] - kpos
    tbl = rel_bias.astype(F32).reshape(N_BUCKETS * NSA_KV, NSA_REP)
    bias = tbl[rel_bucket(dist) * NSA_KV + gi[..., None]]
    s = jnp.einsum("bqgrd,bgqknd->bgqknr", qg, ks, preferred_element_type=F32) + bias
    s = jnp.where((dist >= 0)[..., None], s, NEG)
    p = jax.nn.softmax(s, axis=(3, 4))
    return jnp.einsum("bgqknr,bgqknd->bqgrd", p.astype(vs.dtype), vs)


def sel_attend(qg, qpos, idx, kb, vb, rel_bias):
    B, T = qg.shape[:2]
    qc = math.gcd(T, SEL_QBLK)
    n = T // qc
    K = idx.shape[-1]
    q_ch = qg.reshape(B, n, qc, NSA_KV, NSA_REP, HEAD_DIM).swapaxes(0, 1)
    p_ch = qpos.reshape(n, qc)
    i_ch = idx.reshape(B, NSA_KV, n, qc, K).transpose(2, 0, 1, 3, 4)
    o = lax.map(lambda c: sel_attend_chunk(c[0], c[1], c[2], kb, vb, rel_bias), (q_ch, p_ch, i_ch))
    return o.swapaxes(0, 1).reshape(B, T, NSA_KV, NSA_REP, HEAD_DIM)


def win_attend_prompt(qg, k, v, rel_bias):
    B, T = qg.shape[:2]
    wq = math.gcd(T, WIN_QBLK)
    nq = T // wq
    nprev = -(-WINDOW // wq)
    pad = nprev * wq
    padw = ((0, 0), (pad, 0), (0, 0), (0, 0))
    kp = jnp.pad(k, padw).reshape(B, nq + nprev, wq, NSA_KV, HEAD_DIM)
    vp = jnp.pad(v, padw).reshape(B, nq + nprev, wq, NSA_KV, HEAD_DIM)
    kw = jnp.concatenate([kp[:, j:j + nq] for j in range(nprev + 1)], axis=2)
    vw = jnp.concatenate([vp[:, j:j + nq] for j in range(nprev + 1)], axis=2)
    kw_len = (nprev + 1) * wq
    i = jnp.arange(wq)[:, None]
    jj = jnp.arange(kw_len)[None, :]
    dist = i + pad - jj
    kpos = jnp.arange(nq)[:, None] * wq - pad + jj
    valid = ((dist >= 0) & (dist < WINDOW))[None] & (kpos >= 0)[:, None, :]
    bias = rel_bias.astype(F32)[rel_bucket(dist)].reshape(wq, kw_len, NSA_KV, NSA_REP).transpose(2, 3, 0, 1)
    qb = qg.reshape(B, nq, wq, NSA_KV, NSA_REP, HEAD_DIM)
    s = jnp.einsum("bqigrd,bqjgd->bqgrij", qb, kw, preferred_element_type=F32) + bias
    s = jnp.where(valid[:, None, None], s, NEG)
    p = jax.nn.softmax(s, axis=-1)
    o = jnp.einsum("bqgrij,bqjgd->bqigrd", p.astype(vw.dtype), vw)
    return o.reshape(B, T, NSA_KV, NSA_REP, HEAD_DIM)


def layer_forward(x, past, lb, rel_bias, w):
    B, T, _ = x.shape
    dt = x.dtype
    p0 = 0 if past is None else past["kv"].shape[1]
    qpos = p0 + jnp.arange(T, dtype=jnp.int32)
    xn = rms_norm(x, w["norm_mix"])
    (hq, hf, hi, hg, su, nq, kc, vc, ksl, vsl, kwn, vwn, ngate, ga, gb, gc) = jnp.split(xn @ w["w_in"], IN_OFFSETS, axis=-1)

    fp = hf.astype(F32)
    log_f = jnp.logaddexp(jnp.log(jnp.maximum(lb, LB_FLOOR)), jnp.log1p(-lb) + jax.nn.log_sigmoid(fp))
    k_in = (1.0 - lb) * jax.nn.sigmoid(-fp)
    hs = (B, T, HG_HEADS, HG_DIM)
    s0 = jnp.zeros((B, HG_HEADS, HG_DIM, HG_DIM), F32) if past is None else past["hgrn"].astype(F32)
    o_hg, s_hg = hgrn2_scan(jax.nn.silu(hq.astype(F32)).reshape(hs), log_f.reshape(hs), k_in.reshape(hs), hi.astype(F32).reshape(hs), s0)
    o_a = (rms_norm(o_hg, w["hg_out_norm"]) * jax.nn.silu(hg.astype(F32)).reshape(hs)).reshape(B, T, HG_WIDTH).astype(dt)

    if past is None:
        h0_re = jnp.zeros((B, S5_GROUPS, S5_STATE), F32)
        h0_im = jnp.zeros((B, S5_GROUPS, S5_STATE), F32)
    else:
        h0_re, h0_im = past["s5_re"].astype(F32), past["s5_im"].astype(F32)
    o_b, s5_re, s5_im = s5_layer(su, w["s5_a_re"], w["s5_a_im"], w["s5_log_dt"], w["s5_b_re"], w["s5_b_im"],
                                 w["s5_c_re"], w["s5_c_im"], w["s5_d"], w["s5_w_glu"], h0_re, h0_im)

    qkn = w["nsa_qk_norm"]
    kvs = (B, T, NSA_KV, HEAD_DIM)
    q = (rms_norm(nq.reshape(B, T, NSA_HEADS, HEAD_DIM), qkn[0]) * HEAD_DIM ** -0.5).reshape(B, T, NSA_KV, NSA_REP, HEAD_DIM)
    k_sel = rms_norm(ksl.reshape(kvs), qkn[2])
    k_win = rms_norm(kwn.reshape(kvs), qkn[3])
    v_win = vwn.reshape(kvs)
    new_kv = jnp.stack([kc.reshape(kvs), vc.reshape(kvs), k_sel, vsl.reshape(kvs)], axis=2)
    full = new_kv if past is None else jnp.concatenate([past["kv"].astype(dt), new_kv], axis=1)
    L = full.shape[1]
    k_cmp = rms_norm(compress(full[:, :, 0], w["cmp_pe"][0], w["cmp_w1"][0], w["cmp_w2"][0]), qkn[1])
    v_cmp = compress(full[:, :, 1], w["cmp_pe"][1], w["cmp_w1"][1], w["cmp_w2"][1])
    kend = jnp.arange(k_cmp.shape[1]) * CMP_STRIDE + CMP_LEN - 1
    dist_c = qpos[:, None] - kend[None, :]
    o_cmp, p_cmp = attend_dense(q, k_cmp, v_cmp, dist_c, dist_c >= 0, rel_bias)
    n_sel = -(-L // SEL_BLOCK)
    idx = select_blocks(p_cmp.sum(axis=2), qpos, n_sel)
    sel = jnp.pad(full[:, :, 2:4], ((0, 0), (0, n_sel * SEL_BLOCK - L), (0, 0), (0, 0), (0, 0)))
    kb = sel[:, :, 0].reshape(B, n_sel, SEL_BLOCK, NSA_KV, HEAD_DIM).transpose(0, 3, 1, 2, 4)
    vb = sel[:, :, 1].reshape(B, n_sel, SEL_BLOCK, NSA_KV, HEAD_DIM).transpose(0, 3, 1, 2, 4)
    o_sel = sel_attend(q, qpos, idx, kb, vb, rel_bias)
    kv_win = jnp.stack([k_win, v_win], axis=2)
    if past is None:
        o_win = win_attend_prompt(q, k_win, v_win, rel_bias)
        new_win = kv_win[:, T - min(WINDOW, T):]
    else:
        lw = past["win"].shape[1]
        buf = jnp.concatenate([past["win"].astype(dt), kv_win], axis=1)
        kpos = p0 - lw + jnp.arange(lw + T, dtype=jnp.int32)
        dist_w = qpos[:, None] - kpos[None, :]
        o_win, _ = attend_dense(q, buf[:, :, 0], buf[:, :, 1], dist_w, (dist_w >= 0) & (dist_w < WINDOW), rel_bias)
        new_win = buf[:, T:]
    g = jax.nn.sigmoid(ngate.astype(F32)).reshape(B, T, 3, NSA_KV, NSA_REP)[..., None]
    o_c = (g[:, :, 0] * o_cmp + g[:, :, 1] * o_sel + g[:, :, 2] * o_win).reshape(B, T, NSA_WIDTH).astype(dt)

    wb = w["w_branch"]
    m = jax.nn.sigmoid(ga) * (o_a @ wb[0]) + jax.nn.sigmoid(gb) * (o_b @ wb[1]) + jax.nn.sigmoid(gc) * (o_c @ wb[2])
    x = x + m @ w["w_out"]

    gate, up = jnp.split(rms_norm(x, w["norm_ffn"]) @ w["w_gate_up"], [D_FF], axis=-1)
    x = x + (jax.nn.silu(gate) * up) @ w["w_down"]
    return x, (new_kv, new_win.astype(dt), s_hg.astype(dt), s5_re.astype(dt), s5_im.astype(dt))


def setup_inputs(seed: int = 0) -> dict:
    key = jax.random.key(seed)
    ks = iter(jax.random.split(key, 40))

    def nrm(shape, scale=1.0):
        return scale * jax.random.normal(next(ks), shape, F32)

    n_pages = PAST_LEN // PAGE_SIZE
    n_used = DEC_BATCH * n_pages
    n_phys = n_used + -(-n_used // 4)
    win_len = min(WINDOW, PAST_LEN)
    x_prompt = nrm((BATCH, SEQ, D_MODEL))
    x_sample = nrm((DEC_BATCH, DEC_SEQ, D_MODEL))
    cache_nsa_kv = nrm((DEPTH, n_phys, PAGE_SIZE, 4, NSA_KV, HEAD_DIM))
    cache_win_kv = nrm((DEPTH, DEC_BATCH, win_len, 2, NSA_KV, HEAD_DIM))
    state_hgrn = nrm((DEPTH, DEC_BATCH, HG_HEADS, HG_DIM, HG_DIM), 0.5)
    state_s5_re = nrm((DEPTH, DEC_BATCH, S5_GROUPS, S5_STATE), 0.1)
    state_s5_im = nrm((DEPTH, DEC_BATCH, S5_GROUPS, S5_STATE), 0.1)
    page_table = jax.random.permutation(next(ks), n_phys)[:n_used].reshape(DEC_BATCH, n_pages).astype(jnp.int32)
    norm_mix = 1.0 + nrm((DEPTH, D_MODEL), 0.02)
    w_in = nrm((DEPTH, D_MODEL, N_IN), D_MODEL ** -0.5)
    hg_lb_logits = nrm((DEPTH, HG_WIDTH), 0.1)
    hg_out_norm = 1.0 + nrm((DEPTH, HG_DIM), 0.02)
    s5_a_re = -0.5 + nrm((DEPTH, S5_GROUPS, S5_STATE), 0.01)
    s5_a_im = jnp.pi * jnp.arange(S5_STATE, dtype=F32) + nrm((DEPTH, S5_GROUPS, S5_STATE), 0.01)
    s5_log_dt = jax.random.uniform(next(ks), (DEPTH, S5_GROUPS), F32, math.log(1e-3), math.log(1e-1))
    s5_b_re = nrm((DEPTH, S5_GROUPS, S5_STATE, S5_GROUP), (2 * S5_GROUP) ** -0.5)
    s5_b_im = nrm((DEPTH, S5_GROUPS, S5_STATE, S5_GROUP), (2 * S5_GROUP) ** -0.5)
    s5_c_re = nrm((DEPTH, S5_GROUPS, S5_GROUP, S5_STATE), S5_STATE ** -0.5)
    s5_c_im = nrm((DEPTH, S5_GROUPS, S5_GROUP, S5_STATE), S5_STATE ** -0.5)
    s5_d = nrm((DEPTH, S5_WIDTH))
    s5_w_glu = nrm((DEPTH, S5_WIDTH, S5_WIDTH), S5_WIDTH ** -0.5)
    nsa_qk_norm = 1.0 + nrm((DEPTH, 4, HEAD_DIM), 0.02)
    cmp_pe = nrm((DEPTH, 2, CMP_LEN, HEAD_DIM), 0.02)
    cmp_w1 = nrm((DEPTH, 2, CMP_LEN * HEAD_DIM, CMP_HIDDEN), (CMP_LEN * HEAD_DIM) ** -0.5)
    cmp_w2 = nrm((DEPTH, 2, CMP_HIDDEN, HEAD_DIM), CMP_HIDDEN ** -0.5)
    rel_bias = nrm((N_BUCKETS, NSA_HEADS), 0.5)
    w_branch = nrm((DEPTH, N_BRANCH, BR_WIDTH, D_MODEL), BR_WIDTH ** -0.5)
    w_out = nrm((DEPTH, D_MODEL, D_MODEL), D_MODEL ** -0.5)
    norm_ffn = 1.0 + nrm((DEPTH, D_MODEL), 0.02)
    w_gate_up = nrm((DEPTH, D_MODEL, 2 * D_FF), D_MODEL ** -0.5)
    w_down = nrm((DEPTH, D_FF, D_MODEL), D_FF ** -0.5)
    return {"x_prompt": x_prompt, "x_sample": x_sample, "cache_nsa_kv": cache_nsa_kv, "cache_win_kv": cache_win_kv,
            "state_hgrn": state_hgrn, "state_s5_re": state_s5_re, "state_s5_im": state_s5_im, "page_table": page_table,
            "norm_mix": norm_mix, "w_in": w_in, "hg_lb_logits": hg_lb_logits, "hg_out_norm": hg_out_norm,
            "s5_a_re": s5_a_re, "s5_a_im": s5_a_im, "s5_log_dt": s5_log_dt, "s5_b_re": s5_b_re, "s5_b_im": s5_b_im,
            "s5_c_re": s5_c_re, "s5_c_im": s5_c_im, "s5_d": s5_d, "s5_w_glu": s5_w_glu,
            "nsa_qk_norm": nsa_qk_norm, "cmp_pe": cmp_pe, "cmp_w1": cmp_w1, "cmp_w2": cmp_w2, "rel_bias": rel_bias,
            "w_branch": w_branch, "w_out": w_out, "norm_ffn": norm_ffn, "w_gate_up": w_gate_up, "w_down": w_down}


def reference(x_prompt, x_sample, cache_nsa_kv, cache_win_kv, state_hgrn, state_s5_re, state_s5_im, page_table,
              norm_mix, w_in, hg_lb_logits, hg_out_norm, s5_a_re, s5_a_im, s5_log_dt, s5_b_re, s5_b_im,
              s5_c_re, s5_c_im, s5_d, s5_w_glu, nsa_qk_norm, cmp_pe, cmp_w1, cmp_w2, rel_bias,
              w_branch, w_out, norm_ffn, w_gate_up, w_down):
    lb_sm = jax.nn.softmax(hg_lb_logits.astype(F32), axis=0)
    lower_bounds = jnp.cumsum(lb_sm, axis=0) - lb_sm[0]
    dec_b = x_sample.shape[0]
    n_past = page_table.shape[1] * PAGE_SIZE
    hp, hs = x_prompt, x_sample
    st_p, st_s = [], []
    for l in range(DEPTH):
        w = {"norm_mix": norm_mix[l], "w_in": w_in[l], "hg_out_norm": hg_out_norm[l],
             "s5_a_re": s5_a_re[l], "s5_a_im": s5_a_im[l], "s5_log_dt": s5_log_dt[l], "s5_b_re": s5_b_re[l],
             "s5_b_im": s5_b_im[l], "s5_c_re": s5_c_re[l], "s5_c_im": s5_c_im[l], "s5_d": s5_d[l], "s5_w_glu": s5_w_glu[l],
             "nsa_qk_norm": nsa_qk_norm[l], "cmp_pe": cmp_pe[l], "cmp_w1": cmp_w1[l], "cmp_w2": cmp_w2[l],
             "w_branch": w_branch[l], "w_out": w_out[l], "norm_ffn": norm_ffn[l], "w_gate_up": w_gate_up[l], "w_down": w_down[l]}
        past = {"kv": cache_nsa_kv[l, page_table].reshape(dec_b, n_past, 4, NSA_KV, HEAD_DIM),
                "win": cache_win_kv[l], "hgrn": state_hgrn[l], "s5_re": state_s5_re[l], "s5_im": state_s5_im[l]}
        hp, sp = layer_forward(hp, None, lower_bounds[l], rel_bias, w)
        hs, ss = layer_forward(hs, past, lower_bounds[l], rel_bias, w)
        st_p.append(sp)
        st_s.append(ss)
    kv_p = jnp.stack([s[0] for s in st_p])
    kv_s = jnp.stack([s[0] for s in st_s])
    win_p = jnp.stack([s[1] for s in st_p])
    win_s = jnp.stack([s[1] for s in st_s])
    hg_p = jnp.stack([s[2] for s in st_p])
    hg_s = jnp.stack([s[2] for s in st_s])
    s5r_p = jnp.stack([s[3] for s in st_p])
    s5i_p = jnp.stack([s[4] for s in st_p])
    s5r_s = jnp.stack([s[3] for s in st_s])
    s5i_s = jnp.stack([s[4] for s in st_s])
    return (hp, hs, kv_p, kv_s, win_p, win_s, hg_p, hg_s, s5r_p, s5i_p, s5r_s, s5i_s)
```

```python
import functools
import math

import numpy as np
import jax
import jax.numpy as jnp
from jax import lax
from jax.experimental import pallas as pl
from jax.experimental.pallas import tpu as pltpu

F32 = jnp.float32
BF16 = jnp.bfloat16

D_MODEL = 1024
HG_HEADS = 4
HG_DIM = 128
HG_WIDTH = HG_HEADS * HG_DIM
HG_CHUNK = 64
LB_FLOOR = 1e-30
S5_GROUP = 16
S5_GROUPS = 32
S5_WIDTH = S5_GROUP * S5_GROUPS
S5_STATE = 64
S5_CH = S5_GROUPS * S5_STATE
NSA_HEADS = 8
NSA_KV = 2
NSA_REP = NSA_HEADS // NSA_KV
HEAD_DIM = 64
NSA_WIDTH = NSA_HEADS * HEAD_DIM
KV_W = NSA_KV * HEAD_DIM
CMP_LEN = 32
CMP_STRIDE = 16
CMP_HIDDEN = 128
SEL_BLOCK = 64
SEL_TOPN = 16
WINDOW = 512
N_BUCKETS = 32
MAX_DIST = 128
PAGE_SIZE = 128
D_FF = 2816
NEG = -1e30
FORCE = 1e9
EPS = 1e-6

COL_GA, COL_GB, COL_GC = 0, 1024, 2048
COL_HQ, COL_HF, COL_HI, COL_HG = 3072, 3584, 4096, 4608
COL_SU, COL_NQ, COL_KV4, COL_WKV, COL_NGATE = 5120, 5632, 6144, 6656, 6912
N_IN_PAD = 7040
N_IN = 6936

LANE = 128
VMEM_LIMIT = 56 * 1024 * 1024


def _cp(*sem):
    return pltpu.CompilerParams(dimension_semantics=sem, vmem_limit_bytes=VMEM_LIMIT)


def _dot(a, b):
    return jnp.dot(a.astype(BF16), b.astype(BF16), preferred_element_type=F32)


def _dot_nt(a, b):
    return lax.dot_general(a.astype(BF16), b.astype(BF16), (((1,), (1,)), ((), ())), preferred_element_type=F32)


def _dot_tn(a, b):
    return lax.dot_general(a.astype(BF16), b.astype(BF16), (((0,), (0,)), ((), ())), preferred_element_type=F32)


def _split2(x):
    hi = x.astype(BF16)
    lo = (x - hi.astype(F32)).astype(BF16)
    return hi, lo


def _dot2(x, w):
    hi, lo = _split2(x)
    return jnp.dot(hi, w, preferred_element_type=F32) + jnp.dot(lo, w, preferred_element_type=F32)


def _silu(x):
    return x * jax.nn.sigmoid(x)


def _gelu_tanh(x):
    return 0.5 * x * (1.0 + jnp.tanh(math.sqrt(2.0 / math.pi) * (x + 0.044715 * (x * x * x))))


def _rms_mm_kernel(x_ref, g_ref, w_ref, o_ref, xn_ref):
    @pl.when(pl.program_id(1) == 0)
    def _():
        x = x_ref[...]
        ms = jnp.mean(x * x, axis=-1, keepdims=True)
        xn_ref[...] = (x * lax.rsqrt(ms + EPS) * g_ref[...]).astype(BF16)

    o_ref[...] = jnp.dot(xn_ref[...], w_ref[...], preferred_element_type=F32)


def rms_matmul(x, g, w, tm, tn):
    m, k = x.shape
    n = w.shape[1]
    return pl.pallas_call(
        _rms_mm_kernel,
        grid=(m // tm, n // tn),
        in_specs=[pl.BlockSpec((tm, k), lambda i, j: (i, 0)),
                  pl.BlockSpec((1, k), lambda i, j: (0, 0)),
                  pl.BlockSpec((k, tn), lambda i, j: (0, j))],
        out_specs=pl.BlockSpec((tm, tn), lambda i, j: (i, j)),
        out_shape=jax.ShapeDtypeStruct((m, n), F32),
        scratch_shapes=[pltpu.VMEM((tm, k), BF16)],
        compiler_params=_cp("parallel", "arbitrary"),
        name="rms_in_proj",
    )(x, g, w)


def _mm_kernel(x_ref, w_ref, o_ref):
    o_ref[...] = jnp.dot(x_ref[...], w_ref[...], preferred_element_type=F32)


def matmul(x, w, tm):
    m, k = x.shape
    n = w.shape[1]
    return pl.pallas_call(
        _mm_kernel,
        grid=(m // tm,),
        in_specs=[pl.BlockSpec((tm, k), lambda i: (i, 0)),
                  pl.BlockSpec((k, n), lambda i: (0, 0))],
        out_specs=pl.BlockSpec((tm, n), lambda i: (i, 0)),
        out_shape=jax.ShapeDtypeStruct((m, n), F32),
        compiler_params=_cp("parallel"),
        name="cmp_matmul",
    )(x, w)


def _merge_kernel(x_ref, ga_ref, gb_ref, gc_ref, oa_ref, ob_ref, oc_ref, wb_ref, wo_ref, o_ref):
    m = jax.nn.sigmoid(ga_ref[...]) * jnp.dot(oa_ref[...], wb_ref[0], preferred_element_type=F32)
    m = m + jax.nn.sigmoid(gb_ref[...]) * jnp.dot(ob_ref[...], wb_ref[1], preferred_element_type=F32)
    m = m + jax.nn.sigmoid(gc_ref[...]) * jnp.dot(oc_ref[...], wb_ref[2], preferred_element_type=F32)
    o_ref[...] = x_ref[...] + jnp.dot(m.astype(BF16), wo_ref[...], preferred_element_type=F32)


def merge(x, y, oa, ob, oc, wb, wo, tm):
    m, d = x.shape
    return pl.pallas_call(
        _merge_kernel,
        grid=(m // tm,),
        in_specs=[pl.BlockSpec((tm, d), lambda i: (i, 0)),
                  pl.BlockSpec((tm, d), lambda i: (i, COL_GA // D_MODEL)),
                  pl.BlockSpec((tm, d), lambda i: (i, COL_GB // D_MODEL)),
                  pl.BlockSpec((tm, d), lambda i: (i, COL_GC // D_MODEL)),
                  pl.BlockSpec((tm, 512), lambda i: (i, 0)),
                  pl.BlockSpec((tm, 512), lambda i: (i, 0)),
                  pl.BlockSpec((tm, 512), lambda i: (i, 0)),
                  pl.BlockSpec((3, 512, d), lambda i: (0, 0, 0)),
                  pl.BlockSpec((d, d), lambda i: (0, 0))],
        out_specs=pl.BlockSpec((tm, d), lambda i: (i, 0)),
        out_shape=jax.ShapeDtypeStruct((m, d), F32),
        compiler_params=_cp("parallel"),
        name="merge_out_proj",
    )(x, y, y, y, oa, ob, oc, wb, wo)


def _ffn_kernel(x_ref, g_ref, wg_ref, wu_ref, wd_ref, o_ref, xn_ref):
    @pl.when(pl.program_id(1) == 0)
    def _():
        x = x_ref[...]
        ms = jnp.mean(x * x, axis=-1, keepdims=True)
        xn_ref[...] = (x * lax.rsqrt(ms + EPS) * g_ref[...]).astype(BF16)
        o_ref[...] = x

    xn = xn_ref[...]
    gate = jnp.dot(xn, wg_ref[...], preferred_element_type=F32)
    up = jnp.dot(xn, wu_ref[...], preferred_element_type=F32)
    h = (_silu(gate) * up).astype(BF16)
    o_ref[...] += jnp.dot(h, wd_ref[...], preferred_element_type=F32)


def ffn(x, g, wgu, wd, tm, tf):
    m, d = x.shape
    nf = D_FF // tf
    return pl.pallas_call(
        _ffn_kernel,
        grid=(m // tm, nf),
        in_specs=[pl.BlockSpec((tm, d), lambda i, f: (i, 0)),
                  pl.BlockSpec((1, d), lambda i, f: (0, 0)),
                  pl.BlockSpec((d, tf), lambda i, f: (0, f)),
                  pl.BlockSpec((d, tf), lambda i, f: (0, nf + f)),
                  pl.BlockSpec((tf, d), lambda i, f: (f, 0))],
        out_specs=pl.BlockSpec((tm, d), lambda i, f: (i, 0)),
        out_shape=jax.ShapeDtypeStruct((m, d), F32),
        scratch_shapes=[pltpu.VMEM((tm, d), BF16)],
        compiler_params=_cp("parallel", "arbitrary"),
        name="swiglu_ffn",
    )(x, g, wgu, wgu, wd)


def _hgrn_levels(c):
    return [m for m in (32, 16, 8, 4, 2, 1) if m < c]


def _hgrn_consts(c):
    mats = [np.tril(np.ones((c, c), np.float32))]
    r = np.arange(c)
    for m in _hgrn_levels(c):
        pos = r % (2 * m)
        mid = (r // (2 * m)) * 2 * m + m
        up = np.zeros((c, c), np.float32)
        lo = np.zeros((c, c), np.float32)
        for t in range(c):
            if pos[t] >= m:
                up[t, mid[t]:t + 1] = 1.0
            else:
                lo[t, t + 1:mid[t]] = 1.0
        mats += [up, lo]
    return np.concatenate(mats, axis=0)


def _hgrn_kernel(hq_ref, hf_ref, hi_ref, hg_ref, lb_ref, gn_ref, s0_ref, mst_ref, o_ref, sout_ref, st_ref,
                 *, c, nchunk):
    tb = pl.program_id(2)

    @pl.when(tb == 0)
    def _():
        st_ref[...] = s0_ref[0, 0].T

    levels = _hgrn_levels(c)
    lb = lb_ref[...]
    log_lb = jnp.log(jnp.maximum(lb, LB_FLOOR))
    log_1m = jnp.log1p(-lb)
    one_m = 1.0 - lb
    gn = gn_ref[...]
    mst = mst_ref[...]
    ti = lax.broadcasted_iota(jnp.int32, (c, c), 0)
    si = lax.broadcasted_iota(jnp.int32, (c, c), 1)
    txs = jnp.bitwise_xor(ti, si)
    lower = ti > si

    for ch in range(nchunk):
        rows = pl.ds(ch * c, c)
        fp = hf_ref[rows, :]
        hq = hq_ref[rows, :]
        v = hi_ref[rows, :]
        hg = hg_ref[rows, :]
        log_sig = -(jnp.maximum(-fp, 0.0) + jnp.log1p(jnp.exp(-jnp.abs(fp))))
        b = log_1m + log_sig
        g = jnp.maximum(log_lb, b) + jnp.log1p(jnp.exp(-jnp.abs(log_lb - b)))
        kin = one_m * jax.nn.sigmoid(-fp)
        q = _silu(hq)

        g1 = g.astype(BF16)
        r1 = g - g1.astype(F32)
        g2 = r1.astype(BF16)
        g3 = (r1 - g2.astype(F32)).astype(BF16)
        gs = jnp.concatenate([g1, g2, g3], axis=1)
        rr = jnp.dot(mst, gs, preferred_element_type=F32)
        rr = rr[:, 0:LANE] + rr[:, LANE:2 * LANE] + rr[:, 2 * LANE:3 * LANE]
        gcum = rr[0:c]

        vb = v.astype(BF16)
        att = jnp.where(ti == si, _dot_nt(q, kin), 0.0)
        for li, m in enumerate(levels):
            dq = rr[(1 + 2 * li) * c:(2 + 2 * li) * c]
            ek = rr[(2 + 2 * li) * c:(3 + 2 * li) * c]
            pair = _dot_nt(q * jnp.exp(dq), kin * jnp.exp(ek))
            sel = lower & (jnp.right_shift(txs, int(math.log2(m))) == 1)
            att = jnp.where(sel, pair, att)
        st = st_ref[...]
        o = _dot(att, vb) + _dot_nt(q * jnp.exp(gcum), st)
        g_end = gcum[c - 1:c, :]
        kd = kin * jnp.exp(g_end - gcum)
        st_ref[...] = st * jnp.exp(g_end) + _dot_tn(vb, kd)

        ms = jnp.mean(o * o, axis=-1, keepdims=True)
        on = o * lax.rsqrt(ms + EPS) * gn
        o_ref[rows, :] = (on * _silu(hg)).astype(BF16)

    @pl.when(tb == pl.num_programs(2) - 1)
    def _():
        sout_ref[0, 0] = st_ref[...].T


def hgrn(y, lb, gn, s0, bsz, t, c, tbk):
    ntb = t // tbk
    mst = jnp.asarray(_hgrn_consts(c), BF16)

    def col(base):
        return pl.BlockSpec((tbk, HG_DIM), lambda b, h, k: (b * ntb + k, base // HG_DIM + h))

    return pl.pallas_call(
        functools.partial(_hgrn_kernel, c=c, nchunk=tbk // c),
        grid=(bsz, HG_HEADS, ntb),
        in_specs=[col(COL_HQ), col(COL_HF), col(COL_HI), col(COL_HG),
                  pl.BlockSpec((1, HG_DIM), lambda b, h, k: (0, h)),
                  pl.BlockSpec((1, HG_DIM), lambda b, h, k: (0, 0)),
                  pl.BlockSpec((1, 1, HG_DIM, HG_DIM), lambda b, h, k: (b, h, 0, 0)),
                  pl.BlockSpec(mst.shape, lambda b, h, k: (0, 0))],
        out_specs=[pl.BlockSpec((tbk, HG_DIM), lambda b, h, k: (b * ntb + k, h)),
                   pl.BlockSpec((1, 1, HG_DIM, HG_DIM), lambda b, h, k: (b, h, 0, 0))],
        out_shape=[jax.ShapeDtypeStruct((bsz * t, HG_WIDTH), BF16),
                   jax.ShapeDtypeStruct((bsz, HG_HEADS, HG_DIM, HG_DIM), F32)],
        scratch_shapes=[pltpu.VMEM((HG_DIM, HG_DIM), F32)],
        compiler_params=_cp("parallel", "parallel", "arbitrary"),
        name="hgrn2_scan",
    )(y, y, y, y, lb, gn, s0, mst)


def _shift_rows(x, d):
    n = x.shape[0]
    rolled = pltpu.roll(x, d, axis=0)
    rid = lax.broadcasted_iota(jnp.int32, x.shape, 0)
    return jnp.where(rid >= d, rolled, 0.0)


def _s5_kernel(u_ref, bre_ref, bim_ref, adr_ref, adi_ref, atr_ref, ati_ref, h0r_ref, h0i_ref,
               cre_ref, cim_ref, d_ref, wg_ref, o_ref, hr_out, hi_out, cr_ref, ci_ref, *, ts):
    k = pl.program_id(1)

    @pl.when(k == 0)
    def _():
        cr_ref[...] = h0r_ref[0]
        ci_ref[...] = h0i_ref[0]

    u = u_ref[...]
    ub = u.astype(BF16)
    xr = jnp.dot(ub, bre_ref[...], preferred_element_type=F32)
    xi = jnp.dot(ub, bim_ref[...], preferred_element_type=F32)
    lev = 0
    d = 1
    while d < ts:
        ar = adr_ref[lev:lev + 1, :]
        ai = adi_ref[lev:lev + 1, :]
        sr = _shift_rows(xr, d)
        si = _shift_rows(xi, d)
        xr, xi = xr + ar * sr - ai * si, xi + ar * si + ai * sr
        d *= 2
        lev += 1
    cr = cr_ref[...]
    ci = ci_ref[...]
    atr = atr_ref[...]
    ati = ati_ref[...]
    hr = xr + atr * cr - ati * ci
    hi = xi + atr * ci + ati * cr
    cr_ref[...] = hr[ts - 1:ts, :]
    ci_ref[...] = hi[ts - 1:ts, :]
    y = jnp.dot(hr.astype(BF16), cre_ref[...], preferred_element_type=F32) \
        - jnp.dot(hi.astype(BF16), cim_ref[...], preferred_element_type=F32)
    y = _gelu_tanh(y + d_ref[...] * u)
    o_ref[...] = (y * jax.nn.sigmoid(jnp.dot(y.astype(BF16), wg_ref[...], preferred_element_type=F32))).astype(BF16)

    @pl.when(k == pl.num_programs(1) - 1)
    def _():
        hr_out[0] = hr[ts - 1:ts, :]
        hi_out[0] = hi[ts - 1:ts, :]


def s5(y, prm, h0r, h0i, bsz, t, ts):
    nts = t // ts
    nlev = prm["adr"].shape[0]
    const = lambda shape: pl.BlockSpec(shape, lambda b, k: tuple(0 for _ in shape))
    state = pl.BlockSpec((1, 1, S5_CH), lambda b, k: (b, 0, 0))
    return pl.pallas_call(
        functools.partial(_s5_kernel, ts=ts),
        grid=(bsz, nts),
        in_specs=[pl.BlockSpec((ts, S5_WIDTH), lambda b, k: (b * nts + k, COL_SU // S5_WIDTH)),
                  const((S5_WIDTH, S5_CH)), const((S5_WIDTH, S5_CH)),
                  const((nlev, S5_CH)), const((nlev, S5_CH)),
                  const((ts, S5_CH)), const((ts, S5_CH)),
                  state, state,
                  const((S5_CH, S5_WIDTH)), const((S5_CH, S5_WIDTH)),
                  const((1, S5_WIDTH)), const((S5_WIDTH, S5_WIDTH))],
        out_specs=[pl.BlockSpec((ts, S5_WIDTH), lambda b, k: (b * nts + k, 0)), state, state],
        out_shape=[jax.ShapeDtypeStruct((bsz * t, S5_WIDTH), BF16),
                   jax.ShapeDtypeStruct((bsz, 1, S5_CH), F32),
                   jax.ShapeDtypeStruct((bsz, 1, S5_CH), F32)],
        scratch_shapes=[pltpu.VMEM((1, S5_CH), F32), pltpu.VMEM((1, S5_CH), F32)],
        compiler_params=_cp("parallel", "arbitrary"),
        name="s5_scan",
    )(y, prm["bre"], prm["bim"], prm["adr"], prm["adi"], prm["atr"], prm["ati"], h0r, h0i,
      prm["cre"], prm["cim"], prm["d"], prm["wglu"])


def _cmul(ar, ai, br, bi):
    return ar * br - ai * bi, ar * bi + ai * br


def s5_params(a_re, a_im, log_dt, b_re, b_im, c_re, c_im, d, w_glu, ts):
    step = jnp.exp(log_dt)[:, None]
    mag = jnp.exp(a_re * step)
    ab_re, ab_im = mag * jnp.cos(a_im * step), mag * jnp.sin(a_im * step)
    den = a_re * a_re + a_im * a_im
    z_re = ((ab_re - 1.0) * a_re + ab_im * a_im) / den
    z_im = (ab_im * a_re - (ab_re - 1.0) * a_im) / den
    bb_re = z_re[..., None] * b_re - z_im[..., None] * b_im
    bb_im = z_re[..., None] * b_im + z_im[..., None] * b_re
    eye = jnp.eye(S5_GROUPS, dtype=F32)
    bd_in = lambda w: jnp.einsum("gpn,gh->gnhp", w, eye).reshape(S5_WIDTH, S5_CH).astype(BF16)
    bd_out = lambda w: jnp.einsum("gnp,gh->gphn", w, eye).reshape(S5_CH, S5_WIDTH).astype(BF16)
    ar, ai = ab_re.reshape(1, S5_CH), ab_im.reshape(1, S5_CH)
    adr, adi = [ar], [ai]
    d2 = 2
    while d2 < ts:
        r, i = _cmul(adr[-1], adi[-1], adr[-1], adi[-1])
        adr.append(r)
        adi.append(i)
        d2 *= 2
    atr, ati = ar, ai
    n = 1
    lev = 0
    while n < ts:
        pr, pi = _cmul(atr, ati, adr[lev], adi[lev])
        atr, ati = jnp.concatenate([atr, pr], 0), jnp.concatenate([ati, pi], 0)
        n *= 2
        lev += 1
    return {"bre": bd_in(bb_re), "bim": bd_in(bb_im), "cre": bd_out(c_re), "cim": bd_out(c_im),
            "adr": jnp.concatenate(adr, 0), "adi": jnp.concatenate(adi, 0), "atr": atr, "ati": ati,
            "d": d.reshape(1, S5_WIDTH), "wglu": w_glu.astype(BF16)}


def prep_w_in(w):
    pad = jnp.zeros((w.shape[0], N_IN_PAD - N_IN), w.dtype)
    return jnp.concatenate([w[:, 3864:6936], w[:, 0:3840], w[:, 3840:3864], pad], axis=1).astype(BF16)


def _seg_rms(x, ones, w):
    ss = _dot2(x * x, ones)
    return x * lax.rsqrt(ss * (1.0 / HEAD_DIM) + EPS) * w


def _nsa_prep_kernel(nq_ref, kv_ref, wk_ref, wq_ref, wks_ref, wkw_ref, ones_ref, qn_ref, rows_ref, win_ref, cv_ref):
    qn_ref[...] = _seg_rms(nq_ref[...], ones_ref[...], wq_ref[...]).astype(BF16)
    ones1 = ones_ref[0:LANE, 0:LANE]
    kv = kv_ref[...]
    ksn = _seg_rms(kv[:, 2 * KV_W:3 * KV_W], ones1, wks_ref[...])
    rows_ref[...] = jnp.concatenate([kv[:, 0:2 * KV_W], ksn, kv[:, 3 * KV_W:4 * KV_W]], axis=1)
    cv_ref[...] = kv[:, 0:2 * KV_W].astype(BF16)
    wk = wk_ref[...]
    kwn = _seg_rms(wk[:, 0:KV_W], ones1, wkw_ref[...])
    win_ref[...] = jnp.concatenate([kwn, wk[:, KV_W:2 * KV_W]], axis=1)


def _block_ones(n, blk):
    r = np.arange(n) // blk
    return jnp.asarray((r[:, None] == r[None, :]).astype(np.float32), BF16)


def nsa_prep(y, qkn, tm):
    m = y.shape[0]
    wq = (jnp.tile(qkn[0], NSA_HEADS) * HEAD_DIM ** -0.5).reshape(1, NSA_WIDTH)
    wks = jnp.tile(qkn[2], NSA_KV).reshape(1, KV_W)
    wkw = jnp.tile(qkn[3], NSA_KV).reshape(1, KV_W)
    ones = _block_ones(NSA_WIDTH, HEAD_DIM)
    const = lambda shape: pl.BlockSpec(shape, lambda i: (0, 0))
    return pl.pallas_call(
        _nsa_prep_kernel,
        grid=(m // tm,),
        in_specs=[pl.BlockSpec((tm, 512), lambda i: (i, COL_NQ // 512)),
                  pl.BlockSpec((tm, 512), lambda i: (i, COL_KV4 // 512)),
                  pl.BlockSpec((tm, 256), lambda i: (i, COL_WKV // 256)),
                  const((1, NSA_WIDTH)), const((1, KV_W)), const((1, KV_W)), const((NSA_WIDTH, NSA_WIDTH))],
        out_specs=[pl.BlockSpec((tm, 512), lambda i: (i, 0)), pl.BlockSpec((tm, 512), lambda i: (i, 0)),
                   pl.BlockSpec((tm, 256), lambda i: (i, 0)), pl.BlockSpec((tm, 256), lambda i: (i, 0))],
        out_shape=[jax.ShapeDtypeStruct((m, NSA_WIDTH), BF16), jax.ShapeDtypeStruct((m, 4 * KV_W), F32),
                   jax.ShapeDtypeStruct((m, 2 * KV_W), F32), jax.ShapeDtypeStruct((m, 2 * KV_W), BF16)],
        compiler_params=_cp("parallel"),
        name="nsa_prep",
    )(y, y, y, wq, wks, wkw, ones)


def _cmp_post_kernel(c_ref, cb_ref, w2_ref, wk_ref, ones_ref, k_ref, v_ref):
    c = c_ref[0]
    nc = c.shape[0]
    ca = c[:, 0:512]
    cb = c[:, 512:1024]
    rid = lax.broadcasted_iota(jnp.int32, cb.shape, 0)
    cb_next = jnp.where(rid < nc - 1, pltpu.roll(cb, nc - 1, axis=0), 0.0)
    hid = _gelu_tanh(ca + cb_next + cb_ref[...])
    out = _dot(hid, w2_ref[...])
    k_ref[0] = _seg_rms(out[:, 0:KV_W], ones_ref[...], wk_ref[...]).astype(BF16)
    v_ref[0] = out[:, KV_W:2 * KV_W].astype(BF16)


def cmp_post(cacb, cbias, w2blk, qkn1):
    bsz, nc, _ = cacb.shape
    wk = jnp.tile(qkn1, NSA_KV).reshape(1, KV_W)
    ones = _block_ones(KV_W, HEAD_DIM)
    const = lambda shape: pl.BlockSpec(shape, lambda b: (0, 0))
    return pl.pallas_call(
        _cmp_post_kernel,
        grid=(bsz,),
        in_specs=[pl.BlockSpec((1, nc, 1024), lambda b: (b, 0, 0)),
                  const((1, 512)), const((512, 256)), const((1, KV_W)), const((KV_W, KV_W))],
        out_specs=[pl.BlockSpec((1, nc, KV_W), lambda b: (b, 0, 0)), pl.BlockSpec((1, nc, KV_W), lambda b: (b, 0, 0))],
        out_shape=[jax.ShapeDtypeStruct((bsz, nc, KV_W), BF16), jax.ShapeDtypeStruct((bsz, nc, KV_W), BF16)],
        compiler_params=_cp("parallel"),
        name="cmp_post",
    )(cacb, cbias, w2blk, wk, ones)


def cmp_params(pe, w1, w2):
    w1r = w1.reshape(2, 2, 16, HEAD_DIM, CMP_HIDDEN)
    eye2 = jnp.eye(2, dtype=F32)
    wab = jnp.einsum("khjdc,kl,gm->jkgdhlmc", w1r, eye2, eye2).reshape(16 * 256, 1024).astype(BF16)
    cb = jnp.einsum("kf,kfc->kc", pe.reshape(2, CMP_LEN * HEAD_DIM), w1)
    cbias = jnp.broadcast_to(cb[:, None, :], (2, NSA_KV, CMP_HIDDEN)).reshape(1, 512)
    w2blk = jnp.einsum("kcd,kl,gm->kgclmd", w2, eye2, eye2).reshape(512, 256).astype(BF16)
    return wab, cbias, w2blk


def _cmpsel_kernel(qn_ref, kc_ref, vc_ref, bias_ref, ov_ref, ocmp_ref, selm_ref, *, tq, p0, n_sel, nselp):
    i = pl.program_id(0)
    q = qn_ref[...]
    kc = kc_ref[0]
    vc = vc_ref[0]
    nc = kc.shape[0]
    qpos = p0 + i * tq + lax.broadcasted_iota(jnp.int32, (tq, 1), 0)
    kend = lax.broadcasted_iota(jnp.int32, (1, nc), 1) * CMP_STRIDE + (CMP_LEN - 1)
    valid = qpos >= kend
    row_valid = (qpos >= CMP_LEN - 1).astype(F32)
    imp = [None, None]
    outs = []
    for h in range(NSA_HEADS):
        g = h // NSA_REP
        s = _dot_nt(q[:, h * HEAD_DIM:(h + 1) * HEAD_DIM], kc[:, g * HEAD_DIM:(g + 1) * HEAD_DIM]) + bias_ref[h]
        s = jnp.where(valid, s, NEG)
        e = jnp.exp(s - jnp.max(s, axis=-1, keepdims=True))
        p = e / jnp.sum(e, axis=-1, keepdims=True) * row_valid
        outs.append(_dot(p, vc[:, g * HEAD_DIM:(g + 1) * HEAD_DIM]))
        imp[g] = p if imp[g] is None else imp[g] + p
    ocmp_ref[...] = jnp.concatenate(outs, axis=1)

    jid = lax.broadcasted_iota(jnp.int32, (1, nselp), 1)
    cur = jnp.right_shift(qpos, int(math.log2(SEL_BLOCK)))
    forced = (jid == 0) | (jid == cur) | (jid == cur - 1)
    future = jid * SEL_BLOCK > qpos
    topn = min(SEL_TOPN, n_sel)
    for g in range(NSA_KV):
        score = _dot2(imp[g], ov_ref[...])
        score = jnp.where(forced, FORCE, jnp.where(future, -FORCE, score))
        cnt = jnp.zeros((tq, nselp), F32)
        for c in range(n_sel):
            col = score[:, c:c + 1]
            tie = (jid > c).astype(F32)
            cnt = cnt + jnp.where(col > score, 1.0, jnp.where(col == score, tie, 0.0))
        sel = jnp.where(jid < n_sel, jnp.where(cnt < topn, 1.0, 0.0), 0.0)
        selm_ref[:, g * nselp:(g + 1) * nselp] = sel.astype(BF16)


def cmpsel(qn, kcmp, vcmp, biasc, overlap, bsz, t, tq, p0, n_sel):
    nt = t // tq
    nc = kcmp.shape[1]
    nselp = overlap.shape[1]
    assert CMP_STRIDE * (nc - 1) + CMP_LEN - 1 > p0 + t - 1
    return pl.pallas_call(
        functools.partial(_cmpsel_kernel, tq=tq, p0=p0, n_sel=n_sel, nselp=nselp),
        grid=(nt, bsz),
        in_specs=[pl.BlockSpec((tq, NSA_WIDTH), lambda i, b: (b * nt + i, 0)),
                  pl.BlockSpec((1, nc, KV_W), lambda i, b: (b, 0, 0)),
                  pl.BlockSpec((1, nc, KV_W), lambda i, b: (b, 0, 0)),
                  pl.BlockSpec((NSA_HEADS, tq, nc), lambda i, b: (0, i, 0)),
                  pl.BlockSpec((nc, nselp), lambda i, b: (0, 0))],
        out_specs=[pl.BlockSpec((tq, NSA_WIDTH), lambda i, b: (b * nt + i, 0)),
                   pl.BlockSpec((tq, NSA_KV * nselp), lambda i, b: (b * nt + i, 0))],
        out_shape=[jax.ShapeDtypeStruct((bsz * t, NSA_WIDTH), F32),
                   jax.ShapeDtypeStruct((bsz * t, NSA_KV * nselp), BF16)],
        compiler_params=_cp("arbitrary", "arbitrary"),
        name="cmp_attn_select",
    )(qn, kcmp, vcmp, biasc, overlap)


def _stack_heads(q, g):
    return jnp.concatenate([q[:, (NSA_REP * g + r) * HEAD_DIM:(NSA_REP * g + r + 1) * HEAD_DIM]
                            for r in range(NSA_REP)], axis=0)


def _unstack_heads(o_groups, tq):
    return jnp.concatenate([o[r * tq:(r + 1) * tq] for o in o_groups for r in range(NSA_REP)], axis=1)


def _online_update(carry, s, v):
    m, l, acc = carry
    m_new = jnp.maximum(m, jnp.max(s, axis=-1, keepdims=True))
    a = jnp.exp(m - m_new)
    p = jnp.exp(s - m_new)
    return m_new, a * l + jnp.sum(p, axis=-1, keepdims=True), a * acc + _dot(p, v)


def _gated_sum(ng, ge, ocmp, osel, owin):
    gx = _dot2(jax.nn.sigmoid(ng), ge)
    return gx[:, 0:512] * ocmp + gx[:, 512:1024] * osel + gx[:, 1024:1536] * owin


def _prompt_attn_kernel(qn_ref, ks_ref, vs_ref, selm_ref, e_ref, bnear_ref, bfar_ref, kw_ref, vw_ref, bw_ref,
                        ocmp_ref, ng_ref, ge_ref, oc_ref, mfull_ref, *, tq, nselp, nprev):
    i = pl.program_id(1)
    q = qn_ref[...]
    o_sel, o_win = [], []
    for g in range(NSA_KV):
        hs = slice(NSA_REP * g, NSA_REP * (g + 1))
        cs = slice(g * HEAD_DIM, (g + 1) * HEAD_DIM)
        qg = _stack_heads(q, g)
        mfull_ref[...] = jnp.dot(selm_ref[:, g * nselp:(g + 1) * nselp], e_ref[...], preferred_element_type=F32)

        def tile(j, carry, bias):
            r0 = pl.multiple_of(j * tq, tq)
            kj = ks_ref[pl.ds(r0, tq), cs]
            vj = vs_ref[pl.ds(r0, tq), cs]
            s = _dot_nt(qg, kj).reshape(NSA_REP, tq, tq) + bias
            msk = mfull_ref[:, pl.ds(r0, tq)] > 0.5
            s = jnp.where(msk[None], s, NEG).reshape(NSA_REP * tq, tq)
            return _online_update(carry, s, vj)

        init = (jnp.full((NSA_REP * tq, 1), -jnp.inf, F32), jnp.zeros((NSA_REP * tq, 1), F32),
                jnp.zeros((NSA_REP * tq, HEAD_DIM), F32))
        bfar = bfar_ref[hs][:, None, :]
        carry = lax.fori_loop(0, jnp.maximum(i - 1, 0), lambda j, c: tile(j, c, bfar), init)
        carry = lax.cond(i >= 1, lambda c: tile(i - 1, c, bnear_ref[hs, :, 0:tq]), lambda c: c, carry)
        _, l, acc = tile(i, carry, bnear_ref[hs, :, tq:2 * tq])
        o_sel.append(acc / l)

        nk = (nprev + 1) * tq
        r0 = pl.multiple_of(i * tq, tq)
        kw = kw_ref[0, pl.ds(r0, nk), cs]
        vw = vw_ref[0, pl.ds(r0, nk), cs]
        s = _dot_nt(qg, kw).reshape(NSA_REP, tq, nk) + bw_ref[hs]
        kval = lax.broadcasted_iota(jnp.int32, (1, 1, nk), 2) >= (nprev - i) * tq
        s = jnp.where(kval, s, NEG).reshape(NSA_REP * tq, nk)
        e = jnp.exp(s - jnp.max(s, axis=-1, keepdims=True))
        o_win.append(_dot(e, vw) / jnp.sum(e, axis=-1, keepdims=True))

    oc = _gated_sum(ng_ref[...], ge_ref[...], ocmp_ref[...], _unstack_heads(o_sel, tq), _unstack_heads(o_win, tq))
    oc_ref[...] = oc.astype(BF16)


def _rel_bucket(dist):
    n = jnp.maximum(dist, 0)
    exact = N_BUCKETS // 2
    nf = jnp.maximum(n, 1).astype(F32)
    large = exact + (jnp.log(nf / exact) / math.log(MAX_DIST / exact) * (N_BUCKETS - exact)).astype(jnp.int32)
    return jnp.where(n < exact, n, jnp.minimum(large, N_BUCKETS - 1))


def _bias_table(rel_bias, dist, valid=None):
    b = rel_bias[_rel_bucket(dist)]
    if valid is not None:
        b = jnp.where(valid[..., None], b, NEG)
    return jnp.moveaxis(b, -1, 0)


def _gate_expand():
    ge = np.zeros((LANE, 3 * NSA_WIDTH), np.float32)
    for br in range(3):
        for h in range(NSA_HEADS):
            ge[br * NSA_HEADS + h, br * NSA_WIDTH + h * HEAD_DIM: br * NSA_WIDTH + (h + 1) * HEAD_DIM] = 1.0
    return jnp.asarray(ge, BF16)


def _block_expand(nselp, length):
    e = (np.arange(nselp)[:, None] == (np.arange(length) // SEL_BLOCK)[None, :]).astype(np.float32)
    return jnp.asarray(e, BF16)


def prompt_attn(qn, rows, selm, ocmp, y, winrows, rel_bias, bsz, t, tq):
    nt = t // tq
    nselp = selm.shape[1] // NSA_KV
    nprev = WINDOW // tq
    nk = (nprev + 1) * tq
    ar = jnp.arange
    dn = tq + ar(tq)[:, None] - ar(2 * tq)[None, :]
    bnear = _bias_table(rel_bias, dn, dn >= 0)
    bfar = jnp.broadcast_to(rel_bias[N_BUCKETS - 1][:, None], (NSA_HEADS, tq))
    assert tq + 1 >= MAX_DIST
    dw = ar(tq)[:, None] + nprev * tq - ar(nk)[None, :]
    bw = _bias_table(rel_bias, dw, (dw >= 0) & (dw < WINDOW))
    win3 = winrows.reshape(bsz, t, 2 * KV_W)
    winp = jnp.pad(win3, ((0, 0), (nprev * tq, 0), (0, 0)))
    e = _block_expand(nselp, t)
    ge = _gate_expand()
    const = lambda shape: pl.BlockSpec(shape, lambda b, i: tuple(0 for _ in shape))
    tile = lambda w, c: pl.BlockSpec((tq, w), lambda b, i: (b * nt + i, c))
    return pl.pallas_call(
        functools.partial(_prompt_attn_kernel, tq=tq, nselp=nselp, nprev=nprev),
        grid=(bsz, nt),
        in_specs=[tile(NSA_WIDTH, 0),
                  pl.BlockSpec((t, KV_W), lambda b, i: (b, 2)),
                  pl.BlockSpec((t, KV_W), lambda b, i: (b, 3)),
                  tile(NSA_KV * nselp, 0),
                  const((nselp, t)), const((NSA_HEADS, tq, 2 * tq)), const((NSA_HEADS, tq)),
                  pl.BlockSpec((1, t + nprev * tq, KV_W), lambda b, i: (b, 0, 0)),
                  pl.BlockSpec((1, t + nprev * tq, KV_W), lambda b, i: (b, 0, 1)),
                  const((NSA_HEADS, tq, nk)),
                  tile(NSA_WIDTH, 0),
                  tile(LANE, COL_NGATE // LANE),
                  const((LANE, 3 * NSA_WIDTH))],
        out_specs=tile(NSA_WIDTH, 0),
        out_shape=jax.ShapeDtypeStruct((bsz * t, NSA_WIDTH), BF16),
        scratch_shapes=[pltpu.VMEM((tq, t), F32)],
        compiler_params=_cp("parallel", "arbitrary"),
        name="prompt_sel_win_attn",
    )(qn, rows, rows, selm, e, bnear, bfar, winp, winp, bw, ocmp, y, ge)


def _page_gather_kernel(pt_ref, *refs, pg):
    cv_ref, sel_ref = refs[pg], refs[pg + 1]
    for k in range(pg):
        x = refs[k][0]
        cv_ref[0, k * PAGE_SIZE:(k + 1) * PAGE_SIZE, :] = x[:, 0:2 * KV_W].astype(BF16)
        sel_ref[0, k * PAGE_SIZE:(k + 1) * PAGE_SIZE, :] = x[:, 2 * KV_W:4 * KV_W].astype(BF16)


def page_gather(cache3, pt_flat, bsz, npages, pg):
    def page_spec(k):
        return pl.BlockSpec((1, PAGE_SIZE, 4 * KV_W), lambda b, p, pt: (pt[b * npages + p * pg + k], 0, 0))

    out = pl.BlockSpec((1, pg * PAGE_SIZE, 2 * KV_W), lambda b, p, pt: (b, p, 0))
    shp = jax.ShapeDtypeStruct((bsz, npages * PAGE_SIZE, 2 * KV_W), BF16)
    return pl.pallas_call(
        functools.partial(_page_gather_kernel, pg=pg),
        grid_spec=pltpu.PrefetchScalarGridSpec(
            num_scalar_prefetch=1, grid=(bsz, npages // pg),
            in_specs=[page_spec(k) for k in range(pg)], out_specs=[out, out]),
        out_shape=[shp, shp],
        compiler_params=_cp("parallel", "arbitrary"),
        name="page_gather",
    )(pt_flat, *([cache3] * pg))


def _decode_attn_kernel(*refs, t, nselp, masked):
    if masked:
        qn_ref, kp_ref, vp_ref, bp_ref, kn_ref, vn_ref, bn_ref, selm_ref, e_ref, o_ref, m_ref, l_ref, acc_ref = refs
    else:
        qn_ref, kp_ref, vp_ref, bp_ref, kn_ref, vn_ref, bn_ref, o_ref, m_ref, l_ref, acc_ref = refs
    j = pl.program_id(1)
    rows = NSA_REP * t

    @pl.when(j == 0)
    def _():
        m_ref[...] = jnp.full(m_ref.shape, -jnp.inf, F32)
        l_ref[...] = jnp.zeros(l_ref.shape, F32)
        acc_ref[...] = jnp.zeros(acc_ref.shape, F32)

    q = qn_ref[...]
    kp = kp_ref[0]
    vp = vp_ref[0]
    tk = kp.shape[0]
    for g in range(NSA_KV):
        hs = slice(NSA_REP * g, NSA_REP * (g + 1))
        cs = slice(g * HEAD_DIM, (g + 1) * HEAD_DIM)
        qg = _stack_heads(q, g)
        s = _dot_nt(qg, kp[:, cs]).reshape(NSA_REP, t, tk) + bp_ref[hs]
        if masked:
            msk = jnp.dot(selm_ref[:, g * nselp:(g + 1) * nselp], e_ref[...], preferred_element_type=F32) > 0.5
            s = jnp.where(msk[None], s, NEG)
        carry = _online_update((m_ref[g], l_ref[g], acc_ref[g]), s.reshape(rows, tk), vp[:, cs])
        m_ref[g], l_ref[g], acc_ref[g] = carry

    @pl.when(j == pl.num_programs(1) - 1)
    def _():
        outs = []
        for g in range(NSA_KV):
            hs = slice(NSA_REP * g, NSA_REP * (g + 1))
            cs = slice(g * HEAD_DIM, (g + 1) * HEAD_DIM)
            qg = _stack_heads(q, g)
            s = _dot_nt(qg, kn_ref[:, cs]).reshape(NSA_REP, t, t) + bn_ref[hs]
            _, l, acc = _online_update((m_ref[g], l_ref[g], acc_ref[g]), s.reshape(rows, t), vn_ref[:, cs])
            outs.append(acc / l)
        o_ref[...] = _unstack_heads(outs, t)


def decode_attn(qn, kv_past, bias_past, new_rows, new_cols, bias_new, bsz, t, tk, selm=None, e=None):
    lp = kv_past.shape[1]
    nkt = lp // tk
    masked = selm is not None
    nselp = selm.shape[1] // NSA_KV if masked else 0
    in_specs = [pl.BlockSpec((t, NSA_WIDTH), lambda b, j: (b, 0)),
                pl.BlockSpec((1, tk, KV_W), lambda b, j: (b, j, 0)),
                pl.BlockSpec((1, tk, KV_W), lambda b, j: (b, j, 1)),
                pl.BlockSpec((NSA_HEADS, t, tk), lambda b, j: (0, 0, j)),
                pl.BlockSpec((t, KV_W), lambda b, j: (b, new_cols[0])),
                pl.BlockSpec((t, KV_W), lambda b, j: (b, new_cols[1])),
                pl.BlockSpec((NSA_HEADS, t, t), lambda b, j: (0, 0, 0))]
    args = [qn, kv_past, kv_past, bias_past, new_rows, new_rows, bias_new]
    if masked:
        in_specs += [pl.BlockSpec((t, NSA_KV * nselp), lambda b, j: (b, 0)),
                     pl.BlockSpec((nselp, tk), lambda b, j: (0, j))]
        args += [selm, e]
    rows = NSA_REP * t
    return pl.pallas_call(
        functools.partial(_decode_attn_kernel, t=t, nselp=nselp, masked=masked),
        grid=(bsz, nkt),
        in_specs=in_specs,
        out_specs=pl.BlockSpec((t, NSA_WIDTH), lambda b, j: (b, 0)),
        out_shape=jax.ShapeDtypeStruct((bsz * t, NSA_WIDTH), F32),
        scratch_shapes=[pltpu.VMEM((NSA_KV, rows, 1), F32), pltpu.VMEM((NSA_KV, rows, 1), F32),
                        pltpu.VMEM((NSA_KV, rows, HEAD_DIM), F32)],
        compiler_params=_cp("parallel", "arbitrary"),
        name="decode_sel_attn" if masked else "decode_win_attn",
    )(*args)


def _combine_kernel(ng_ref, ge_ref, ocmp_ref, osel_ref, owin_ref, oc_ref):
    oc_ref[...] = _gated_sum(ng_ref[...], ge_ref[...], ocmp_ref[...], osel_ref[...], owin_ref[...]).astype(BF16)


def combine(y, ocmp, osel, owin):
    m = ocmp.shape[0]
    full = pl.BlockSpec((m, NSA_WIDTH), lambda i: (0, 0))
    return pl.pallas_call(
        _combine_kernel,
        grid=(1,),
        in_specs=[pl.BlockSpec((m, LANE), lambda i: (0, COL_NGATE // LANE)),
                  pl.BlockSpec((LANE, 3 * NSA_WIDTH), lambda i: (0, 0)), full, full, full],
        out_specs=full,
        out_shape=jax.ShapeDtypeStruct((m, NSA_WIDTH), BF16),
        compiler_params=_cp("arbitrary"),
        name="nsa_combine",
    )(y, _gate_expand(), ocmp, osel, owin)


def _overlap(nc, n_sel, nselp):
    cs = np.arange(nc) * CMP_STRIDE
    ss = np.arange(nselp) * SEL_BLOCK
    ov = (cs[:, None] < ss[None, :] + SEL_BLOCK) & (cs[:, None] + CMP_LEN > ss[None, :])
    ov &= (np.arange(nc) < nc - 1)[:, None] & (np.arange(nselp) < n_sel)[None, :]
    return jnp.asarray(ov.astype(np.float32), BF16)


def _round_up(x, m):
    return -(-x // m) * m


def layer(x, past, lw, rel_bias, bsz, t):
    m = bsz * t
    prompt = past is None
    tm = min(512, m)
    y = rms_matmul(x, lw["norm_mix"], lw["w_in"], tm, 1408)

    c = math.gcd(t, HG_CHUNK)
    s0 = jnp.zeros((bsz, HG_HEADS, HG_DIM, HG_DIM), F32) if prompt else past["hgrn"]
    oa, s_hg = hgrn(y, lw["lb"], lw["hg_norm"], s0, bsz, t, c, min(t, 256))

    ts = min(t, 256)
    if prompt:
        h0r = h0i = jnp.zeros((bsz, 1, S5_CH), F32)
    else:
        h0r, h0i = past["s5r"].reshape(bsz, 1, S5_CH), past["s5i"].reshape(bsz, 1, S5_CH)
    ob, s5r, s5i = s5(y, lw["s5"], h0r, h0i, bsz, t, ts)

    qn, rows, winrows, cv = nsa_prep(y, lw["qkn"], tm)
    if prompt:
        p0, nc = 0, t // CMP_STRIDE
        chunks = cv.reshape(m // CMP_STRIDE, CMP_STRIDE * 2 * KV_W)
    else:
        npages = past["npages"]
        p0 = npages * PAGE_SIZE
        nc = p0 // CMP_STRIDE
        cv_past, sel_past = page_gather(past["cache3"], past["pt_flat"], bsz, npages, min(16, npages))
        chunks = cv_past.reshape(bsz * nc, CMP_STRIDE * 2 * KV_W)
    cacb = matmul(chunks, lw["cmp_wab"], min(512, chunks.shape[0]))
    kcmp, vcmp = cmp_post(cacb.reshape(bsz, nc, 1024), lw["cmp_bias"], lw["cmp_w2"], lw["qkn"][1])
    n_sel = -(-(p0 + t) // SEL_BLOCK)
    nselp = _round_up(n_sel, LANE)
    qpos = p0 + jnp.arange(t)
    kend = jnp.arange(nc) * CMP_STRIDE + CMP_LEN - 1
    biasc = _bias_table(rel_bias, qpos[:, None] - kend[None, :])
    tq = min(t, 128)
    ocmp, selm = cmpsel(qn, kcmp, vcmp, biasc, _overlap(nc, n_sel, nselp), bsz, t, tq, p0, n_sel)
    if prompt:
        oc = prompt_attn(qn, rows, selm, ocmp, y, winrows, rel_bias, bsz, t, tq)
        lw_ = min(WINDOW, t)
        new_win = winrows.reshape(bsz, t, 2 * KV_W)[:, t - lw_:]
    else:
        ar = jnp.arange
        dn = ar(t)[:, None] - ar(t)[None, :]
        bias_new = _bias_table(rel_bias, dn, dn >= 0)
        bias_sel = _bias_table(rel_bias, qpos[:, None] - ar(p0)[None, :])
        tk = min(2048, p0)
        osel = decode_attn(qn, sel_past, bias_sel, rows, (2, 3), bias_new, bsz, t, tk,
                           selm=selm, e=_block_expand(nselp, p0))
        win = past["win"]
        lwin = win.shape[1]
        dw = qpos[:, None] - (p0 - lwin + ar(lwin))[None, :]
        bias_win = _bias_table(rel_bias, dw, (dw >= 0) & (dw < WINDOW))
        owin = decode_attn(qn, win, bias_win, winrows, (0, 1), bias_new, bsz, t, lwin)
        oc = combine(y, ocmp, osel, owin)
        new_win = jnp.concatenate([win, winrows.reshape(bsz, t, 2 * KV_W)], axis=1)[:, t:]

    x1 = merge(x, y, oa, ob, oc, lw["w_branch"], lw["w_out"], tm)
    x2 = ffn(x1, lw["norm_ffn"], lw["w_gate_up"], lw["w_down"], tm, 1408)
    return x2, (rows, new_win, s_hg, s5r, s5i)


def layer_weights(l, ts_list, norm_mix, w_in, lower_bounds, hg_out_norm, s5_a_re, s5_a_im, s5_log_dt, s5_b_re, s5_b_im,
                  s5_c_re, s5_c_im, s5_d, s5_w_glu, nsa_qk_norm, cmp_pe, cmp_w1, cmp_w2, w_branch, w_out, norm_ffn,
                  w_gate_up, w_down):
    wab, cbias, w2blk = cmp_params(cmp_pe[l], cmp_w1[l], cmp_w2[l])
    s5p = {ts: s5_params(s5_a_re[l], s5_a_im[l], s5_log_dt[l], s5_b_re[l], s5_b_im[l], s5_c_re[l], s5_c_im[l],
                         s5_d[l], s5_w_glu[l], ts) for ts in ts_list}
    return {"norm_mix": norm_mix[l].reshape(1, D_MODEL), "w_in": prep_w_in(w_in[l]),
            "lb": lower_bounds[l].reshape(1, HG_WIDTH), "hg_norm": hg_out_norm[l].reshape(1, HG_DIM),
            "s5_by_ts": s5p, "qkn": nsa_qk_norm[l], "cmp_wab": wab, "cmp_bias": cbias, "cmp_w2": w2blk,
            "w_branch": w_branch[l].astype(BF16), "w_out": w_out[l].astype(BF16),
            "norm_ffn": norm_ffn[l].reshape(1, D_MODEL), "w_gate_up": w_gate_up[l].astype(BF16),
            "w_down": w_down[l].astype(BF16)}


def kernel(x_prompt, x_sample, cache_nsa_kv, cache_win_kv, state_hgrn, state_s5_re, state_s5_im, page_table,
           norm_mix, w_in, hg_lb_logits, hg_out_norm, s5_a_re, s5_a_im, s5_log_dt, s5_b_re, s5_b_im,
           s5_c_re, s5_c_im, s5_d, s5_w_glu, nsa_qk_norm, cmp_pe, cmp_w1, cmp_w2, rel_bias,
           w_branch, w_out, norm_ffn, w_gate_up, w_down):
    depth = w_in.shape[0]
    bp, tp, d = x_prompt.shape
    bs, tsm, _ = x_sample.shape
    n_phys = cache_nsa_kv.shape[1]
    npages = page_table.shape[1]
    lb_sm = jax.nn.softmax(hg_lb_logits.astype(F32), axis=0)
    lower_bounds = jnp.cumsum(lb_sm, axis=0) - lb_sm[0]
    ts_p, ts_s = min(tp, 256), min(tsm, 256)
    hp = x_prompt.reshape(bp * tp, d)
    hs = x_sample.reshape(bs * tsm, d)
    cache3 = cache_nsa_kv.reshape(depth * n_phys, PAGE_SIZE, 4 * KV_W)
    st_p, st_s = [], []
    for l in range(depth):
        lw = layer_weights(l, sorted({ts_p, ts_s}), norm_mix, w_in, lower_bounds, hg_out_norm, s5_a_re, s5_a_im,
                           s5_log_dt, s5_b_re, s5_b_im, s5_c_re, s5_c_im, s5_d, s5_w_glu, nsa_qk_norm, cmp_pe,
                           cmp_w1, cmp_w2, w_branch, w_out, norm_ffn, w_gate_up, w_down)
        past = {"cache3": cache3, "pt_flat": (page_table + l * n_phys).reshape(-1).astype(jnp.int32),
                "npages": npages, "win": cache_win_kv[l].reshape(bs, -1, 2 * KV_W), "hgrn": state_hgrn[l],
                "s5r": state_s5_re[l], "s5i": state_s5_im[l]}
        hp, sp = layer(hp, None, dict(lw, s5=lw["s5_by_ts"][ts_p]), rel_bias, bp, tp)
        hs, ss = layer(hs, past, dict(lw, s5=lw["s5_by_ts"][ts_s]), rel_bias, bs, tsm)
        st_p.append(sp)
        st_s.append(ss)

    def stack(states, k, shape):
        return jnp.stack([s[k].reshape(shape) for s in states])

    kvs = (4, NSA_KV, HEAD_DIM)
    return (hp.reshape(bp, tp, d), hs.reshape(bs, tsm, d),
            stack(st_p, 0, (bp, tp) + kvs), stack(st_s, 0, (bs, tsm) + kvs),
            stack(st_p, 1, (bp, -1, 2, NSA_KV, HEAD_DIM)), stack(st_s, 1, (bs, -1, 2, NSA_KV, HEAD_DIM)),
            stack(st_p, 2, (bp, HG_HEADS, HG_DIM, HG_DIM)), stack(st_s, 2, (bs, HG_HEADS, HG_DIM, HG_DIM)),
            stack(st_p, 3, (bp, S5_GROUPS, S5_STATE)), stack(st_p, 4, (bp, S5_GROUPS, S5_STATE)),
            stack(st_s, 3, (bs, S5_GROUPS, S5_STATE)), stack(st_s, 4, (bs, S5_GROUPS, S5_STATE)))
```

```python
import functools
import math

import numpy as np
import jax
import jax.numpy as jnp
from jax import lax
from jax.experimental import pallas as pl
from jax.experimental.pallas import tpu as pltpu

F32 = jnp.float32
BF16 = jnp.bfloat16

D_MODEL = 1024
HG_HEADS = 4
HG_DIM = 128
HG_WIDTH = HG_HEADS * HG_DIM
HG_CHUNK = 64
LB_FLOOR = 1e-30
S5_GROUP = 16
S5_GROUPS = 32
S5_WIDTH = S5_GROUP * S5_GROUPS
S5_STATE = 64
S5_CH = S5_GROUPS * S5_STATE
NSA_HEADS = 8
NSA_KV = 2
NSA_REP = NSA_HEADS // NSA_KV
HEAD_DIM = 64
NSA_WIDTH = NSA_HEADS * HEAD_DIM
KV_W = NSA_KV * HEAD_DIM
CMP_LEN = 32
CMP_STRIDE = 16
CMP_HIDDEN = 128
SEL_BLOCK = 64
SEL_TOPN = 16
WINDOW = 512
N_BUCKETS = 32
MAX_DIST = 128
PAGE_SIZE = 128
D_FF = 2816
NEG = -1e30
FORCE = 1e9
EPS = 1e-6

COL_GA, COL_GB, COL_GC = 0, 1024, 2048
COL_HQ, COL_HF, COL_HI, COL_HG = 3072, 3584, 4096, 4608
COL_SU, COL_NQ, COL_KV4, COL_WKV, COL_NGATE = 5120, 5632, 6144, 6656, 6912
N_IN_PAD = 7040
N_IN = 6936

LANE = 128
VMEM_LIMIT = 56 * 1024 * 1024


def _cp(*sem):
    return pltpu.CompilerParams(dimension_semantics=sem, vmem_limit_bytes=VMEM_LIMIT)


def _dot(a, b):
    return jnp.dot(a.astype(BF16), b.astype(BF16), preferred_element_type=F32)


def _dot_nt(a, b):
    return lax.dot_general(a.astype(BF16), b.astype(BF16), (((1,), (1,)), ((), ())), preferred_element_type=F32)


def _dot_tn(a, b):
    return lax.dot_general(a.astype(BF16), b.astype(BF16), (((0,), (0,)), ((), ())), preferred_element_type=F32)


def _split2(x):
    hi = x.astype(BF16)
    lo = (x - hi.astype(F32)).astype(BF16)
    return hi, lo


def _dot2(x, w):
    hi, lo = _split2(x)
    return jnp.dot(hi, w, preferred_element_type=F32) + jnp.dot(lo, w, preferred_element_type=F32)


def _silu(x):
    return x * jax.nn.sigmoid(x)


def _gelu_tanh(x):
    return 0.5 * x * (1.0 + jnp.tanh(math.sqrt(2.0 / math.pi) * (x + 0.044715 * (x * x * x))))


def _rms_mm_kernel(x_ref, g_ref, w_ref, o_ref, xn_ref):
    @pl.when(pl.program_id(1) == 0)
    def _():
        x = x_ref[...]
        ms = jnp.mean(x * x, axis=-1, keepdims=True)
        xn_ref[...] = (x * lax.rsqrt(ms + EPS) * g_ref[...]).astype(BF16)

    o_ref[...] = jnp.dot(xn_ref[...], w_ref[...], preferred_element_type=F32)


def rms_matmul(x, g, w, tm, tn):
    m, k = x.shape
    n = w.shape[1]
    return pl.pallas_call(
        _rms_mm_kernel,
        grid=(m // tm, n // tn),
        in_specs=[pl.BlockSpec((tm, k), lambda i, j: (i, 0)),
                  pl.BlockSpec((1, k), lambda i, j: (0, 0)),
                  pl.BlockSpec((k, tn), lambda i, j: (0, j))],
        out_specs=pl.BlockSpec((tm, tn), lambda i, j: (i, j)),
        out_shape=jax.ShapeDtypeStruct((m, n), F32),
        scratch_shapes=[pltpu.VMEM((tm, k), BF16)],
        compiler_params=_cp("parallel", "arbitrary"),
        name="rms_in_proj",
    )(x, g, w)


def _mm_kernel(x_ref, w_ref, o_ref):
    o_ref[...] = jnp.dot(x_ref[...], w_ref[...], preferred_element_type=F32)


def matmul(x, w, tm):
    m, k = x.shape
    n = w.shape[1]
    return pl.pallas_call(
        _mm_kernel,
        grid=(m // tm,),
        in_specs=[pl.BlockSpec((tm, k), lambda i: (i, 0)),
                  pl.BlockSpec((k, n), lambda i: (0, 0))],
        out_specs=pl.BlockSpec((tm, n), lambda i: (i, 0)),
        out_shape=jax.ShapeDtypeStruct((m, n), F32),
        compiler_params=_cp("parallel"),
        name="cmp_matmul",
    )(x, w)


def _merge_kernel(x_ref, ga_ref, gb_ref, gc_ref, oa_ref, ob_ref, oc_ref, wb_ref, wo_ref, o_ref):
    m = jax.nn.sigmoid(ga_ref[...]) * jnp.dot(oa_ref[...], wb_ref[0], preferred_element_type=F32)
    m = m + jax.nn.sigmoid(gb_ref[...]) * jnp.dot(ob_ref[...], wb_ref[1], preferred_element_type=F32)
    m = m + jax.nn.sigmoid(gc_ref[...]) * jnp.dot(oc_ref[...], wb_ref[2], preferred_element_type=F32)
    o_ref[...] = x_ref[...] + jnp.dot(m.astype(BF16), wo_ref[...], preferred_element_type=F32)


def merge(x, y, oa, ob, oc, wb, wo, tm):
    m, d = x.shape
    return pl.pallas_call(
        _merge_kernel,
        grid=(m // tm,),
        in_specs=[pl.BlockSpec((tm, d), lambda i: (i, 0)),
                  pl.BlockSpec((tm, d), lambda i: (i, COL_GA // D_MODEL)),
                  pl.BlockSpec((tm, d), lambda i: (i, COL_GB // D_MODEL)),
                  pl.BlockSpec((tm, d), lambda i: (i, COL_GC // D_MODEL)),
                  pl.BlockSpec((tm, 512), lambda i: (i, 0)),
                  pl.BlockSpec((tm, 512), lambda i: (i, 0)),
                  pl.BlockSpec((tm, 512), lambda i: (i, 0)),
                  pl.BlockSpec((3, 512, d), lambda i: (0, 0, 0)),
                  pl.BlockSpec((d, d), lambda i: (0, 0))],
        out_specs=pl.BlockSpec((tm, d), lambda i: (i, 0)),
        out_shape=jax.ShapeDtypeStruct((m, d), F32),
        compiler_params=_cp("parallel"),
        name="merge_out_proj",
    )(x, y, y, y, oa, ob, oc, wb, wo)


def _ffn_kernel(x_ref, g_ref, wg_ref, wu_ref, wd_ref, o_ref, xn_ref):
    @pl.when(pl.program_id(1) == 0)
    def _():
        x = x_ref[...]
        ms = jnp.mean(x * x, axis=-1, keepdims=True)
        xn_ref[...] = (x * lax.rsqrt(ms + EPS) * g_ref[...]).astype(BF16)
        o_ref[...] = x

    xn = xn_ref[...]
    gate = jnp.dot(xn, wg_ref[...], preferred_element_type=F32)
    up = jnp.dot(xn, wu_ref[...], preferred_element_type=F32)
    h = (_silu(gate) * up).astype(BF16)
    o_ref[...] += jnp.dot(h, wd_ref[...], preferred_element_type=F32)


def ffn(x, g, wgu, wd, tm, tf):
    m, d = x.shape
    nf = D_FF // tf
    return pl.pallas_call(
        _ffn_kernel,
        grid=(m // tm, nf),
        in_specs=[pl.BlockSpec((tm, d), lambda i, f: (i, 0)),
                  pl.BlockSpec((1, d), lambda i, f: (0, 0)),
                  pl.BlockSpec((d, tf), lambda i, f: (0, f)),
                  pl.BlockSpec((d, tf), lambda i, f: (0, nf + f)),
                  pl.BlockSpec((tf, d), lambda i, f: (f, 0))],
        out_specs=pl.BlockSpec((tm, d), lambda i, f: (i, 0)),
        out_shape=jax.ShapeDtypeStruct((m, d), F32),
        scratch_shapes=[pltpu.VMEM((tm, d), BF16)],
        compiler_params=_cp("parallel", "arbitrary"),
        name="swiglu_ffn",
    )(x, g, wgu, wgu, wd)


def _hgrn_levels(c):
    return [m for m in (32, 16, 8, 4, 2, 1) if m < c]


def _hgrn_consts(c):
    mats = [np.tril(np.ones((c, c), np.float32))]
    r = np.arange(c)
    for m in _hgrn_levels(c):
        pos = r % (2 * m)
        mid = (r // (2 * m)) * 2 * m + m
        up = np.zeros((c, c), np.float32)
        lo = np.zeros((c, c), np.float32)
        for t in range(c):
            if pos[t] >= m:
                up[t, mid[t]:t + 1] = 1.0
            else:
                lo[t, t + 1:mid[t]] = 1.0
        mats += [up, lo]
    return np.concatenate(mats, axis=0)


def _hgrn_kernel(hq_ref, hf_ref, hi_ref, hg_ref, lb_ref, gn_ref, s0_ref, mst_ref, o_ref, sout_ref, st_ref,
                 *, c, nchunk):
    tb = pl.program_id(2)

    @pl.when(tb == 0)
    def _():
        st_ref[...] = s0_ref[0, 0].T

    levels = _hgrn_levels(c)
    lb = lb_ref[...]
    log_lb = jnp.log(jnp.maximum(lb, LB_FLOOR))
    log_1m = jnp.log1p(-lb)
    one_m = 1.0 - lb
    gn = gn_ref[...]
    mst = mst_ref[...]
    ti = lax.broadcasted_iota(jnp.int32, (c, c), 0)
    si = lax.broadcasted_iota(jnp.int32, (c, c), 1)
    txs = jnp.bitwise_xor(ti, si)
    lower = ti > si

    for ch in range(nchunk):
        rows = pl.ds(ch * c, c)
        fp = hf_ref[rows, :]
        hq = hq_ref[rows, :]
        v = hi_ref[rows, :]
        hg = hg_ref[rows, :]
        log_sig = -(jnp.maximum(-fp, 0.0) + jnp.log1p(jnp.exp(-jnp.abs(fp))))
        b = log_1m + log_sig
        g = jnp.maximum(log_lb, b) + jnp.log1p(jnp.exp(-jnp.abs(log_lb - b)))
        kin = one_m * jax.nn.sigmoid(-fp)
        q = _silu(hq)

        g1 = g.astype(BF16)
        r1 = g - g1.astype(F32)
        g2 = r1.astype(BF16)
        g3 = (r1 - g2.astype(F32)).astype(BF16)
        gs = jnp.concatenate([g1, g2, g3], axis=1)
        rr = jnp.dot(mst, gs, preferred_element_type=F32)
        rr = rr[:, 0:LANE] + rr[:, LANE:2 * LANE] + rr[:, 2 * LANE:3 * LANE]
        gcum = rr[0:c]

        vb = v.astype(BF16)
        att = jnp.where(ti == si, _dot_nt(q, kin), 0.0)
        for li, m in enumerate(levels):
            dq = rr[(1 + 2 * li) * c:(2 + 2 * li) * c]
            ek = rr[(2 + 2 * li) * c:(3 + 2 * li) * c]
            pair = _dot_nt(q * jnp.exp(dq), kin * jnp.exp(ek))
            sel = lower & (jnp.right_shift(txs, int(math.log2(m))) == 1)
            att = jnp.where(sel, pair, att)
        st = st_ref[...]
        o = _dot(att, vb) + _dot_nt(q * jnp.exp(gcum), st)
        g_end = gcum[c - 1:c, :]
        kd = kin * jnp.exp(g_end - gcum)
        st_ref[...] = st * jnp.exp(g_end) + _dot_tn(vb, kd)

        ms = jnp.mean(o * o, axis=-1, keepdims=True)
        on = o * lax.rsqrt(ms + EPS) * gn
        o_ref[rows, :] = (on * _silu(hg)).astype(BF16)

    @pl.when(tb == pl.num_programs(2) - 1)
    def _():
        sout_ref[0, 0] = st_ref[...].T


def hgrn(y, lb, gn, s0, bsz, t, c, tbk):
    ntb = t // tbk
    mst = jnp.asarray(_hgrn_consts(c), BF16)

    def col(base):
        return pl.BlockSpec((tbk, HG_DIM), lambda b, h, k: (b * ntb + k, base // HG_DIM + h))

    return pl.pallas_call(
        functools.partial(_hgrn_kernel, c=c, nchunk=tbk // c),
        grid=(bsz, HG_HEADS, ntb),
        in_specs=[col(COL_HQ), col(COL_HF), col(COL_HI), col(COL_HG),
                  pl.BlockSpec((1, HG_DIM), lambda b, h, k: (0, h)),
                  pl.BlockSpec((1, HG_DIM), lambda b, h, k: (0, 0)),
                  pl.BlockSpec((1, 1, HG_DIM, HG_DIM), lambda b, h, k: (b, h, 0, 0)),
                  pl.BlockSpec(mst.shape, lambda b, h, k: (0, 0))],
        out_specs=[pl.BlockSpec((tbk, HG_DIM), lambda b, h, k: (b * ntb + k, h)),
                   pl.BlockSpec((1, 1, HG_DIM, HG_DIM), lambda b, h, k: (b, h, 0, 0))],
        out_shape=[jax.ShapeDtypeStruct((bsz * t, HG_WIDTH), BF16),
                   jax.ShapeDtypeStruct((bsz, HG_HEADS, HG_DIM, HG_DIM), F32)],
        scratch_shapes=[pltpu.VMEM((HG_DIM, HG_DIM), F32)],
        compiler_params=_cp("parallel", "parallel", "arbitrary"),
        name="hgrn2_scan",
    )(y, y, y, y, lb, gn, s0, mst)


def _shift_rows(x, d):
    n = x.shape[0]
    rolled = pltpu.roll(x, d, axis=0)
    rid = lax.broadcasted_iota(jnp.int32, x.shape, 0)
    return jnp.where(rid >= d, rolled, 0.0)


def _s5_kernel(u_ref, bre_ref, bim_ref, adr_ref, adi_ref, atr_ref, ati_ref, h0r_ref, h0i_ref,
               cre_ref, cim_ref, d_ref, wg_ref, o_ref, hr_out, hi_out, cr_ref, ci_ref, *, ts):
    k = pl.program_id(1)

    @pl.when(k == 0)
    def _():
        cr_ref[...] = h0r_ref[0]
        ci_ref[...] = h0i_ref[0]

    u = u_ref[...]
    ub = u.astype(BF16)
    xr = jnp.dot(ub, bre_ref[...], preferred_element_type=F32)
    xi = jnp.dot(ub, bim_ref[...], preferred_element_type=F32)
    lev = 0
    d = 1
    while d < ts:
        ar = adr_ref[lev:lev + 1, :]
        ai = adi_ref[lev:lev + 1, :]
        sr = _shift_rows(xr, d)
        si = _shift_rows(xi, d)
        xr, xi = xr + ar * sr - ai * si, xi + ar * si + ai * sr
        d *= 2
        lev += 1
    cr = cr_ref[...]
    ci = ci_ref[...]
    atr = atr_ref[...]
    ati = ati_ref[...]
    hr = xr + atr * cr - ati * ci
    hi = xi + atr * ci + ati * cr
    cr_ref[...] = hr[ts - 1:ts, :]
    ci_ref[...] = hi[ts - 1:ts, :]
    y = jnp.dot(hr.astype(BF16), cre_ref[...], preferred_element_type=F32) \
        - jnp.dot(hi.astype(BF16), cim_ref[...], preferred_element_type=F32)
    y = _gelu_tanh(y + d_ref[...] * u)
    o_ref[...] = (y * jax.nn.sigmoid(jnp.dot(y.astype(BF16), wg_ref[...], preferred_element_type=F32))).astype(BF16)

    @pl.when(k == pl.num_programs(1) - 1)
    def _():
        hr_out[0] = hr[ts - 1:ts, :]
        hi_out[0] = hi[ts - 1:ts, :]


def s5(y, prm, h0r, h0i, bsz, t, ts):
    nts = t // ts
    nlev = prm["adr"].shape[0]
    const = lambda shape: pl.BlockSpec(shape, lambda b, k: tuple(0 for _ in shape))
    state = pl.BlockSpec((1, 1, S5_CH), lambda b, k: (b, 0, 0))
    return pl.pallas_call(
        functools.partial(_s5_kernel, ts=ts),
        grid=(bsz, nts),
        in_specs=[pl.BlockSpec((ts, S5_WIDTH), lambda b, k: (b * nts + k, COL_SU // S5_WIDTH)),
                  const((S5_WIDTH, S5_CH)), const((S5_WIDTH, S5_CH)),
                  const((nlev, S5_CH)), const((nlev, S5_CH)),
                  const((ts, S5_CH)), const((ts, S5_CH)),
                  state, state,
                  const((S5_CH, S5_WIDTH)), const((S5_CH, S5_WIDTH)),
                  const((1, S5_WIDTH)), const((S5_WIDTH, S5_WIDTH))],
        out_specs=[pl.BlockSpec((ts, S5_WIDTH), lambda b, k: (b * nts + k, 0)), state, state],
        out_shape=[jax.ShapeDtypeStruct((bsz * t, S5_WIDTH), BF16),
                   jax.ShapeDtypeStruct((bsz, 1, S5_CH), F32),
                   jax.ShapeDtypeStruct((bsz, 1, S5_CH), F32)],
        scratch_shapes=[pltpu.VMEM((1, S5_CH), F32), pltpu.VMEM((1, S5_CH), F32)],
        compiler_params=_cp("parallel", "arbitrary"),
        name="s5_scan",
    )(y, prm["bre"], prm["bim"], prm["adr"], prm["adi"], prm["atr"], prm["ati"], h0r, h0i,
      prm["cre"], prm["cim"], prm["d"], prm["wglu"])


def _cmul(ar, ai, br, bi):
    return ar * br - ai * bi, ar * bi + ai * br


def s5_params(a_re, a_im, log_dt, b_re, b_im, c_re, c_im, d, w_glu, ts):
    step = jnp.exp(log_dt)[:, None]
    mag = jnp.exp(a_re * step)
    ab_re, ab_im = mag * jnp.cos(a_im * step), mag * jnp.sin(a_im * step)
    den = a_re * a_re + a_im * a_im
    z_re = ((ab_re - 1.0) * a_re + ab_im * a_im) / den
    z_im = (ab_im * a_re - (ab_re - 1.0) * a_im) / den
    bb_re = z_re[..., None] * b_re - z_im[..., None] * b_im
    bb_im = z_re[..., None] * b_im + z_im[..., None] * b_re
    eye = jnp.eye(S5_GROUPS, dtype=F32)
    bd_in = lambda w: jnp.einsum("gpn,gh->gnhp", w, eye).reshape(S5_WIDTH, S5_CH).astype(BF16)
    bd_out = lambda w: jnp.einsum("gnp,gh->gphn", w, eye).reshape(S5_CH, S5_WIDTH).astype(BF16)
    ar, ai = ab_re.reshape(1, S5_CH), ab_im.reshape(1, S5_CH)
    adr, adi = [ar], [ai]
    d2 = 2
    while d2 < ts:
        r, i = _cmul(adr[-1], adi[-1], adr[-1], adi[-1])
        adr.append(r)
        adi.append(i)
        d2 *= 2
    atr, ati = ar, ai
    n = 1
    lev = 0
    while n < ts:
        pr, pi = _cmul(atr, ati, adr[lev], adi[lev])
        atr, ati = jnp.concatenate([atr, pr], 0), jnp.concatenate([ati, pi], 0)
        n *= 2
        lev += 1
    return {"bre": bd_in(bb_re), "bim": bd_in(bb_im), "cre": bd_out(c_re), "cim": bd_out(c_im),
            "adr": jnp.concatenate(adr, 0), "adi": jnp.concatenate(adi, 0), "atr": atr, "ati": ati,
            "d": d.reshape(1, S5_WIDTH), "wglu": w_glu.astype(BF16)}


def prep_w_in(w):
    pad = jnp.zeros((w.shape[0], N_IN_PAD - N_IN), w.dtype)
    return jnp.concatenate([w[:, 3864:6936], w[:, 0:3840], w[:, 3840:3864], pad], axis=1).astype(BF16)


def _seg_rms(x, ones, w):
    ss = _dot2(x * x, ones)
    return x * lax.rsqrt(ss * (1.0 / HEAD_DIM) + EPS) * w


def _nsa_prep_kernel(nq_ref, kv_ref, wk_ref, wq_ref, wks_ref, wkw_ref, ones_ref, qn_ref, rows_ref, win_ref, cv_ref):
    qn_ref[...] = _seg_rms(nq_ref[...], ones_ref[...], wq_ref[...]).astype(BF16)
    ones1 = ones_ref[0:LANE, 0:LANE]
    kv = kv_ref[...]
    ksn = _seg_rms(kv[:, 2 * KV_W:3 * KV_W], ones1, wks_ref[...])
    rows_ref[...] = jnp.concatenate([kv[:, 0:2 * KV_W], ksn, kv[:, 3 * KV_W:4 * KV_W]], axis=1)
    cv_ref[...] = kv[:, 0:2 * KV_W].astype(BF16)
    wk = wk_ref[...]
    kwn = _seg_rms(wk[:, 0:KV_W], ones1, wkw_ref[...])
    win_ref[...] = jnp.concatenate([kwn, wk[:, KV_W:2 * KV_W]], axis=1)


def _block_ones(n, blk):
    r = np.arange(n) // blk
    return jnp.asarray((r[:, None] == r[None, :]).astype(np.float32), BF16)


def nsa_prep(y, qkn, tm):
    m = y.shape[0]
    wq = (jnp.tile(qkn[0], NSA_HEADS) * HEAD_DIM ** -0.5).reshape(1, NSA_WIDTH)
    wks = jnp.tile(qkn[2], NSA_KV).reshape(1, KV_W)
    wkw = jnp.tile(qkn[3], NSA_KV).reshape(1, KV_W)
    ones = _block_ones(NSA_WIDTH, HEAD_DIM)
    const = lambda shape: pl.BlockSpec(shape, lambda i: (0, 0))
    row = lambda w: pl.BlockSpec((tm, w), lambda i: (i, 0))
    return pl.pallas_call(
        _nsa_prep_kernel,
        grid=(m // tm,),
        in_specs=[pl.BlockSpec((tm, 512), lambda i: (i, COL_NQ // 512)),
                  pl.BlockSpec((tm, 512), lambda i: (i, COL_KV4 // 512)),
                  pl.BlockSpec((tm, 256), lambda i: (i, COL_WKV // 256)),
                  const((1, NSA_WIDTH)), const((1, KV_W)), const((1, KV_W)), const((NSA_WIDTH, NSA_WIDTH))],
        out_specs=[row(512), row(512), row(256), row(256)],
        out_shape=[jax.ShapeDtypeStruct((m, NSA_WIDTH), BF16), jax.ShapeDtypeStruct((m, 4 * KV_W), F32),
                   jax.ShapeDtypeStruct((m, 2 * KV_W), F32), jax.ShapeDtypeStruct((m, 2 * KV_W), BF16)],
        compiler_params=_cp("parallel"),
        name="nsa_prep",
    )(y, y, y, wq, wks, wkw, ones)


def _cmp_post_kernel(c_ref, cb_ref, w2_ref, wk_ref, ones_ref, k_ref, v_ref):
    c = c_ref[0]
    nc = c.shape[0]
    ca = c[:, 0:512]
    cb = c[:, 512:1024]
    rid = lax.broadcasted_iota(jnp.int32, cb.shape, 0)
    cb_next = jnp.where(rid < nc - 1, pltpu.roll(cb, nc - 1, axis=0), 0.0)
    hid = _gelu_tanh(ca + cb_next + cb_ref[...])
    out = _dot(hid, w2_ref[...])
    k_ref[0] = _seg_rms(out[:, 0:KV_W], ones_ref[...], wk_ref[...]).astype(BF16)
    v_ref[0] = out[:, KV_W:2 * KV_W].astype(BF16)


def cmp_post(cacb, cbias, w2blk, qkn1):
    bsz, nc, _ = cacb.shape
    wk = jnp.tile(qkn1, NSA_KV).reshape(1, KV_W)
    ones = _block_ones(KV_W, HEAD_DIM)
    const = lambda shape: pl.BlockSpec(shape, lambda b: (0, 0))
    return pl.pallas_call(
        _cmp_post_kernel,
        grid=(bsz,),
        in_specs=[pl.BlockSpec((1, nc, 1024), lambda b: (b, 0, 0)),
                  const((1, 512)), const((512, 256)), const((1, KV_W)), const((KV_W, KV_W))],
        out_specs=[pl.BlockSpec((1, nc, KV_W), lambda b: (b, 0, 0)), pl.BlockSpec((1, nc, KV_W), lambda b: (b, 0, 0))],
        out_shape=[jax.ShapeDtypeStruct((bsz, nc, KV_W), BF16), jax.ShapeDtypeStruct((bsz, nc, KV_W), BF16)],
        compiler_params=_cp("parallel"),
        name="cmp_post",
    )(cacb, cbias, w2blk, wk, ones)


def cmp_params(pe, w1, w2):
    w1r = w1.reshape(2, 2, 16, HEAD_DIM, CMP_HIDDEN)
    eye2 = jnp.eye(2, dtype=F32)
    wab = jnp.einsum("khjdc,kl,gm->jkgdhlmc", w1r, eye2, eye2).reshape(16 * 256, 1024).astype(BF16)
    cb = jnp.einsum("kf,kfc->kc", pe.reshape(2, CMP_LEN * HEAD_DIM), w1)
    cbias = jnp.broadcast_to(cb[:, None, :], (2, NSA_KV, CMP_HIDDEN)).reshape(1, 512)
    w2blk = jnp.einsum("kcd,kl,gm->kgclmd", w2, eye2, eye2).reshape(512, 256).astype(BF16)
    return wab, cbias, w2blk


def _cmpsel_kernel(qn_ref, kc_ref, vc_ref, bias_ref, ov_ref, ocmp_ref, selm_ref, *, tq, p0, n_sel, nselp, nsr):
    i = pl.program_id(0)
    q = qn_ref[...]
    kc = kc_ref[0]
    vc = vc_ref[0]
    nc = kc.shape[0]
    qpos = p0 + i * tq + lax.broadcasted_iota(jnp.int32, (tq, 1), 0)
    kend = lax.broadcasted_iota(jnp.int32, (1, nc), 1) * CMP_STRIDE + (CMP_LEN - 1)
    valid = qpos >= kend
    row_valid = (qpos >= CMP_LEN - 1).astype(F32)
    imp = [None, None]
    outs = []
    for h in range(NSA_HEADS):
        g = h // NSA_REP
        s = _dot_nt(q[:, h * HEAD_DIM:(h + 1) * HEAD_DIM], kc[:, g * HEAD_DIM:(g + 1) * HEAD_DIM]) + bias_ref[h]
        s = jnp.where(valid, s, NEG)
        e = jnp.exp(s - jnp.max(s, axis=-1, keepdims=True))
        p = e / jnp.sum(e, axis=-1, keepdims=True) * row_valid
        outs.append(_dot(p, vc[:, g * HEAD_DIM:(g + 1) * HEAD_DIM]))
        imp[g] = p if imp[g] is None else imp[g] + p
    ocmp_ref[...] = jnp.concatenate(outs, axis=1)

    jid = lax.broadcasted_iota(jnp.int32, (1, nselp), 1)
    cur = jnp.right_shift(qpos, int(math.log2(SEL_BLOCK)))
    forced = (jid == 0) | (jid == cur) | (jid == cur - 1)
    future = jid * SEL_BLOCK > qpos
    topn = min(SEL_TOPN, n_sel)
    for g in range(NSA_KV):
        score = _dot2(imp[g], ov_ref[...])
        score = jnp.where(forced, FORCE, jnp.where(future, -FORCE, score))
        if nsr:
            st = score.T[0:nsr]
            rid = lax.broadcasted_iota(jnp.int32, (nsr, tq), 0)
            cnt = jnp.zeros((nsr, tq), F32)
            for c in range(n_sel):
                tie = jnp.where(rid > c, 1.0, 0.0)
                row = st[c:c + 1, :]
                cnt = cnt + jnp.where(row > st, 1.0, jnp.where(row == st, tie, 0.0))
            sel = jnp.where(rid < n_sel, jnp.where(cnt < topn, 1.0, 0.0), 0.0)
            sel = jnp.concatenate([sel, jnp.zeros((nselp - nsr, tq), F32)], axis=0).T
            selm_ref[:, g * nselp:(g + 1) * nselp] = sel.astype(BF16)
        else:
            cnt = jnp.zeros((tq, nselp), F32)
            for c in range(n_sel):
                col = score[:, c:c + 1]
                tie = (jid > c).astype(F32)
                cnt = cnt + jnp.where(col > score, 1.0, jnp.where(col == score, tie, 0.0))
            sel = jnp.where(jid < n_sel, jnp.where(cnt < topn, 1.0, 0.0), 0.0)
            selm_ref[:, g * nselp:(g + 1) * nselp] = sel.astype(BF16)


def cmpsel(qn, kcmp, vcmp, biasc, overlap, bsz, t, tq, p0, n_sel, nsr=0):
    nt = t // tq
    nc = kcmp.shape[1]
    nselp = overlap.shape[1]
    assert CMP_STRIDE * (nc - 1) + CMP_LEN - 1 > p0 + t - 1
    assert not nsr or (tq == LANE and nselp == LANE)
    sel_spec = pl.BlockSpec((tq, NSA_KV * nselp), lambda i, b: (b * nt + i, 0))
    sel_shape = jax.ShapeDtypeStruct((bsz * t, NSA_KV * nselp), BF16)
    return pl.pallas_call(
        functools.partial(_cmpsel_kernel, tq=tq, p0=p0, n_sel=n_sel, nselp=nselp, nsr=nsr),
        grid=(nt, bsz),
        in_specs=[pl.BlockSpec((tq, NSA_WIDTH), lambda i, b: (b * nt + i, 0)),
                  pl.BlockSpec((1, nc, KV_W), lambda i, b: (b, 0, 0)),
                  pl.BlockSpec((1, nc, KV_W), lambda i, b: (b, 0, 0)),
                  pl.BlockSpec((NSA_HEADS, tq, nc), lambda i, b: (0, i, 0)),
                  pl.BlockSpec((nc, nselp), lambda i, b: (0, 0))],
        out_specs=[pl.BlockSpec((tq, NSA_WIDTH), lambda i, b: (b * nt + i, 0)), sel_spec],
        out_shape=[jax.ShapeDtypeStruct((bsz * t, NSA_WIDTH), F32), sel_shape],
        compiler_params=_cp("arbitrary", "arbitrary"),
        name="cmp_attn_select",
    )(qn, kcmp, vcmp, biasc, overlap)


def _stack_heads(q, g):
    return jnp.concatenate([q[:, (NSA_REP * g + r) * HEAD_DIM:(NSA_REP * g + r + 1) * HEAD_DIM]
                            for r in range(NSA_REP)], axis=0)


def _unstack_heads(o_groups, tq):
    return jnp.concatenate([o[r * tq:(r + 1) * tq] for o in o_groups for r in range(NSA_REP)], axis=1)


def _online_update(carry, s, v):
    m, l, acc = carry
    m_new = jnp.maximum(m, jnp.max(s, axis=-1, keepdims=True))
    a = jnp.exp(m - m_new)
    p = jnp.exp(s - m_new)
    return m_new, a * l + jnp.sum(p, axis=-1, keepdims=True), a * acc + _dot(p, v)


def _gated_sum(ng, ge, ocmp, osel, owin):
    gx = _dot2(jax.nn.sigmoid(ng), ge)
    return gx[:, 0:512] * ocmp + gx[:, 512:1024] * osel + gx[:, 1024:1536] * owin


def _prompt_attn_kernel(qn_ref, ks_ref, vs_ref, selm_ref, e_ref, bnear_ref, kw_ref, vw_ref, bw_ref,
                        ocmp_ref, ng_ref, ge_ref, oc_ref, madd_ref, *, tq, nselp, nprev, kt):
    i = pl.program_id(1)
    q = qn_ref[...]
    rows = NSA_REP * tq
    n_far = jnp.maximum(i - 1, 0)
    n_macro = n_far // kt
    init = (jnp.full((rows, 1), -jnp.inf, F32), jnp.zeros((rows, 1), F32), jnp.zeros((rows, HEAD_DIM), F32))
    o_sel, o_win = [], []
    for g in range(NSA_KV):
        hs = slice(NSA_REP * g, NSA_REP * (g + 1))
        cs = slice(g * HEAD_DIM, (g + 1) * HEAD_DIM)
        qg = _stack_heads(q, g)
        hit = jnp.dot(selm_ref[:, g * nselp:(g + 1) * nselp], e_ref[...], preferred_element_type=F32)
        madd_ref[g] = (hit - 1.0) * (-NEG)

        def step(r0, width, carry, bias):
            add = madd_ref[g, :, pl.ds(r0, width)][None]
            if bias is not None:
                add = add + bias
            s = _dot_nt(qg, ks_ref[pl.ds(r0, width), cs]).reshape(NSA_REP, tq, width) + add
            return _online_update(carry, s.reshape(rows, width), vs_ref[pl.ds(r0, width), cs])

        carry = lax.fori_loop(0, n_macro, lambda j, c: step(pl.multiple_of(j * kt * tq, kt * tq), kt * tq, c, None), init)
        carry = lax.fori_loop(n_macro * kt, n_far, lambda j, c: step(pl.multiple_of(j * tq, tq), tq, c, None), carry)
        _, l, acc = lax.cond(
            i >= 1,
            lambda c: step(pl.multiple_of((i - 1) * tq, tq), 2 * tq, c, bnear_ref[hs]),
            lambda c: step(0, tq, c, bnear_ref[hs, :, tq:2 * tq]),
            carry)
        o_sel.append(acc / l)

        nk = (nprev + 1) * tq
        r0 = pl.multiple_of(i * tq, tq)
        s = _dot_nt(qg, kw_ref[0, pl.ds(r0, nk), cs]).reshape(NSA_REP, tq, nk) + bw_ref[hs]
        kval = lax.broadcasted_iota(jnp.int32, (1, 1, nk), 2) >= (nprev - i) * tq
        s = jnp.where(kval, s, NEG).reshape(rows, nk)
        _, l, acc = _online_update(init, s, vw_ref[0, pl.ds(r0, nk), cs])
        o_win.append(acc / l)

    oc = _gated_sum(ng_ref[...], ge_ref[...], ocmp_ref[...], _unstack_heads(o_sel, tq), _unstack_heads(o_win, tq))
    oc_ref[...] = oc.astype(BF16)


def _rel_bucket(dist):
    n = jnp.maximum(dist, 0)
    exact = N_BUCKETS // 2
    nf = jnp.maximum(n, 1).astype(F32)
    large = exact + (jnp.log(nf / exact) / math.log(MAX_DIST / exact) * (N_BUCKETS - exact)).astype(jnp.int32)
    return jnp.where(n < exact, n, jnp.minimum(large, N_BUCKETS - 1))


def _bias_last(rel_bias, dist, valid=None):
    onehot = (_rel_bucket(dist)[..., None] == jnp.arange(N_BUCKETS)).astype(F32)
    b = jnp.einsum("...k,kh->...h", onehot, rel_bias.astype(F32), precision=lax.Precision.HIGHEST)
    if valid is not None:
        b = jnp.where(valid[..., None], b, NEG)
    return b


def _bias_table(rel_bias, dist, valid=None):
    return jnp.moveaxis(_bias_last(rel_bias, dist, valid), -1, 0)


def _bias_table_t(rel_bias, dist, valid=None):
    b = jnp.swapaxes(_bias_last(rel_bias, dist, valid), -1, -2)
    return b.reshape(b.shape[:-2] + (b.shape[-2] * b.shape[-1],))


def _gate_expand():
    ge = np.zeros((LANE, 3 * NSA_WIDTH), np.float32)
    for br in range(3):
        for h in range(NSA_HEADS):
            ge[br * NSA_HEADS + h, br * NSA_WIDTH + h * HEAD_DIM: br * NSA_WIDTH + (h + 1) * HEAD_DIM] = 1.0
    return jnp.asarray(ge, BF16)


def _block_expand(nselp, length):
    e = (np.arange(nselp)[:, None] == (np.arange(length) // SEL_BLOCK)[None, :]).astype(np.float32)
    return jnp.asarray(e, BF16)


def prompt_attn_tables(rel_bias, tq):
    nprev = WINDOW // tq
    nk = (nprev + 1) * tq
    ar = jnp.arange
    dn = tq + ar(tq)[:, None] - ar(2 * tq)[None, :]
    dw = ar(tq)[:, None] + nprev * tq - ar(nk)[None, :]
    assert tq + 1 >= MAX_DIST
    far = rel_bias[N_BUCKETS - 1].astype(F32)[:, None, None]
    return {"near": _bias_table(rel_bias, dn, dn >= 0) - far,
            "win": _bias_table(rel_bias, dw, (dw >= 0) & (dw < WINDOW))}


def prompt_attn(qn, rows, selm, ocmp, y, winrows, tabs, bsz, t, tq, kt):
    nt = t // tq
    nselp = selm.shape[1] // NSA_KV
    nprev = WINDOW // tq
    nk = (nprev + 1) * tq
    winp = jnp.pad(winrows.reshape(bsz, t, 2 * KV_W), ((0, 0), (nprev * tq, 0), (0, 0)))
    const = lambda shape: pl.BlockSpec(shape, lambda b, i: tuple(0 for _ in shape))
    tile = lambda w, c: pl.BlockSpec((tq, w), lambda b, i: (b * nt + i, c))
    return pl.pallas_call(
        functools.partial(_prompt_attn_kernel, tq=tq, nselp=nselp, nprev=nprev, kt=kt),
        grid=(bsz, nt),
        in_specs=[tile(NSA_WIDTH, 0),
                  pl.BlockSpec((t, KV_W), lambda b, i: (b, 2)),
                  pl.BlockSpec((t, KV_W), lambda b, i: (b, 3)),
                  tile(NSA_KV * nselp, 0),
                  const((nselp, t)), const((NSA_HEADS, tq, 2 * tq)),
                  pl.BlockSpec((1, t + nprev * tq, KV_W), lambda b, i: (b, 0, 0)),
                  pl.BlockSpec((1, t + nprev * tq, KV_W), lambda b, i: (b, 0, 1)),
                  const((NSA_HEADS, tq, nk)),
                  tile(NSA_WIDTH, 0),
                  tile(LANE, COL_NGATE // LANE),
                  const((LANE, 3 * NSA_WIDTH))],
        out_specs=tile(NSA_WIDTH, 0),
        out_shape=jax.ShapeDtypeStruct((bsz * t, NSA_WIDTH), BF16),
        scratch_shapes=[pltpu.VMEM((NSA_KV, tq, t), F32)],
        compiler_params=_cp("parallel", "arbitrary"),
        name="prompt_sel_win_attn",
    )(qn, rows, rows, selm, _block_expand(nselp, t), tabs["near"], winp, winp, tabs["win"], ocmp, y, _gate_expand())


def _page_gather_kernel(pt_ref, *refs, pg):
    cv_ref, sel_ref = refs[pg], refs[pg + 1]
    for k in range(pg):
        x = refs[k][0]
        cv_ref[0, k * PAGE_SIZE:(k + 1) * PAGE_SIZE, :] = x[:, 0:2 * KV_W].astype(BF16)
        sel_ref[0, k * PAGE_SIZE:(k + 1) * PAGE_SIZE, :] = x[:, 2 * KV_W:4 * KV_W].astype(BF16)


def page_gather(cache3, pt_flat, bsz, npages, pg):
    def page_spec(k):
        return pl.BlockSpec((1, PAGE_SIZE, 4 * KV_W), lambda b, p, pt: (pt[b * npages + p * pg + k], 0, 0))

    out = pl.BlockSpec((1, pg * PAGE_SIZE, 2 * KV_W), lambda b, p, pt: (b, p, 0))
    shp = jax.ShapeDtypeStruct((bsz, npages * PAGE_SIZE, 2 * KV_W), BF16)
    return pl.pallas_call(
        functools.partial(_page_gather_kernel, pg=pg),
        grid_spec=pltpu.PrefetchScalarGridSpec(
            num_scalar_prefetch=1, grid=(bsz, npages // pg),
            in_specs=[page_spec(k) for k in range(pg)], out_specs=[out, out]),
        out_shape=[shp, shp],
        compiler_params=_cp("parallel", "arbitrary"),
        name="page_gather",
    )(pt_flat, *([cache3] * pg))


def _decode_attn_kernel(*refs, t, nselp, masked):
    if masked:
        qn_ref, kp_ref, vp_ref, bp_ref, kn_ref, vn_ref, bn_ref, selm_ref, e_ref, o_ref, m_ref, l_ref, acc_ref = refs
    else:
        qn_ref, kp_ref, vp_ref, bp_ref, kn_ref, vn_ref, bn_ref, o_ref, m_ref, l_ref, acc_ref = refs
    j = pl.program_id(1)
    rows = NSA_REP * t

    @pl.when(j == 0)
    def _():
        m_ref[...] = jnp.full(m_ref.shape, -jnp.inf, F32)
        l_ref[...] = jnp.zeros(l_ref.shape, F32)
        acc_ref[...] = jnp.zeros(acc_ref.shape, F32)

    q = qn_ref[...]
    kp = kp_ref[0]
    vp = vp_ref[0]
    tk = kp.shape[0]
    for g in range(NSA_KV):
        hs = slice(NSA_REP * g, NSA_REP * (g + 1))
        cs = slice(g * HEAD_DIM, (g + 1) * HEAD_DIM)
        qg = _stack_heads(q, g)
        s = _dot_nt(qg, kp[:, cs]).reshape(NSA_REP, t, tk) + bp_ref[hs]
        if masked:
            msk = jnp.dot(selm_ref[:, g * nselp:(g + 1) * nselp], e_ref[...], preferred_element_type=F32) > 0.5
            s = jnp.where(msk[None], s, NEG)
        carry = _online_update((m_ref[g], l_ref[g], acc_ref[g]), s.reshape(rows, tk), vp[:, cs])
        m_ref[g], l_ref[g], acc_ref[g] = carry

    @pl.when(j == pl.num_programs(1) - 1)
    def _():
        outs = []
        for g in range(NSA_KV):
            hs = slice(NSA_REP * g, NSA_REP * (g + 1))
            cs = slice(g * HEAD_DIM, (g + 1) * HEAD_DIM)
            qg = _stack_heads(q, g)
            s = _dot_nt(qg, kn_ref[:, cs]).reshape(NSA_REP, t, t) + bn_ref[hs]
            _, l, acc = _online_update((m_ref[g], l_ref[g], acc_ref[g]), s.reshape(rows, t), vn_ref[:, cs])
            outs.append(acc / l)
        o_ref[...] = _unstack_heads(outs, t)


def decode_attn(qn, kv_past, bias_past, new_rows, new_cols, bias_new, bsz, t, tk, selm=None, e=None):
    lp = kv_past.shape[1]
    nkt = lp // tk
    masked = selm is not None
    nselp = selm.shape[1] // NSA_KV if masked else 0
    in_specs = [pl.BlockSpec((t, NSA_WIDTH), lambda b, j: (b, 0)),
                pl.BlockSpec((1, tk, KV_W), lambda b, j: (b, j, 0)),
                pl.BlockSpec((1, tk, KV_W), lambda b, j: (b, j, 1)),
                pl.BlockSpec((NSA_HEADS, t, tk), lambda b, j: (0, 0, j)),
                pl.BlockSpec((t, KV_W), lambda b, j: (b, new_cols[0])),
                pl.BlockSpec((t, KV_W), lambda b, j: (b, new_cols[1])),
                pl.BlockSpec((NSA_HEADS, t, t), lambda b, j: (0, 0, 0))]
    args = [qn, kv_past, kv_past, bias_past, new_rows, new_rows, bias_new]
    if masked:
        in_specs += [pl.BlockSpec((t, NSA_KV * nselp), lambda b, j: (b, 0)),
                     pl.BlockSpec((nselp, tk), lambda b, j: (0, j))]
        args += [selm, e]
    rows = NSA_REP * t
    return pl.pallas_call(
        functools.partial(_decode_attn_kernel, t=t, nselp=nselp, masked=masked),
        grid=(bsz, nkt),
        in_specs=in_specs,
        out_specs=pl.BlockSpec((t, NSA_WIDTH), lambda b, j: (b, 0)),
        out_shape=jax.ShapeDtypeStruct((bsz * t, NSA_WIDTH), F32),
        scratch_shapes=[pltpu.VMEM((NSA_KV, rows, 1), F32), pltpu.VMEM((NSA_KV, rows, 1), F32),
                        pltpu.VMEM((NSA_KV, rows, HEAD_DIM), F32)],
        compiler_params=_cp("parallel", "arbitrary"),
        name="decode_sel_attn" if masked else "decode_win_attn",
    )(*args)


def _combine_kernel(ng_ref, ge_ref, ocmp_ref, osel_ref, owin_ref, oc_ref):
    oc_ref[...] = _gated_sum(ng_ref[...], ge_ref[...], ocmp_ref[...], osel_ref[...], owin_ref[...]).astype(BF16)


def combine(y, ocmp, osel, owin):
    m = ocmp.shape[0]
    full = pl.BlockSpec((m, NSA_WIDTH), lambda i: (0, 0))
    return pl.pallas_call(
        _combine_kernel,
        grid=(1,),
        in_specs=[pl.BlockSpec((m, LANE), lambda i: (0, COL_NGATE // LANE)),
                  pl.BlockSpec((LANE, 3 * NSA_WIDTH), lambda i: (0, 0)), full, full, full],
        out_specs=full,
        out_shape=jax.ShapeDtypeStruct((m, NSA_WIDTH), BF16),
        compiler_params=_cp("arbitrary"),
        name="nsa_combine",
    )(y, _gate_expand(), ocmp, osel, owin)


def _overlap(nc, n_sel, nselp):
    cs = np.arange(nc) * CMP_STRIDE
    ss = np.arange(nselp) * SEL_BLOCK
    ov = (cs[:, None] < ss[None, :] + SEL_BLOCK) & (cs[:, None] + CMP_LEN > ss[None, :])
    ov &= (np.arange(nc) < nc - 1)[:, None] & (np.arange(nselp) < n_sel)[None, :]
    return jnp.asarray(ov.astype(np.float32), BF16)


def _round_up(x, m):
    return -(-x // m) * m


def position_tables(rel_bias, t, p0, lwin):
    nc = (t if p0 == 0 else p0) // CMP_STRIDE
    ar = jnp.arange
    qpos = p0 + ar(t)
    kend = ar(nc) * CMP_STRIDE + CMP_LEN - 1
    tabs = {"cmp": _bias_table(rel_bias, qpos[:, None] - kend[None, :])}
    if p0 == 0:
        tabs["attn"] = prompt_attn_tables(rel_bias, min(t, 128))
    else:
        dn = ar(t)[:, None] - ar(t)[None, :]
        tabs["new"] = _bias_table(rel_bias, dn, dn >= 0)
        tabs["sel"] = _bias_table(rel_bias, qpos[:, None] - ar(p0)[None, :])
        dw = qpos[:, None] - (p0 - lwin + ar(lwin))[None, :]
        tabs["win"] = _bias_table(rel_bias, dw, (dw >= 0) & (dw < WINDOW))
    return tabs


def layer(x, past, lw, tabs, bsz, t):
    m = bsz * t
    prompt = past is None
    tm = min(512, m)
    y = rms_matmul(x, lw["norm_mix"], lw["w_in"], tm, 1408)

    c = math.gcd(t, HG_CHUNK)
    s0 = jnp.zeros((bsz, HG_HEADS, HG_DIM, HG_DIM), F32) if prompt else past["hgrn"]
    oa, s_hg = hgrn(y, lw["lb"], lw["hg_norm"], s0, bsz, t, c, min(t, 256))

    ts = min(t, 256)
    if prompt:
        h0r = h0i = jnp.zeros((bsz, 1, S5_CH), F32)
    else:
        h0r, h0i = past["s5r"].reshape(bsz, 1, S5_CH), past["s5i"].reshape(bsz, 1, S5_CH)
    ob, s5r, s5i = s5(y, lw["s5"], h0r, h0i, bsz, t, ts)

    qn, rows, winrows, cv = nsa_prep(y, lw["qkn"], tm)
    if prompt:
        p0, nc = 0, t // CMP_STRIDE
        chunks = cv.reshape(m // CMP_STRIDE, CMP_STRIDE * 2 * KV_W)
    else:
        npages = past["npages"]
        p0 = npages * PAGE_SIZE
        nc = p0 // CMP_STRIDE
        cv_past, sel_past = page_gather(past["cache3"], past["pt_flat"], bsz, npages, min(16, npages))
        chunks = cv_past.reshape(bsz * nc, CMP_STRIDE * 2 * KV_W)
    cacb = matmul(chunks, lw["cmp_wab"], min(512, chunks.shape[0]))
    kcmp, vcmp = cmp_post(cacb.reshape(bsz, nc, 1024), lw["cmp_bias"], lw["cmp_w2"], lw["qkn"][1])
    n_sel = -(-(p0 + t) // SEL_BLOCK)
    nselp = _round_up(n_sel, LANE)
    tq = min(t, 128)
    nsr = _round_up(n_sel, 8) if prompt else 0
    ocmp, selm = cmpsel(qn, kcmp, vcmp, tabs["cmp"], _overlap(nc, n_sel, nselp), bsz, t, tq, p0, n_sel, nsr)
    if prompt:
        oc = prompt_attn(qn, rows, selm, ocmp, y, winrows, tabs["attn"], bsz, t, tq, 4)
        lw_ = min(WINDOW, t)
        new_win = winrows.reshape(bsz, t, 2 * KV_W)[:, t - lw_:]
    else:
        tk = min(2048, p0)
        osel = decode_attn(qn, sel_past, tabs["sel"], rows, (2, 3), tabs["new"], bsz, t, tk,
                           selm=selm, e=_block_expand(nselp, p0))
        win = past["win"]
        owin = decode_attn(qn, win, tabs["win"], winrows, (0, 1), tabs["new"], bsz, t, win.shape[1])
        oc = combine(y, ocmp, osel, owin)
        new_win = jnp.concatenate([win, winrows.reshape(bsz, t, 2 * KV_W)], axis=1)[:, t:]

    x1 = merge(x, y, oa, ob, oc, lw["w_branch"], lw["w_out"], tm)
    x2 = ffn(x1, lw["norm_ffn"], lw["w_gate_up"], lw["w_down"], tm, 1408)
    return x2, (rows, new_win, s_hg, s5r, s5i)


def layer_weights(l, ts_list, norm_mix, w_in, lower_bounds, hg_out_norm, s5_a_re, s5_a_im, s5_log_dt, s5_b_re, s5_b_im,
                  s5_c_re, s5_c_im, s5_d, s5_w_glu, nsa_qk_norm, cmp_pe, cmp_w1, cmp_w2, w_branch, w_out, norm_ffn,
                  w_gate_up, w_down):
    wab, cbias, w2blk = cmp_params(cmp_pe[l], cmp_w1[l], cmp_w2[l])
    s5p = {ts: s5_params(s5_a_re[l], s5_a_im[l], s5_log_dt[l], s5_b_re[l], s5_b_im[l], s5_c_re[l], s5_c_im[l],
                         s5_d[l], s5_w_glu[l], ts) for ts in ts_list}
    return {"norm_mix": norm_mix[l].reshape(1, D_MODEL), "w_in": prep_w_in(w_in[l]),
            "lb": lower_bounds[l].reshape(1, HG_WIDTH), "hg_norm": hg_out_norm[l].reshape(1, HG_DIM),
            "s5_by_ts": s5p, "qkn": nsa_qk_norm[l], "cmp_wab": wab, "cmp_bias": cbias, "cmp_w2": w2blk,
            "w_branch": w_branch[l].astype(BF16), "w_out": w_out[l].astype(BF16),
            "norm_ffn": norm_ffn[l].reshape(1, D_MODEL), "w_gate_up": w_gate_up[l].astype(BF16),
            "w_down": w_down[l].astype(BF16)}


def kernel(x_prompt, x_sample, cache_nsa_kv, cache_win_kv, state_hgrn, state_s5_re, state_s5_im, page_table,
           norm_mix, w_in, hg_lb_logits, hg_out_norm, s5_a_re, s5_a_im, s5_log_dt, s5_b_re, s5_b_im,
           s5_c_re, s5_c_im, s5_d, s5_w_glu, nsa_qk_norm, cmp_pe, cmp_w1, cmp_w2, rel_bias,
           w_branch, w_out, norm_ffn, w_gate_up, w_down):
    depth = w_in.shape[0]
    bp, tp, d = x_prompt.shape
    bs, tsm, _ = x_sample.shape
    n_phys = cache_nsa_kv.shape[1]
    npages = page_table.shape[1]
    lb_sm = jax.nn.softmax(hg_lb_logits.astype(F32), axis=0)
    lower_bounds = jnp.cumsum(lb_sm, axis=0) - lb_sm[0]
    ts_p, ts_s = min(tp, 256), min(tsm, 256)
    hp = x_prompt.reshape(bp * tp, d)
    hs = x_sample.reshape(bs * tsm, d)
    cache3 = cache_nsa_kv.reshape(depth * n_phys, PAGE_SIZE, 4 * KV_W)
    tabs_p = position_tables(rel_bias, tp, 0, 0)
    tabs_s = position_tables(rel_bias, tsm, npages * PAGE_SIZE, cache_win_kv.shape[2])
    st_p, st_s = [], []
    for l in range(depth):
        lw = layer_weights(l, sorted({ts_p, ts_s}), norm_mix, w_in, lower_bounds, hg_out_norm, s5_a_re, s5_a_im,
                           s5_log_dt, s5_b_re, s5_b_im, s5_c_re, s5_c_im, s5_d, s5_w_glu, nsa_qk_norm, cmp_pe,
                           cmp_w1, cmp_w2, w_branch, w_out, norm_ffn, w_gate_up, w_down)
        past = {"cache3": cache3, "pt_flat": (page_table + l * n_phys).reshape(-1).astype(jnp.int32),
                "npages": npages, "win": cache_win_kv[l].reshape(bs, -1, 2 * KV_W), "hgrn": state_hgrn[l],
                "s5r": state_s5_re[l], "s5i": state_s5_im[l]}
        hp, sp = layer(hp, None, dict(lw, s5=lw["s5_by_ts"][ts_p]), tabs_p, bp, tp)
        hs, ss = layer(hs, past, dict(lw, s5=lw["s5_by_ts"][ts_s]), tabs_s, bs, tsm)
        st_p.append(sp)
        st_s.append(ss)

    def stack(states, k, shape):
        return jnp.stack([s[k].reshape(shape) for s in states])

    kvs = (4, NSA_KV, HEAD_DIM)
    return (hp.reshape(bp, tp, d), hs.reshape(bs, tsm, d),
            stack(st_p, 0, (bp, tp) + kvs), stack(st_s, 0, (bs, tsm) + kvs),
            stack(st_p, 1, (bp, -1, 2, NSA_KV, HEAD_DIM)), stack(st_s, 1, (bs, -1, 2, NSA_KV, HEAD_DIM)),
            stack(st_p, 2, (bp, HG_HEADS, HG_DIM, HG_DIM)), stack(st_s, 2, (bs, HG_HEADS, HG_DIM, HG_DIM)),
            stack(st_p, 3, (bp, S5_GROUPS, S5_STATE)), stack(st_p, 4, (bp, S5_GROUPS, S5_STATE)),
            stack(st_s, 3, (bs, S5_GROUPS, S5_STATE)), stack(st_s, 4, (bs, S5_GROUPS, S5_STATE)))
```

```python
import functools
import math

import numpy as np
import jax
import jax.numpy as jnp
from jax import lax
from jax.experimental import pallas as pl
from jax.experimental.pallas import tpu as pltpu

F32 = jnp.float32
BF16 = jnp.bfloat16

D_MODEL = 1024
HG_HEADS = 4
HG_DIM = 128
HG_WIDTH = HG_HEADS * HG_DIM
HG_CHUNK = 64
LB_FLOOR = 1e-30
S5_GROUP = 16
S5_GROUPS = 32
S5_WIDTH = S5_GROUP * S5_GROUPS
S5_STATE = 64
S5_CH = S5_GROUPS * S5_STATE
NSA_HEADS = 8
NSA_KV = 2
NSA_REP = NSA_HEADS // NSA_KV
HEAD_DIM = 64
NSA_WIDTH = NSA_HEADS * HEAD_DIM
KV_W = NSA_KV * HEAD_DIM
CMP_LEN = 32
CMP_STRIDE = 16
CMP_HIDDEN = 128
SEL_BLOCK = 64
SEL_TOPN = 16
WINDOW = 512
N_BUCKETS = 32
MAX_DIST = 128
PAGE_SIZE = 128
D_FF = 2816
NEG = -1e30
FORCE = 1e9
EPS = 1e-6

COL_GA, COL_GB, COL_GC = 0, 1024, 2048
COL_HQ, COL_HF, COL_HI, COL_HG = 3072, 3584, 4096, 4608
COL_SU, COL_NQ, COL_KV4, COL_WKV, COL_NGATE = 5120, 5632, 6144, 6656, 6912
N_IN_PAD = 7040
N_IN = 6936

LANE = 128
VMEM_LIMIT = 56 * 1024 * 1024


def _cp(*sem):
    return pltpu.CompilerParams(dimension_semantics=sem, vmem_limit_bytes=VMEM_LIMIT)


def _dot(a, b):
    return jnp.dot(a.astype(BF16), b.astype(BF16), preferred_element_type=F32)


def _dot_nt(a, b):
    return lax.dot_general(a.astype(BF16), b.astype(BF16), (((1,), (1,)), ((), ())), preferred_element_type=F32)


def _dot_tn(a, b):
    return lax.dot_general(a.astype(BF16), b.astype(BF16), (((0,), (0,)), ((), ())), preferred_element_type=F32)


def _split2(x):
    hi = x.astype(BF16)
    lo = (x - hi.astype(F32)).astype(BF16)
    return hi, lo


def _dot2(x, w):
    hi, lo = _split2(x)
    return jnp.dot(hi, w, preferred_element_type=F32) + jnp.dot(lo, w, preferred_element_type=F32)


def _silu(x):
    return x * jax.nn.sigmoid(x)


def _gelu_tanh(x):
    return 0.5 * x * (1.0 + jnp.tanh(math.sqrt(2.0 / math.pi) * (x + 0.044715 * (x * x * x))))


def _rms_mm_kernel(x_ref, g_ref, w_ref, o_ref, xn_ref):
    @pl.when(pl.program_id(1) == 0)
    def _():
        x = x_ref[...]
        ms = jnp.mean(x * x, axis=-1, keepdims=True)
        xn_ref[...] = (x * lax.rsqrt(ms + EPS) * g_ref[...]).astype(BF16)

    o_ref[...] = jnp.dot(xn_ref[...], w_ref[...], preferred_element_type=F32)


def rms_matmul(x, g, w, tm, tn):
    m, k = x.shape
    n = w.shape[1]
    return pl.pallas_call(
        _rms_mm_kernel,
        grid=(m // tm, n // tn),
        in_specs=[pl.BlockSpec((tm, k), lambda i, j: (i, 0)),
                  pl.BlockSpec((1, k), lambda i, j: (0, 0)),
                  pl.BlockSpec((k, tn), lambda i, j: (0, j))],
        out_specs=pl.BlockSpec((tm, tn), lambda i, j: (i, j)),
        out_shape=jax.ShapeDtypeStruct((m, n), F32),
        scratch_shapes=[pltpu.VMEM((tm, k), BF16)],
        compiler_params=_cp("parallel", "arbitrary"),
        name="rms_in_proj",
    )(x, g, w)


def _mm_kernel(x_ref, w_ref, o_ref):
    o_ref[...] = jnp.dot(x_ref[...], w_ref[...], preferred_element_type=F32)


def matmul(x, w, tm):
    m, k = x.shape
    n = w.shape[1]
    return pl.pallas_call(
        _mm_kernel,
        grid=(m // tm,),
        in_specs=[pl.BlockSpec((tm, k), lambda i: (i, 0)),
                  pl.BlockSpec((k, n), lambda i: (0, 0))],
        out_specs=pl.BlockSpec((tm, n), lambda i: (i, 0)),
        out_shape=jax.ShapeDtypeStruct((m, n), F32),
        compiler_params=_cp("parallel"),
        name="cmp_matmul",
    )(x, w)


def _merge_kernel(x_ref, ga_ref, gb_ref, gc_ref, oa_ref, ob_ref, oc_ref, wb_ref, wo_ref, o_ref):
    m = jax.nn.sigmoid(ga_ref[...]) * jnp.dot(oa_ref[...], wb_ref[0], preferred_element_type=F32)
    m = m + jax.nn.sigmoid(gb_ref[...]) * jnp.dot(ob_ref[...], wb_ref[1], preferred_element_type=F32)
    m = m + jax.nn.sigmoid(gc_ref[...]) * jnp.dot(oc_ref[...], wb_ref[2], preferred_element_type=F32)
    o_ref[...] = x_ref[...] + jnp.dot(m.astype(BF16), wo_ref[...], preferred_element_type=F32)


def merge(x, y, oa, ob, oc, wb, wo, tm):
    m, d = x.shape
    return pl.pallas_call(
        _merge_kernel,
        grid=(m // tm,),
        in_specs=[pl.BlockSpec((tm, d), lambda i: (i, 0)),
                  pl.BlockSpec((tm, d), lambda i: (i, COL_GA // D_MODEL)),
                  pl.BlockSpec((tm, d), lambda i: (i, COL_GB // D_MODEL)),
                  pl.BlockSpec((tm, d), lambda i: (i, COL_GC // D_MODEL)),
                  pl.BlockSpec((tm, 512), lambda i: (i, 0)),
                  pl.BlockSpec((tm, 512), lambda i: (i, 0)),
                  pl.BlockSpec((tm, 512), lambda i: (i, 0)),
                  pl.BlockSpec((3, 512, d), lambda i: (0, 0, 0)),
                  pl.BlockSpec((d, d), lambda i: (0, 0))],
        out_specs=pl.BlockSpec((tm, d), lambda i: (i, 0)),
        out_shape=jax.ShapeDtypeStruct((m, d), F32),
        compiler_params=_cp("parallel"),
        name="merge_out_proj",
    )(x, y, y, y, oa, ob, oc, wb, wo)


def _ffn_kernel(x_ref, g_ref, wg_ref, wu_ref, wd_ref, o_ref, xn_ref):
    @pl.when(pl.program_id(1) == 0)
    def _():
        x = x_ref[...]
        ms = jnp.mean(x * x, axis=-1, keepdims=True)
        xn_ref[...] = (x * lax.rsqrt(ms + EPS) * g_ref[...]).astype(BF16)
        o_ref[...] = x

    xn = xn_ref[...]
    gate = jnp.dot(xn, wg_ref[...], preferred_element_type=F32)
    up = jnp.dot(xn, wu_ref[...], preferred_element_type=F32)
    h = (_silu(gate) * up).astype(BF16)
    o_ref[...] += jnp.dot(h, wd_ref[...], preferred_element_type=F32)


def ffn(x, g, wgu, wd, tm, tf):
    m, d = x.shape
    nf = D_FF // tf
    return pl.pallas_call(
        _ffn_kernel,
        grid=(m // tm, nf),
        in_specs=[pl.BlockSpec((tm, d), lambda i, f: (i, 0)),
                  pl.BlockSpec((1, d), lambda i, f: (0, 0)),
                  pl.BlockSpec((d, tf), lambda i, f: (0, f)),
                  pl.BlockSpec((d, tf), lambda i, f: (0, nf + f)),
                  pl.BlockSpec((tf, d), lambda i, f: (f, 0))],
        out_specs=pl.BlockSpec((tm, d), lambda i, f: (i, 0)),
        out_shape=jax.ShapeDtypeStruct((m, d), F32),
        scratch_shapes=[pltpu.VMEM((tm, d), BF16)],
        compiler_params=_cp("parallel", "arbitrary"),
        name="swiglu_ffn",
    )(x, g, wgu, wgu, wd)


def _hgrn_levels(c):
    return [m for m in (32, 16, 8, 4, 2, 1) if m < c]


def _hgrn_consts(c):
    mats = [np.tril(np.ones((c, c), np.float32))]
    r = np.arange(c)
    for m in _hgrn_levels(c):
        pos = r % (2 * m)
        mid = (r // (2 * m)) * 2 * m + m
        up = np.zeros((c, c), np.float32)
        lo = np.zeros((c, c), np.float32)
        for t in range(c):
            if pos[t] >= m:
                up[t, mid[t]:t + 1] = 1.0
            else:
                lo[t, t + 1:mid[t]] = 1.0
        mats += [up, lo]
    return np.concatenate(mats, axis=0)


def _hgrn_kernel(hq_ref, hf_ref, hi_ref, hg_ref, lb_ref, gn_ref, s0_ref, mst_ref, o_ref, sout_ref, st_ref,
                 *, c, nchunk):
    tb = pl.program_id(2)

    @pl.when(tb == 0)
    def _():
        st_ref[...] = s0_ref[0, 0].T

    levels = _hgrn_levels(c)
    lb = lb_ref[...]
    log_lb = jnp.log(jnp.maximum(lb, LB_FLOOR))
    log_1m = jnp.log1p(-lb)
    one_m = 1.0 - lb
    gn = gn_ref[...]
    mst = mst_ref[...]
    ti = lax.broadcasted_iota(jnp.int32, (c, c), 0)
    si = lax.broadcasted_iota(jnp.int32, (c, c), 1)
    txs = jnp.bitwise_xor(ti, si)
    lower = ti > si

    for ch in range(nchunk):
        rows = pl.ds(ch * c, c)
        fp = hf_ref[rows, :]
        hq = hq_ref[rows, :]
        v = hi_ref[rows, :]
        hg = hg_ref[rows, :]
        log_sig = -(jnp.maximum(-fp, 0.0) + jnp.log1p(jnp.exp(-jnp.abs(fp))))
        b = log_1m + log_sig
        g = jnp.maximum(log_lb, b) + jnp.log1p(jnp.exp(-jnp.abs(log_lb - b)))
        kin = one_m * jax.nn.sigmoid(-fp)
        q = _silu(hq)

        g1 = g.astype(BF16)
        r1 = g - g1.astype(F32)
        g2 = r1.astype(BF16)
        g3 = (r1 - g2.astype(F32)).astype(BF16)
        gs = jnp.concatenate([g1, g2, g3], axis=1)
        rr = jnp.dot(mst, gs, preferred_element_type=F32)
        rr = rr[:, 0:LANE] + rr[:, LANE:2 * LANE] + rr[:, 2 * LANE:3 * LANE]
        gcum = rr[0:c]

        vb = v.astype(BF16)
        att = jnp.where(ti == si, _dot_nt(q, kin), 0.0)
        for li, m in enumerate(levels):
            dq = rr[(1 + 2 * li) * c:(2 + 2 * li) * c]
            ek = rr[(2 + 2 * li) * c:(3 + 2 * li) * c]
            pair = _dot_nt(q * jnp.exp(dq), kin * jnp.exp(ek))
            sel = lower & (jnp.right_shift(txs, int(math.log2(m))) == 1)
            att = jnp.where(sel, pair, att)
        st = st_ref[...]
        o = _dot(att, vb) + _dot_nt(q * jnp.exp(gcum), st)
        g_end = gcum[c - 1:c, :]
        kd = kin * jnp.exp(g_end - gcum)
        st_ref[...] = st * jnp.exp(g_end) + _dot_tn(vb, kd)

        ms = jnp.mean(o * o, axis=-1, keepdims=True)
        on = o * lax.rsqrt(ms + EPS) * gn
        o_ref[rows, :] = (on * _silu(hg)).astype(BF16)

    @pl.when(tb == pl.num_programs(2) - 1)
    def _():
        sout_ref[0, 0] = st_ref[...].T


def hgrn(y, lb, gn, s0, bsz, t, c, tbk):
    ntb = t // tbk
    mst = jnp.asarray(_hgrn_consts(c), BF16)

    def col(base):
        return pl.BlockSpec((tbk, HG_DIM), lambda b, h, k: (b * ntb + k, base // HG_DIM + h))

    return pl.pallas_call(
        functools.partial(_hgrn_kernel, c=c, nchunk=tbk // c),
        grid=(bsz, HG_HEADS, ntb),
        in_specs=[col(COL_HQ), col(COL_HF), col(COL_HI), col(COL_HG),
                  pl.BlockSpec((1, HG_DIM), lambda b, h, k: (0, h)),
                  pl.BlockSpec((1, HG_DIM), lambda b, h, k: (0, 0)),
                  pl.BlockSpec((1, 1, HG_DIM, HG_DIM), lambda b, h, k: (b, h, 0, 0)),
                  pl.BlockSpec(mst.shape, lambda b, h, k: (0, 0))],
        out_specs=[pl.BlockSpec((tbk, HG_DIM), lambda b, h, k: (b * ntb + k, h)),
                   pl.BlockSpec((1, 1, HG_DIM, HG_DIM), lambda b, h, k: (b, h, 0, 0))],
        out_shape=[jax.ShapeDtypeStruct((bsz * t, HG_WIDTH), BF16),
                   jax.ShapeDtypeStruct((bsz, HG_HEADS, HG_DIM, HG_DIM), F32)],
        scratch_shapes=[pltpu.VMEM((HG_DIM, HG_DIM), F32)],
        compiler_params=_cp("parallel", "parallel", "arbitrary"),
        name="hgrn2_scan",
    )(y, y, y, y, lb, gn, s0, mst)


def _shift_rows(x, d):
    n = x.shape[0]
    rolled = pltpu.roll(x, d, axis=0)
    rid = lax.broadcasted_iota(jnp.int32, x.shape, 0)
    return jnp.where(rid >= d, rolled, 0.0)


def _s5_kernel(u_ref, bre_ref, bim_ref, adr_ref, adi_ref, atr_ref, ati_ref, h0r_ref, h0i_ref,
               cre_ref, cim_ref, d_ref, wg_ref, o_ref, hr_out, hi_out, cr_ref, ci_ref, *, ts):
    k = pl.program_id(1)

    @pl.when(k == 0)
    def _():
        cr_ref[...] = h0r_ref[0]
        ci_ref[...] = h0i_ref[0]

    u = u_ref[...]
    ub = u.astype(BF16)
    xr = jnp.dot(ub, bre_ref[...], preferred_element_type=F32)
    xi = jnp.dot(ub, bim_ref[...], preferred_element_type=F32)
    lev = 0
    d = 1
    while d < ts:
        ar = adr_ref[lev:lev + 1, :]
        ai = adi_ref[lev:lev + 1, :]
        sr = _shift_rows(xr, d)
        si = _shift_rows(xi, d)
        xr, xi = xr + ar * sr - ai * si, xi + ar * si + ai * sr
        d *= 2
        lev += 1
    cr = cr_ref[...]
    ci = ci_ref[...]
    atr = atr_ref[...]
    ati = ati_ref[...]
    hr = xr + atr * cr - ati * ci
    hi = xi + atr * ci + ati * cr
    cr_ref[...] = hr[ts - 1:ts, :]
    ci_ref[...] = hi[ts - 1:ts, :]
    y = jnp.dot(hr.astype(BF16), cre_ref[...], preferred_element_type=F32) \
        - jnp.dot(hi.astype(BF16), cim_ref[...], preferred_element_type=F32)
    y = _gelu_tanh(y + d_ref[...] * u)
    o_ref[...] = (y * jax.nn.sigmoid(jnp.dot(y.astype(BF16), wg_ref[...], preferred_element_type=F32))).astype(BF16)

    @pl.when(k == pl.num_programs(1) - 1)
    def _():
        hr_out[0] = hr[ts - 1:ts, :]
        hi_out[0] = hi[ts - 1:ts, :]


def s5(y, prm, h0r, h0i, bsz, t, ts):
    nts = t // ts
    nlev = prm["adr"].shape[0]
    const = lambda shape: pl.BlockSpec(shape, lambda b, k: tuple(0 for _ in shape))
    state = pl.BlockSpec((1, 1, S5_CH), lambda b, k: (b, 0, 0))
    return pl.pallas_call(
        functools.partial(_s5_kernel, ts=ts),
        grid=(bsz, nts),
        in_specs=[pl.BlockSpec((ts, S5_WIDTH), lambda b, k: (b * nts + k, COL_SU // S5_WIDTH)),
                  const((S5_WIDTH, S5_CH)), const((S5_WIDTH, S5_CH)),
                  const((nlev, S5_CH)), const((nlev, S5_CH)),
                  const((ts, S5_CH)), const((ts, S5_CH)),
                  state, state,
                  const((S5_CH, S5_WIDTH)), const((S5_CH, S5_WIDTH)),
                  const((1, S5_WIDTH)), const((S5_WIDTH, S5_WIDTH))],
        out_specs=[pl.BlockSpec((ts, S5_WIDTH), lambda b, k: (b * nts + k, 0)), state, state],
        out_shape=[jax.ShapeDtypeStruct((bsz * t, S5_WIDTH), BF16),
                   jax.ShapeDtypeStruct((bsz, 1, S5_CH), F32),
                   jax.ShapeDtypeStruct((bsz, 1, S5_CH), F32)],
        scratch_shapes=[pltpu.VMEM((1, S5_CH), F32), pltpu.VMEM((1, S5_CH), F32)],
        compiler_params=_cp("parallel", "arbitrary"),
        name="s5_scan",
    )(y, prm["bre"], prm["bim"], prm["adr"], prm["adi"], prm["atr"], prm["ati"], h0r, h0i,
      prm["cre"], prm["cim"], prm["d"], prm["wglu"])


def _cmul(ar, ai, br, bi):
    return ar * br - ai * bi, ar * bi + ai * br


def s5_params(a_re, a_im, log_dt, b_re, b_im, c_re, c_im, d, w_glu, ts):
    step = jnp.exp(log_dt)[:, None]
    mag = jnp.exp(a_re * step)
    ab_re, ab_im = mag * jnp.cos(a_im * step), mag * jnp.sin(a_im * step)
    den = a_re * a_re + a_im * a_im
    z_re = ((ab_re - 1.0) * a_re + ab_im * a_im) / den
    z_im = (ab_im * a_re - (ab_re - 1.0) * a_im) / den
    bb_re = z_re[..., None] * b_re - z_im[..., None] * b_im
    bb_im = z_re[..., None] * b_im + z_im[..., None] * b_re
    eye = jnp.eye(S5_GROUPS, dtype=F32)
    bd_in = lambda w: jnp.einsum("gpn,gh->gnhp", w, eye).reshape(S5_WIDTH, S5_CH).astype(BF16)
    bd_out = lambda w: jnp.einsum("gnp,gh->gphn", w, eye).reshape(S5_CH, S5_WIDTH).astype(BF16)
    ar, ai = ab_re.reshape(1, S5_CH), ab_im.reshape(1, S5_CH)
    adr, adi = [ar], [ai]
    d2 = 2
    while d2 < ts:
        r, i = _cmul(adr[-1], adi[-1], adr[-1], adi[-1])
        adr.append(r)
        adi.append(i)
        d2 *= 2
    atr, ati = ar, ai
    n = 1
    lev = 0
    while n < ts:
        pr, pi = _cmul(atr, ati, adr[lev], adi[lev])
        atr, ati = jnp.concatenate([atr, pr], 0), jnp.concatenate([ati, pi], 0)
        n *= 2
        lev += 1
    return {"bre": bd_in(bb_re), "bim": bd_in(bb_im), "cre": bd_out(c_re), "cim": bd_out(c_im),
            "adr": jnp.concatenate(adr, 0), "adi": jnp.concatenate(adi, 0), "atr": atr, "ati": ati,
            "d": d.reshape(1, S5_WIDTH), "wglu": w_glu.astype(BF16)}


def prep_w_in(w):
    pad = jnp.zeros((w.shape[0], N_IN_PAD - N_IN), w.dtype)
    return jnp.concatenate([w[:, 3864:6936], w[:, 0:3840], w[:, 3840:3864], pad], axis=1).astype(BF16)


def _seg_rms(x, ones, w):
    ss = _dot2(x * x, ones)
    return x * lax.rsqrt(ss * (1.0 / HEAD_DIM) + EPS) * w


def _nsa_prep_kernel(nq_ref, kv_ref, wk_ref, wq_ref, wks_ref, wkw_ref, ones_ref, qn_ref, rows_ref, win_ref, cv_ref):
    qn_ref[...] = _seg_rms(nq_ref[...], ones_ref[...], wq_ref[...]).astype(BF16)
    ones1 = ones_ref[0:LANE, 0:LANE]
    kv = kv_ref[...]
    ksn = _seg_rms(kv[:, 2 * KV_W:3 * KV_W], ones1, wks_ref[...])
    rows_ref[...] = jnp.concatenate([kv[:, 0:2 * KV_W], ksn, kv[:, 3 * KV_W:4 * KV_W]], axis=1)
    cv_ref[...] = kv[:, 0:2 * KV_W].astype(BF16)
    wk = wk_ref[...]
    kwn = _seg_rms(wk[:, 0:KV_W], ones1, wkw_ref[...])
    win_ref[...] = jnp.concatenate([kwn, wk[:, KV_W:2 * KV_W]], axis=1)


def _block_ones(n, blk):
    r = np.arange(n) // blk
    return jnp.asarray((r[:, None] == r[None, :]).astype(np.float32), BF16)


def nsa_prep(y, qkn, tm):
    m = y.shape[0]
    wq = (jnp.tile(qkn[0], NSA_HEADS) * HEAD_DIM ** -0.5).reshape(1, NSA_WIDTH)
    wks = jnp.tile(qkn[2], NSA_KV).reshape(1, KV_W)
    wkw = jnp.tile(qkn[3], NSA_KV).reshape(1, KV_W)
    ones = _block_ones(NSA_WIDTH, HEAD_DIM)
    const = lambda shape: pl.BlockSpec(shape, lambda i: (0, 0))
    row = lambda w: pl.BlockSpec((tm, w), lambda i: (i, 0))
    return pl.pallas_call(
        _nsa_prep_kernel,
        grid=(m // tm,),
        in_specs=[pl.BlockSpec((tm, 512), lambda i: (i, COL_NQ // 512)),
                  pl.BlockSpec((tm, 512), lambda i: (i, COL_KV4 // 512)),
                  pl.BlockSpec((tm, 256), lambda i: (i, COL_WKV // 256)),
                  const((1, NSA_WIDTH)), const((1, KV_W)), const((1, KV_W)), const((NSA_WIDTH, NSA_WIDTH))],
        out_specs=[row(512), row(512), row(256), row(256)],
        out_shape=[jax.ShapeDtypeStruct((m, NSA_WIDTH), BF16), jax.ShapeDtypeStruct((m, 4 * KV_W), F32),
                   jax.ShapeDtypeStruct((m, 2 * KV_W), F32), jax.ShapeDtypeStruct((m, 2 * KV_W), BF16)],
        compiler_params=_cp("parallel"),
        name="nsa_prep",
    )(y, y, y, wq, wks, wkw, ones)


def _cmp_post_kernel(c_ref, cb_ref, w2_ref, wk_ref, ones_ref, k_ref, v_ref, *, transposed):
    c = c_ref[0]
    if transposed:
        nc = c.shape[1]
        ca, cb = c[0:512], c[512:1024]
        lid = lax.broadcasted_iota(jnp.int32, cb.shape, 1)
        cb_next = jnp.where(lid < nc - 1, pltpu.roll(cb, nc - 1, axis=1), 0.0)
        hid = _gelu_tanh(ca + cb_next + cb_ref[...])
        out = _dot(w2_ref[...], hid)
        kc = out[0:KV_W]
        hi, lo = _split2(kc * kc)
        ss = jnp.dot(ones_ref[...], hi, preferred_element_type=F32) + jnp.dot(ones_ref[...], lo, preferred_element_type=F32)
        k_ref[0] = (kc * lax.rsqrt(ss * (1.0 / HEAD_DIM) + EPS) * wk_ref[...]).astype(BF16)
        v_ref[0] = out[KV_W:2 * KV_W].astype(BF16)
    else:
        nc = c.shape[0]
        ca, cb = c[:, 0:512], c[:, 512:1024]
        rid = lax.broadcasted_iota(jnp.int32, cb.shape, 0)
        cb_next = jnp.where(rid < nc - 1, pltpu.roll(cb, nc - 1, axis=0), 0.0)
        hid = _gelu_tanh(ca + cb_next + cb_ref[...])
        out = _dot(hid, w2_ref[...])
        k_ref[0] = _seg_rms(out[:, 0:KV_W], ones_ref[...], wk_ref[...]).T.astype(BF16)
        v_ref[0] = out[:, KV_W:2 * KV_W].T.astype(BF16)


def cmp_post(cacb, cbias, w2blk, qkn1, transposed):
    bsz = cacb.shape[0]
    nc = cacb.shape[2] if transposed else cacb.shape[1]
    wk = jnp.tile(qkn1, NSA_KV)
    ones = _block_ones(KV_W, HEAD_DIM)
    if transposed:
        cbias, w2blk, wk = cbias.reshape(512, 1), w2blk.T, wk.reshape(KV_W, 1)
    else:
        wk = wk.reshape(1, KV_W)
    const = lambda a: pl.BlockSpec(a.shape, lambda b: (0, 0))
    out = pl.BlockSpec((1, KV_W, nc), lambda b: (b, 0, 0))
    return pl.pallas_call(
        functools.partial(_cmp_post_kernel, transposed=transposed),
        grid=(bsz,),
        in_specs=[pl.BlockSpec((1,) + cacb.shape[1:], lambda b: (b, 0, 0)),
                  const(cbias), const(w2blk), const(wk), const(ones)],
        out_specs=[out, out],
        out_shape=[jax.ShapeDtypeStruct((bsz, KV_W, nc), BF16)] * 2,
        compiler_params=_cp("parallel"),
        name="cmp_post",
    )(cacb, cbias, w2blk, wk, ones)


def _cmp_paged_kernel(pt_ref, *refs, pg):
    pages = refs[:pg]
    wt_ref, eye_ref, o_ref, tok_ref = refs[pg:]
    nhalf = 2 * KV_W // LANE
    for k in range(pg):
        tok = _dot_nt(eye_ref[...], pages[k][0])
        for c in range(nhalf):
            tok_ref[c, k * PAGE_SIZE:(k + 1) * PAGE_SIZE, :] = tok[:, c * LANE:(c + 1) * LANE]
    nch = pg * PAGE_SIZE // CMP_STRIDE
    acc = jnp.zeros((1024, nch), F32)
    for j in range(CMP_STRIDE):
        rj = jnp.concatenate([tok_ref[c, pl.ds(j, nch, stride=CMP_STRIDE), :] for c in range(nhalf)], axis=1)
        acc = acc + _dot_nt(wt_ref[j], rj)
    o_ref[0] = acc


def cmp_paged(cache_t, pt_flat, wab, bsz, npages, pg):
    nch = pg * PAGE_SIZE // CMP_STRIDE
    nc = npages * PAGE_SIZE // CMP_STRIDE
    wt = wab.reshape(CMP_STRIDE, 2 * KV_W, 1024).transpose(0, 2, 1)
    eye = jnp.eye(PAGE_SIZE, dtype=BF16)

    def page_spec(k):
        return pl.BlockSpec((1, 2 * KV_W, PAGE_SIZE), lambda b, p, pt: (pt[b * npages + p * pg + k], 0, 0))

    return pl.pallas_call(
        functools.partial(_cmp_paged_kernel, pg=pg),
        grid_spec=pltpu.PrefetchScalarGridSpec(
            num_scalar_prefetch=1, grid=(bsz, npages // pg),
            in_specs=[page_spec(k) for k in range(pg)] +
                     [pl.BlockSpec(wt.shape, lambda b, p, pt: (0, 0, 0)), pl.BlockSpec(eye.shape, lambda b, p, pt: (0, 0))],
            out_specs=pl.BlockSpec((1, 1024, nch), lambda b, p, pt: (b, 0, p)),
            scratch_shapes=[pltpu.VMEM((2 * KV_W // LANE, pg * PAGE_SIZE, LANE), F32)]),
        out_shape=jax.ShapeDtypeStruct((bsz, 1024, nc), F32),
        compiler_params=_cp("parallel", "arbitrary"),
        name="cmp_paged",
    )(pt_flat, *([cache_t] * pg), wt, eye)


def cmp_params(pe, w1, w2):
    w1r = w1.reshape(2, 2, 16, HEAD_DIM, CMP_HIDDEN)
    eye2 = jnp.eye(2, dtype=F32)
    wab = jnp.einsum("khjdc,kl,gm->jkgdhlmc", w1r, eye2, eye2).reshape(16 * 256, 1024).astype(BF16)
    cb = jnp.einsum("kf,kfc->kc", pe.reshape(2, CMP_LEN * HEAD_DIM), w1)
    cbias = jnp.broadcast_to(cb[:, None, :], (2, NSA_KV, CMP_HIDDEN)).reshape(1, 512)
    w2blk = jnp.einsum("kcd,kl,gm->kgclmd", w2, eye2, eye2).reshape(512, 256).astype(BF16)
    return wab, cbias, w2blk


def _cmpsel_kernel(qn_ref, kc_ref, vc_ref, bias_ref, ov_ref, ocmp_ref, selm_ref, *, tq, p0, n_sel, nselp, nsr):
    i = pl.program_id(0)
    q = qn_ref[...]
    kc = kc_ref[0]
    vc = vc_ref[0]
    nc = kc.shape[1]
    qpos = p0 + i * tq + lax.broadcasted_iota(jnp.int32, (tq, 1), 0)
    kend = lax.broadcasted_iota(jnp.int32, (1, nc), 1) * CMP_STRIDE + (CMP_LEN - 1)
    valid = qpos >= kend
    row_valid = (qpos >= CMP_LEN - 1).astype(F32)
    imp = [None, None]
    outs = []
    for h in range(NSA_HEADS):
        g = h // NSA_REP
        s = _dot(q[:, h * HEAD_DIM:(h + 1) * HEAD_DIM], kc[g * HEAD_DIM:(g + 1) * HEAD_DIM, :]) + bias_ref[h]
        s = jnp.where(valid, s, NEG)
        e = jnp.exp(s - jnp.max(s, axis=-1, keepdims=True))
        p = e / jnp.sum(e, axis=-1, keepdims=True) * row_valid
        outs.append(_dot_nt(p, vc[g * HEAD_DIM:(g + 1) * HEAD_DIM, :]))
        imp[g] = p if imp[g] is None else imp[g] + p
    ocmp_ref[...] = jnp.concatenate(outs, axis=1)

    jid = lax.broadcasted_iota(jnp.int32, (1, nselp), 1)
    cur = jnp.right_shift(qpos, int(math.log2(SEL_BLOCK)))
    forced = (jid == 0) | (jid == cur) | (jid == cur - 1)
    future = jid * SEL_BLOCK > qpos
    topn = min(SEL_TOPN, n_sel)
    for g in range(NSA_KV):
        score = _dot2(imp[g], ov_ref[...])
        score = jnp.where(forced, FORCE, jnp.where(future, -FORCE, score))
        if nsr:
            st = score.T[0:nsr]
            rid = lax.broadcasted_iota(jnp.int32, (nsr, tq), 0)
            cnt = jnp.zeros((nsr, tq), F32)
            for c in range(n_sel):
                tie = jnp.where(rid > c, 1.0, 0.0)
                row = st[c:c + 1, :]
                cnt = cnt + jnp.where(row > st, 1.0, jnp.where(row == st, tie, 0.0))
            sel = jnp.where(rid < n_sel, jnp.where(cnt < topn, 1.0, 0.0), 0.0)
            sel = jnp.concatenate([sel, jnp.zeros((nselp - nsr, tq), F32)], axis=0).T
            selm_ref[:, g * nselp:(g + 1) * nselp] = sel.astype(BF16)
        else:
            cnt = jnp.zeros((tq, nselp), F32)
            for c in range(n_sel):
                col = score[:, c:c + 1]
                tie = (jid > c).astype(F32)
                cnt = cnt + jnp.where(col > score, 1.0, jnp.where(col == score, tie, 0.0))
            sel = jnp.where(jid < n_sel, jnp.where(cnt < topn, 1.0, 0.0), 0.0)
            selm_ref[:, g * nselp:(g + 1) * nselp] = sel.astype(BF16)


def cmpsel(qn, kcmp, vcmp, biasc, overlap, bsz, t, tq, p0, n_sel, nsr=0):
    nt = t // tq
    nc = kcmp.shape[2]
    nselp = overlap.shape[1]
    assert CMP_STRIDE * (nc - 1) + CMP_LEN - 1 > p0 + t - 1
    assert not nsr or (tq == LANE and nselp == LANE)
    sel_spec = pl.BlockSpec((tq, NSA_KV * nselp), lambda i, b: (b * nt + i, 0))
    sel_shape = jax.ShapeDtypeStruct((bsz * t, NSA_KV * nselp), BF16)
    return pl.pallas_call(
        functools.partial(_cmpsel_kernel, tq=tq, p0=p0, n_sel=n_sel, nselp=nselp, nsr=nsr),
        grid=(nt, bsz),
        in_specs=[pl.BlockSpec((tq, NSA_WIDTH), lambda i, b: (b * nt + i, 0)),
                  pl.BlockSpec((1, KV_W, nc), lambda i, b: (b, 0, 0)),
                  pl.BlockSpec((1, KV_W, nc), lambda i, b: (b, 0, 0)),
                  pl.BlockSpec((NSA_HEADS, tq, nc), lambda i, b: (0, i, 0)),
                  pl.BlockSpec((nc, nselp), lambda i, b: (0, 0))],
        out_specs=[pl.BlockSpec((tq, NSA_WIDTH), lambda i, b: (b * nt + i, 0)), sel_spec],
        out_shape=[jax.ShapeDtypeStruct((bsz * t, NSA_WIDTH), F32), sel_shape],
        compiler_params=_cp("arbitrary", "arbitrary"),
        name="cmp_attn_select",
    )(qn, kcmp, vcmp, biasc, overlap)


def _stack_heads(q, g):
    return jnp.concatenate([q[:, (NSA_REP * g + r) * HEAD_DIM:(NSA_REP * g + r + 1) * HEAD_DIM]
                            for r in range(NSA_REP)], axis=0)


def _unstack_heads(o_groups, tq):
    return jnp.concatenate([o[r * tq:(r + 1) * tq] for o in o_groups for r in range(NSA_REP)], axis=1)


def _online_update(carry, s, v, v_transposed=False):
    m, l, acc = carry
    m_new = jnp.maximum(m, jnp.max(s, axis=-1, keepdims=True))
    a = jnp.exp(m - m_new)
    p = jnp.exp(s - m_new)
    pv = _dot_nt(p, v) if v_transposed else _dot(p, v)
    return m_new, a * l + jnp.sum(p, axis=-1, keepdims=True), a * acc + pv


def _gated_sum(ng, ge, ocmp, osel, owin):
    gx = _dot2(jax.nn.sigmoid(ng), ge)
    return gx[:, 0:512] * ocmp + gx[:, 512:1024] * osel + gx[:, 1024:1536] * owin


def _prompt_attn_kernel(qn_ref, ks_ref, vs_ref, selm_ref, e_ref, bnear_ref, kw_ref, vw_ref, bw_ref,
                        ocmp_ref, ng_ref, ge_ref, oc_ref, madd_ref, *, tq, nselp, nprev, kt):
    i = pl.program_id(1)
    q = qn_ref[...]
    rows = NSA_REP * tq
    n_far = jnp.maximum(i - 1, 0)
    n_macro = n_far // kt
    init = (jnp.full((rows, 1), -jnp.inf, F32), jnp.zeros((rows, 1), F32), jnp.zeros((rows, HEAD_DIM), F32))
    o_sel, o_win = [], []
    for g in range(NSA_KV):
        hs = slice(NSA_REP * g, NSA_REP * (g + 1))
        cs = slice(g * HEAD_DIM, (g + 1) * HEAD_DIM)
        qg = _stack_heads(q, g)
        hit = jnp.dot(selm_ref[:, g * nselp:(g + 1) * nselp], e_ref[...], preferred_element_type=F32)
        madd_ref[g] = (hit - 1.0) * (-NEG)

        def step(r0, width, carry, bias):
            add = madd_ref[g, :, pl.ds(r0, width)][None]
            if bias is not None:
                add = add + bias
            s = _dot_nt(qg, ks_ref[pl.ds(r0, width), cs]).reshape(NSA_REP, tq, width) + add
            return _online_update(carry, s.reshape(rows, width), vs_ref[pl.ds(r0, width), cs])

        carry = lax.fori_loop(0, n_macro, lambda j, c: step(pl.multiple_of(j * kt * tq, kt * tq), kt * tq, c, None), init)
        carry = lax.fori_loop(n_macro * kt, n_far, lambda j, c: step(pl.multiple_of(j * tq, tq), tq, c, None), carry)
        _, l, acc = lax.cond(
            i >= 1,
            lambda c: step(pl.multiple_of((i - 1) * tq, tq), 2 * tq, c, bnear_ref[hs]),
            lambda c: step(0, tq, c, bnear_ref[hs, :, tq:2 * tq]),
            carry)
        o_sel.append(acc / l)

        nk = (nprev + 1) * tq
        r0 = pl.multiple_of(i * tq, tq)
        s = _dot_nt(qg, kw_ref[0, pl.ds(r0, nk), cs]).reshape(NSA_REP, tq, nk) + bw_ref[hs]
        kval = lax.broadcasted_iota(jnp.int32, (1, 1, nk), 2) >= (nprev - i) * tq
        s = jnp.where(kval, s, NEG).reshape(rows, nk)
        _, l, acc = _online_update(init, s, vw_ref[0, pl.ds(r0, nk), cs])
        o_win.append(acc / l)

    oc = _gated_sum(ng_ref[...], ge_ref[...], ocmp_ref[...], _unstack_heads(o_sel, tq), _unstack_heads(o_win, tq))
    oc_ref[...] = oc.astype(BF16)


def _rel_bucket(dist):
    n = jnp.maximum(dist, 0)
    exact = N_BUCKETS // 2
    nf = jnp.maximum(n, 1).astype(F32)
    large = exact + (jnp.log(nf / exact) / math.log(MAX_DIST / exact) * (N_BUCKETS - exact)).astype(jnp.int32)
    return jnp.where(n < exact, n, jnp.minimum(large, N_BUCKETS - 1))


def _bias_last(rel_bias, dist, valid=None):
    onehot = (_rel_bucket(dist)[..., None] == jnp.arange(N_BUCKETS)).astype(F32)
    b = jnp.einsum("...k,kh->...h", onehot, rel_bias.astype(F32), precision=lax.Precision.HIGHEST)
    if valid is not None:
        b = jnp.where(valid[..., None], b, NEG)
    return b


def _bias_table(rel_bias, dist, valid=None):
    return jnp.moveaxis(_bias_last(rel_bias, dist, valid), -1, 0)


def _bias_table_t(rel_bias, dist, valid=None):
    b = jnp.swapaxes(_bias_last(rel_bias, dist, valid), -1, -2)
    return b.reshape(b.shape[:-2] + (b.shape[-2] * b.shape[-1],))


def _gate_expand():
    ge = np.zeros((LANE, 3 * NSA_WIDTH), np.float32)
    for br in range(3):
        for h in range(NSA_HEADS):
            ge[br * NSA_HEADS + h, br * NSA_WIDTH + h * HEAD_DIM: br * NSA_WIDTH + (h + 1) * HEAD_DIM] = 1.0
    return jnp.asarray(ge, BF16)


def _block_expand(nselp, length):
    e = (np.arange(nselp)[:, None] == (np.arange(length) // SEL_BLOCK)[None, :]).astype(np.float32)
    return jnp.asarray(e, BF16)


def prompt_attn_tables(rel_bias, tq):
    nprev = WINDOW // tq
    nk = (nprev + 1) * tq
    ar = jnp.arange
    dn = tq + ar(tq)[:, None] - ar(2 * tq)[None, :]
    dw = ar(tq)[:, None] + nprev * tq - ar(nk)[None, :]
    assert tq + 1 >= MAX_DIST
    far = rel_bias[N_BUCKETS - 1].astype(F32)[:, None, None]
    return {"near": _bias_table(rel_bias, dn, dn >= 0) - far,
            "win": _bias_table(rel_bias, dw, (dw >= 0) & (dw < WINDOW))}


def prompt_attn(qn, rows, selm, ocmp, y, winrows, tabs, bsz, t, tq, kt):
    nt = t // tq
    nselp = selm.shape[1] // NSA_KV
    nprev = WINDOW // tq
    nk = (nprev + 1) * tq
    winp = jnp.pad(winrows.reshape(bsz, t, 2 * KV_W), ((0, 0), (nprev * tq, 0), (0, 0)))
    const = lambda shape: pl.BlockSpec(shape, lambda b, i: tuple(0 for _ in shape))
    tile = lambda w, c: pl.BlockSpec((tq, w), lambda b, i: (b * nt + i, c))
    return pl.pallas_call(
        functools.partial(_prompt_attn_kernel, tq=tq, nselp=nselp, nprev=nprev, kt=kt),
        grid=(bsz, nt),
        in_specs=[tile(NSA_WIDTH, 0),
                  pl.BlockSpec((t, KV_W), lambda b, i: (b, 2)),
                  pl.BlockSpec((t, KV_W), lambda b, i: (b, 3)),
                  tile(NSA_KV * nselp, 0),
                  const((nselp, t)), const((NSA_HEADS, tq, 2 * tq)),
                  pl.BlockSpec((1, t + nprev * tq, KV_W), lambda b, i: (b, 0, 0)),
                  pl.BlockSpec((1, t + nprev * tq, KV_W), lambda b, i: (b, 0, 1)),
                  const((NSA_HEADS, tq, nk)),
                  tile(NSA_WIDTH, 0),
                  tile(LANE, COL_NGATE // LANE),
                  const((LANE, 3 * NSA_WIDTH))],
        out_specs=tile(NSA_WIDTH, 0),
        out_shape=jax.ShapeDtypeStruct((bsz * t, NSA_WIDTH), BF16),
        scratch_shapes=[pltpu.VMEM((NSA_KV, tq, t), F32)],
        compiler_params=_cp("parallel", "arbitrary"),
        name="prompt_sel_win_attn",
    )(qn, rows, rows, selm, _block_expand(nselp, t), tabs["near"], winp, winp, tabs["win"], ocmp, y, _gate_expand())


def _decode_attn_kernel(*refs, t, nselp, masked, pg):
    refs = list(refs[1:] if pg else refs)
    qn_ref = refs.pop(0)
    past = [refs.pop(0) for _ in range(pg if pg else 2)]
    bp_ref, kn_ref, vn_ref, bn_ref = (refs.pop(0) for _ in range(4))
    if masked:
        selm_ref, e_ref = refs.pop(0), refs.pop(0)
    o_ref, m_ref, l_ref, acc_ref = refs
    j = pl.program_id(1)
    rows = NSA_REP * t

    @pl.when(j == 0)
    def _():
        m_ref[...] = jnp.full(m_ref.shape, -jnp.inf, F32)
        l_ref[...] = jnp.zeros(l_ref.shape, F32)
        acc_ref[...] = jnp.zeros(acc_ref.shape, F32)

    q = qn_ref[...]
    tk = bp_ref.shape[2]
    for g in range(NSA_KV):
        hs = slice(NSA_REP * g, NSA_REP * (g + 1))
        cs = slice(g * HEAD_DIM, (g + 1) * HEAD_DIM)
        qg = _stack_heads(q, g)
        if pg:
            kt = jnp.concatenate([p[0, g * HEAD_DIM:(g + 1) * HEAD_DIM, :] for p in past], axis=1)
            vt = jnp.concatenate([p[0, KV_W + g * HEAD_DIM:KV_W + (g + 1) * HEAD_DIM, :] for p in past], axis=1)
            s = _dot(qg, kt)
        else:
            s = _dot_nt(qg, past[0][0][:, cs])
        s = s.reshape(NSA_REP, t, tk) + bp_ref[hs]
        if masked:
            msk = jnp.dot(selm_ref[:, g * nselp:(g + 1) * nselp], e_ref[...], preferred_element_type=F32) > 0.5
            s = jnp.where(msk[None], s, NEG)
        carry = (m_ref[g], l_ref[g], acc_ref[g])
        if pg:
            carry = _online_update(carry, s.reshape(rows, tk), vt, v_transposed=True)
        else:
            carry = _online_update(carry, s.reshape(rows, tk), past[1][0][:, cs])
        m_ref[g], l_ref[g], acc_ref[g] = carry

    @pl.when(j == pl.num_programs(1) - 1)
    def _():
        outs = []
        for g in range(NSA_KV):
            hs = slice(NSA_REP * g, NSA_REP * (g + 1))
            cs = slice(g * HEAD_DIM, (g + 1) * HEAD_DIM)
            qg = _stack_heads(q, g)
            s = _dot_nt(qg, kn_ref[:, cs]).reshape(NSA_REP, t, t) + bn_ref[hs]
            _, l, acc = _online_update((m_ref[g], l_ref[g], acc_ref[g]), s.reshape(rows, t), vn_ref[:, cs])
            outs.append(acc / l)
        o_ref[...] = _unstack_heads(outs, t)


def decode_attn(qn, kv_past, bias_past, new_rows, new_cols, bias_new, bsz, t, tk, selm=None, e=None, paged=None):
    masked = selm is not None
    nselp = selm.shape[1] // NSA_KV if masked else 0
    if paged:
        pt_flat, npages, pg = paged
        assert tk == pg * PAGE_SIZE
        nkt = npages // pg
        ix = lambda f: (lambda b, j, pt: f(b, j))

        def page_spec(k):
            return pl.BlockSpec((1, 2 * KV_W, PAGE_SIZE), lambda b, j, pt: (pt[b * npages + j * pg + k], 1, 0))

        past_specs, past_args = [page_spec(k) for k in range(pg)], [kv_past] * pg
    else:
        pg = 0
        nkt = kv_past.shape[1] // tk
        ix = lambda f: f
        past_specs = [pl.BlockSpec((1, tk, KV_W), lambda b, j: (b, j, 0)),
                      pl.BlockSpec((1, tk, KV_W), lambda b, j: (b, j, 1))]
        past_args = [kv_past, kv_past]
    in_specs = ([pl.BlockSpec((t, NSA_WIDTH), ix(lambda b, j: (b, 0)))] + past_specs +
                [pl.BlockSpec((NSA_HEADS, t, tk), ix(lambda b, j: (0, 0, j))),
                 pl.BlockSpec((t, KV_W), ix(lambda b, j: (b, new_cols[0]))),
                 pl.BlockSpec((t, KV_W), ix(lambda b, j: (b, new_cols[1]))),
                 pl.BlockSpec((NSA_HEADS, t, t), ix(lambda b, j: (0, 0, 0)))])
    args = [qn] + past_args + [bias_past, new_rows, new_rows, bias_new]
    if masked:
        in_specs += [pl.BlockSpec((t, NSA_KV * nselp), ix(lambda b, j: (b, 0))),
                     pl.BlockSpec((nselp, tk), ix(lambda b, j: (0, j)))]
        args += [selm, e]
    rows = NSA_REP * t
    out_spec = pl.BlockSpec((t, NSA_WIDTH), ix(lambda b, j: (b, 0)))
    scratch = [pltpu.VMEM((NSA_KV, rows, 1), F32), pltpu.VMEM((NSA_KV, rows, 1), F32),
               pltpu.VMEM((NSA_KV, rows, HEAD_DIM), F32)]
    body = functools.partial(_decode_attn_kernel, t=t, nselp=nselp, masked=masked, pg=pg)
    common = dict(out_shape=jax.ShapeDtypeStruct((bsz * t, NSA_WIDTH), F32),
                  compiler_params=_cp("parallel", "arbitrary"),
                  name="decode_sel_attn" if masked else "decode_win_attn")
    if paged:
        grid_spec = pltpu.PrefetchScalarGridSpec(num_scalar_prefetch=1, grid=(bsz, nkt), in_specs=in_specs,
                                                 out_specs=out_spec, scratch_shapes=scratch)
        return pl.pallas_call(body, grid_spec=grid_spec, **common)(pt_flat, *args)
    return pl.pallas_call(body, grid=(bsz, nkt), in_specs=in_specs, out_specs=out_spec, scratch_shapes=scratch,
                          **common)(*args)


def _combine_kernel(ng_ref, ge_ref, ocmp_ref, osel_ref, owin_ref, oc_ref):
    oc_ref[...] = _gated_sum(ng_ref[...], ge_ref[...], ocmp_ref[...], osel_ref[...], owin_ref[...]).astype(BF16)


def combine(y, ocmp, osel, owin):
    m = ocmp.shape[0]
    full = pl.BlockSpec((m, NSA_WIDTH), lambda i: (0, 0))
    return pl.pallas_call(
        _combine_kernel,
        grid=(1,),
        in_specs=[pl.BlockSpec((m, LANE), lambda i: (0, COL_NGATE // LANE)),
                  pl.BlockSpec((LANE, 3 * NSA_WIDTH), lambda i: (0, 0)), full, full, full],
        out_specs=full,
        out_shape=jax.ShapeDtypeStruct((m, NSA_WIDTH), BF16),
        compiler_params=_cp("arbitrary"),
        name="nsa_combine",
    )(y, _gate_expand(), ocmp, osel, owin)


def _overlap(nc, n_sel, nselp):
    cs = np.arange(nc) * CMP_STRIDE
    ss = np.arange(nselp) * SEL_BLOCK
    ov = (cs[:, None] < ss[None, :] + SEL_BLOCK) & (cs[:, None] + CMP_LEN > ss[None, :])
    ov &= (np.arange(nc) < nc - 1)[:, None] & (np.arange(nselp) < n_sel)[None, :]
    return jnp.asarray(ov.astype(np.float32), BF16)


def _round_up(x, m):
    return -(-x // m) * m


def position_tables(rel_bias, t, p0, lwin):
    nc = (t if p0 == 0 else p0) // CMP_STRIDE
    ar = jnp.arange
    qpos = p0 + ar(t)
    kend = ar(nc) * CMP_STRIDE + CMP_LEN - 1
    tabs = {"cmp": _bias_table(rel_bias, qpos[:, None] - kend[None, :])}
    if p0 == 0:
        tabs["attn"] = prompt_attn_tables(rel_bias, min(t, 128))
    else:
        dn = ar(t)[:, None] - ar(t)[None, :]
        tabs["new"] = _bias_table(rel_bias, dn, dn >= 0)
        tabs["sel"] = _bias_table(rel_bias, qpos[:, None] - ar(p0)[None, :])
        dw = qpos[:, None] - (p0 - lwin + ar(lwin))[None, :]
        tabs["win"] = _bias_table(rel_bias, dw, (dw >= 0) & (dw < WINDOW))
    return tabs


def layer(x, past, lw, tabs, bsz, t):
    m = bsz * t
    prompt = past is None
    tm = min(512, m)
    y = rms_matmul(x, lw["norm_mix"], lw["w_in"], tm, 1408)

    c = math.gcd(t, HG_CHUNK)
    s0 = jnp.zeros((bsz, HG_HEADS, HG_DIM, HG_DIM), F32) if prompt else past["hgrn"]
    oa, s_hg = hgrn(y, lw["lb"], lw["hg_norm"], s0, bsz, t, c, min(t, 256))

    ts = min(t, 256)
    if prompt:
        h0r = h0i = jnp.zeros((bsz, 1, S5_CH), F32)
    else:
        h0r, h0i = past["s5r"].reshape(bsz, 1, S5_CH), past["s5i"].reshape(bsz, 1, S5_CH)
    ob, s5r, s5i = s5(y, lw["s5"], h0r, h0i, bsz, t, ts)

    qn, rows, winrows, cv = nsa_prep(y, lw["qkn"], tm)
    if prompt:
        p0, nc = 0, t // CMP_STRIDE
        chunks = cv.reshape(m // CMP_STRIDE, CMP_STRIDE * 2 * KV_W)
        cacb = matmul(chunks, lw["cmp_wab"], min(512, chunks.shape[0])).reshape(bsz, nc, 1024)
    else:
        npages = past["npages"]
        pg = min(16, npages)
        p0 = npages * PAGE_SIZE
        nc = p0 // CMP_STRIDE
        cacb = cmp_paged(past["cache_t"], past["pt_flat"], lw["cmp_wab"], bsz, npages, pg)
    kcmp, vcmp = cmp_post(cacb, lw["cmp_bias"], lw["cmp_w2"], lw["qkn"][1], transposed=not prompt)
    n_sel = -(-(p0 + t) // SEL_BLOCK)
    nselp = _round_up(n_sel, LANE)
    tq = min(t, 128)
    nsr = _round_up(n_sel, 8) if prompt else 0
    ocmp, selm = cmpsel(qn, kcmp, vcmp, tabs["cmp"], _overlap(nc, n_sel, nselp), bsz, t, tq, p0, n_sel, nsr)
    if prompt:
        oc = prompt_attn(qn, rows, selm, ocmp, y, winrows, tabs["attn"], bsz, t, tq, 4)
        lw_ = min(WINDOW, t)
        new_win = winrows.reshape(bsz, t, 2 * KV_W)[:, t - lw_:]
    else:
        osel = decode_attn(qn, past["cache_t"], tabs["sel"], rows, (2, 3), tabs["new"], bsz, t, pg * PAGE_SIZE,
                           selm=selm, e=_block_expand(nselp, p0), paged=(past["pt_flat"], npages, pg))
        win = past["win"]
        owin = decode_attn(qn, win, tabs["win"], winrows, (0, 1), tabs["new"], bsz, t, win.shape[1])
        oc = combine(y, ocmp, osel, owin)
        new_win = jnp.concatenate([win, winrows.reshape(bsz, t, 2 * KV_W)], axis=1)[:, t:]

    x1 = merge(x, y, oa, ob, oc, lw["w_branch"], lw["w_out"], tm)
    x2 = ffn(x1, lw["norm_ffn"], lw["w_gate_up"], lw["w_down"], tm, 1408)
    return x2, (rows, new_win, s_hg, s5r, s5i)


def layer_weights(l, ts_list, norm_mix, w_in, lower_bounds, hg_out_norm, s5_a_re, s5_a_im, s5_log_dt, s5_b_re, s5_b_im,
                  s5_c_re, s5_c_im, s5_d, s5_w_glu, nsa_qk_norm, cmp_pe, cmp_w1, cmp_w2, w_branch, w_out, norm_ffn,
                  w_gate_up, w_down):
    wab, cbias, w2blk = cmp_params(cmp_pe[l], cmp_w1[l], cmp_w2[l])
    s5p = {ts: s5_params(s5_a_re[l], s5_a_im[l], s5_log_dt[l], s5_b_re[l], s5_b_im[l], s5_c_re[l], s5_c_im[l],
                         s5_d[l], s5_w_glu[l], ts) for ts in ts_list}
    return {"norm_mix": norm_mix[l].reshape(1, D_MODEL), "w_in": prep_w_in(w_in[l]),
            "lb": lower_bounds[l].reshape(1, HG_WIDTH), "hg_norm": hg_out_norm[l].reshape(1, HG_DIM),
            "s5_by_ts": s5p, "qkn": nsa_qk_norm[l], "cmp_wab": wab, "cmp_bias": cbias, "cmp_w2": w2blk,
            "w_branch": w_branch[l].astype(BF16), "w_out": w_out[l].astype(BF16),
            "norm_ffn": norm_ffn[l].reshape(1, D_MODEL), "w_gate_up": w_gate_up[l].astype(BF16),
            "w_down": w_down[l].astype(BF16)}


def kernel(x_prompt, x_sample, cache_nsa_kv, cache_win_kv, state_hgrn, state_s5_re, state_s5_im, page_table,
           norm_mix, w_in, hg_lb_logits, hg_out_norm, s5_a_re, s5_a_im, s5_log_dt, s5_b_re, s5_b_im,
           s5_c_re, s5_c_im, s5_d, s5_w_glu, nsa_qk_norm, cmp_pe, cmp_w1, cmp_w2, rel_bias,
           w_branch, w_out, norm_ffn, w_gate_up, w_down):
    depth = w_in.shape[0]
    bp, tp, d = x_prompt.shape
    bs, tsm, _ = x_sample.shape
    n_phys = cache_nsa_kv.shape[1]
    npages = page_table.shape[1]
    lb_sm = jax.nn.softmax(hg_lb_logits.astype(F32), axis=0)
    lower_bounds = jnp.cumsum(lb_sm, axis=0) - lb_sm[0]
    ts_p, ts_s = min(tp, 256), min(tsm, 256)
    hp = x_prompt.reshape(bp * tp, d)
    hs = x_sample.reshape(bs * tsm, d)
    cache_t = jnp.transpose(cache_nsa_kv, (0, 1, 3, 4, 5, 2)).reshape(depth * n_phys, 4 * KV_W, PAGE_SIZE)
    tabs_p = position_tables(rel_bias, tp, 0, 0)
    tabs_s = position_tables(rel_bias, tsm, npages * PAGE_SIZE, cache_win_kv.shape[2])
    st_p, st_s = [], []
    for l in range(depth):
        lw = layer_weights(l, sorted({ts_p, ts_s}), norm_mix, w_in, lower_bounds, hg_out_norm, s5_a_re, s5_a_im,
                           s5_log_dt, s5_b_re, s5_b_im, s5_c_re, s5_c_im, s5_d, s5_w_glu, nsa_qk_norm, cmp_pe,
                           cmp_w1, cmp_w2, w_branch, w_out, norm_ffn, w_gate_up, w_down)
        past = {"cache_t": cache_t, "pt_flat": (page_table + l * n_phys).reshape(-1).astype(jnp.int32),
                "npages": npages, "win": cache_win_kv[l].reshape(bs, -1, 2 * KV_W), "hgrn": state_hgrn[l],
                "s5r": state_s5_re[l], "s5i": state_s5_im[l]}
        hp, sp = layer(hp, None, dict(lw, s5=lw["s5_by_ts"][ts_p]), tabs_p, bp, tp)
        hs, ss = layer(hs, past, dict(lw, s5=lw["s5_by_ts"][ts_s]), tabs_s, bs, tsm)
        st_p.append(sp)
        st_s.append(ss)

    def stack(states, k, shape):
        return jnp.stack([s[k].reshape(shape) for s in states])

    kvs = (4, NSA_KV, HEAD_DIM)
    return (hp.reshape(bp, tp, d), hs.reshape(bs, tsm, d),
            stack(st_p, 0, (bp, tp) + kvs), stack(st_s, 0, (bs, tsm) + kvs),
            stack(st_p, 1, (bp, -1, 2, NSA_KV, HEAD_DIM)), stack(st_s, 1, (bs, -1, 2, NSA_KV, HEAD_DIM)),
            stack(st_p, 2, (bp, HG_HEADS, HG_DIM, HG_DIM)), stack(st_s, 2, (bs, HG_HEADS, HG_DIM, HG_DIM)),
            stack(st_p, 3, (bp, S5_GROUPS, S5_STATE)), stack(st_p, 4, (bp, S5_GROUPS, S5_STATE)),
            stack(st_s, 3, (bs, S5_GROUPS, S5_STATE)), stack(st_s, 4, (bs, S5_GROUPS, S5_STATE)))
```

```python
import functools
import math

import numpy as np
import jax
import jax.numpy as jnp
from jax import lax
from jax.experimental import pallas as pl
from jax.experimental.pallas import tpu as pltpu

F32 = jnp.float32
BF16 = jnp.bfloat16

D_MODEL = 1024
HG_HEADS = 4
HG_DIM = 128
HG_WIDTH = HG_HEADS * HG_DIM
HG_CHUNK = 64
LB_FLOOR = 1e-30
S5_GROUP = 16
S5_GROUPS = 32
S5_WIDTH = S5_GROUP * S5_GROUPS
S5_STATE = 64
S5_CH = S5_GROUPS * S5_STATE
NSA_HEADS = 8
NSA_KV = 2
NSA_REP = NSA_HEADS // NSA_KV
HEAD_DIM = 64
NSA_WIDTH = NSA_HEADS * HEAD_DIM
KV_W = NSA_KV * HEAD_DIM
CMP_LEN = 32
CMP_STRIDE = 16
CMP_HIDDEN = 128
SEL_BLOCK = 64
SEL_TOPN = 16
WINDOW = 512
N_BUCKETS = 32
MAX_DIST = 128
PAGE_SIZE = 128
D_FF = 2816
NEG = -1e30
FORCE = 1e9
EPS = 1e-6

COL_GA, COL_GB, COL_GC = 0, 1024, 2048
COL_HQ, COL_HF, COL_HI, COL_HG = 3072, 3584, 4096, 4608
COL_SU, COL_NQ, COL_KV4, COL_WKV, COL_NGATE = 5120, 5632, 6144, 6656, 6912
N_IN_PAD = 7040
N_IN = 6936

LANE = 128
VMEM_LIMIT = 56 * 1024 * 1024


def _cp(*sem):
    return pltpu.CompilerParams(dimension_semantics=sem, vmem_limit_bytes=VMEM_LIMIT)


def _dot(a, b):
    return jnp.dot(a.astype(BF16), b.astype(BF16), preferred_element_type=F32)


def _dot_nt(a, b):
    return lax.dot_general(a.astype(BF16), b.astype(BF16), (((1,), (1,)), ((), ())), preferred_element_type=F32)


def _dot_tn(a, b):
    return lax.dot_general(a.astype(BF16), b.astype(BF16), (((0,), (0,)), ((), ())), preferred_element_type=F32)


def _split2(x):
    hi = x.astype(BF16)
    lo = (x - hi.astype(F32)).astype(BF16)
    return hi, lo


def _dot2(x, w):
    hi, lo = _split2(x)
    return jnp.dot(hi, w, preferred_element_type=F32) + jnp.dot(lo, w, preferred_element_type=F32)


def _silu(x):
    return x * jax.nn.sigmoid(x)


def _gelu_tanh(x):
    return 0.5 * x * (1.0 + jnp.tanh(math.sqrt(2.0 / math.pi) * (x + 0.044715 * (x * x * x))))


def _rms_mm_kernel(x_ref, g_ref, w_ref, o_ref, xn_ref):
    @pl.when(pl.program_id(1) == 0)
    def _():
        x = x_ref[...]
        ms = jnp.mean(x * x, axis=-1, keepdims=True)
        xn_ref[...] = (x * lax.rsqrt(ms + EPS) * g_ref[...]).astype(BF16)

    o_ref[...] = jnp.dot(xn_ref[...], w_ref[...], preferred_element_type=F32)


def rms_matmul(x, g, w, tm, tn):
    m, k = x.shape
    n = w.shape[1]
    return pl.pallas_call(
        _rms_mm_kernel,
        grid=(m // tm, n // tn),
        in_specs=[pl.BlockSpec((tm, k), lambda i, j: (i, 0)),
                  pl.BlockSpec((1, k), lambda i, j: (0, 0)),
                  pl.BlockSpec((k, tn), lambda i, j: (0, j))],
        out_specs=pl.BlockSpec((tm, tn), lambda i, j: (i, j)),
        out_shape=jax.ShapeDtypeStruct((m, n), F32),
        scratch_shapes=[pltpu.VMEM((tm, k), BF16)],
        compiler_params=_cp("parallel", "arbitrary"),
        name="rms_in_proj",
    )(x, g, w)


def _mm_kernel(x_ref, w_ref, o_ref):
    o_ref[...] = jnp.dot(x_ref[...], w_ref[...], preferred_element_type=F32)


def matmul(x, w, tm):
    m, k = x.shape
    n = w.shape[1]
    return pl.pallas_call(
        _mm_kernel,
        grid=(m // tm,),
        in_specs=[pl.BlockSpec((tm, k), lambda i: (i, 0)),
                  pl.BlockSpec((k, n), lambda i: (0, 0))],
        out_specs=pl.BlockSpec((tm, n), lambda i: (i, 0)),
        out_shape=jax.ShapeDtypeStruct((m, n), F32),
        compiler_params=_cp("parallel"),
        name="cmp_matmul",
    )(x, w)


def _merge_kernel(x_ref, ga_ref, gb_ref, gc_ref, oa_ref, ob_ref, oc_ref, wb_ref, wo_ref, o_ref):
    m = jax.nn.sigmoid(ga_ref[...]) * jnp.dot(oa_ref[...], wb_ref[0], preferred_element_type=F32)
    m = m + jax.nn.sigmoid(gb_ref[...]) * jnp.dot(ob_ref[...], wb_ref[1], preferred_element_type=F32)
    m = m + jax.nn.sigmoid(gc_ref[...]) * jnp.dot(oc_ref[...], wb_ref[2], preferred_element_type=F32)
    o_ref[...] = x_ref[...] + jnp.dot(m.astype(BF16), wo_ref[...], preferred_element_type=F32)


def merge(x, y, oa, ob, oc, wb, wo, tm):
    m, d = x.shape
    return pl.pallas_call(
        _merge_kernel,
        grid=(m // tm,),
        in_specs=[pl.BlockSpec((tm, d), lambda i: (i, 0)),
                  pl.BlockSpec((tm, d), lambda i: (i, COL_GA // D_MODEL)),
                  pl.BlockSpec((tm, d), lambda i: (i, COL_GB // D_MODEL)),
                  pl.BlockSpec((tm, d), lambda i: (i, COL_GC // D_MODEL)),
                  pl.BlockSpec((tm, 512), lambda i: (i, 0)),
                  pl.BlockSpec((tm, 512), lambda i: (i, 0)),
                  pl.BlockSpec((tm, 512), lambda i: (i, 0)),
                  pl.BlockSpec((3, 512, d), lambda i: (0, 0, 0)),
                  pl.BlockSpec((d, d), lambda i: (0, 0))],
        out_specs=pl.BlockSpec((tm, d), lambda i: (i, 0)),
        out_shape=jax.ShapeDtypeStruct((m, d), F32),
        compiler_params=_cp("parallel"),
        name="merge_out_proj",
    )(x, y, y, y, oa, ob, oc, wb, wo)


def _ffn_kernel(x_ref, g_ref, wg_ref, wu_ref, wd_ref, o_ref, xn_ref):
    @pl.when(pl.program_id(1) == 0)
    def _():
        x = x_ref[...]
        ms = jnp.mean(x * x, axis=-1, keepdims=True)
        xn_ref[...] = (x * lax.rsqrt(ms + EPS) * g_ref[...]).astype(BF16)
        o_ref[...] = x

    xn = xn_ref[...]
    gate = jnp.dot(xn, wg_ref[...], preferred_element_type=F32)
    up = jnp.dot(xn, wu_ref[...], preferred_element_type=F32)
    h = (_silu(gate) * up).astype(BF16)
    o_ref[...] += jnp.dot(h, wd_ref[...], preferred_element_type=F32)


def ffn(x, g, wgu, wd, tm, tf):
    m, d = x.shape
    nf = D_FF // tf
    return pl.pallas_call(
        _ffn_kernel,
        grid=(m // tm, nf),
        in_specs=[pl.BlockSpec((tm, d), lambda i, f: (i, 0)),
                  pl.BlockSpec((1, d), lambda i, f: (0, 0)),
                  pl.BlockSpec((d, tf), lambda i, f: (0, f)),
                  pl.BlockSpec((d, tf), lambda i, f: (0, nf + f)),
                  pl.BlockSpec((tf, d), lambda i, f: (f, 0))],
        out_specs=pl.BlockSpec((tm, d), lambda i, f: (i, 0)),
        out_shape=jax.ShapeDtypeStruct((m, d), F32),
        scratch_shapes=[pltpu.VMEM((tm, d), BF16)],
        compiler_params=_cp("parallel", "arbitrary"),
        name="swiglu_ffn",
    )(x, g, wgu, wgu, wd)


def _hgrn_levels(c):
    levels = [m for m in (32, 16, 8, 4, 2, 1) if m < c]
    return [m for m in levels if m < 8], [m for m in levels if m >= 8]


def _hgrn_consts(c):
    mats = [np.tril(np.ones((c, c), np.float32))]
    r = np.arange(c)
    for m in _hgrn_levels(c)[0]:
        pos = r % (2 * m)
        mid = (r // (2 * m)) * 2 * m + m
        up = np.zeros((c, c), np.float32)
        lo = np.zeros((c, c), np.float32)
        for t in range(c):
            if pos[t] >= m:
                up[t, mid[t]:t + 1] = 1.0
            else:
                lo[t, t + 1:mid[t]] = 1.0
        mats += [up, lo]
    return np.concatenate(mats, axis=0)


def _hgrn_kernel(hq_ref, hf_ref, hi_ref, hg_ref, lb_ref, gn_ref, s0_ref, mst_ref, o_ref, sout_ref, st_ref,
                 *, c, nchunk):
    tb = pl.program_id(2)

    @pl.when(tb == 0)
    def _():
        st_ref[...] = s0_ref[0, 0].T

    small, big = _hgrn_levels(c)
    rid = lax.broadcasted_iota(jnp.int32, (c, HG_DIM), 0)
    lb = lb_ref[...]
    log_lb = jnp.log(jnp.maximum(lb, LB_FLOOR))
    log_1m = jnp.log1p(-lb)
    one_m = 1.0 - lb
    gn = gn_ref[...]
    mst = mst_ref[...]
    ti = lax.broadcasted_iota(jnp.int32, (c, c), 0)
    si = lax.broadcasted_iota(jnp.int32, (c, c), 1)
    txs = jnp.bitwise_xor(ti, si)
    lower = ti > si

    for ch in range(nchunk):
        rows = pl.ds(ch * c, c)
        fp = hf_ref[rows, :]
        hq = hq_ref[rows, :]
        v = hi_ref[rows, :]
        hg = hg_ref[rows, :]
        log_sig = -(jnp.maximum(-fp, 0.0) + jnp.log1p(jnp.exp(-jnp.abs(fp))))
        b = log_1m + log_sig
        g = jnp.maximum(log_lb, b) + jnp.log1p(jnp.exp(-jnp.abs(log_lb - b)))
        kin = one_m * jax.nn.sigmoid(-fp)
        q = _silu(hq)

        g1 = g.astype(BF16)
        r1 = g - g1.astype(F32)
        g2 = r1.astype(BF16)
        g3 = (r1 - g2.astype(F32)).astype(BF16)
        gs = jnp.concatenate([g1, g2, g3], axis=1)
        rr = jnp.dot(mst, gs, preferred_element_type=F32)
        rr = rr[:, 0:LANE] + rr[:, LANE:2 * LANE] + rr[:, 2 * LANE:3 * LANE]
        gcum = rr[0:c]

        vb = v.astype(BF16)
        att = jnp.where(ti == si, _dot_nt(q, kin), 0.0)
        for li, m in enumerate(small + big):
            if m in small:
                dq = rr[(1 + 2 * li) * c:(2 + 2 * li) * c]
                ek = rr[(2 + 2 * li) * c:(3 + 2 * li) * c]
            else:
                gb = gcum[m - 1:m, :]
                for blk in range(1, c // (2 * m)):
                    gb = jnp.where(rid >= blk * 2 * m, gcum[blk * 2 * m + m - 1:blk * 2 * m + m, :], gb)
                dq = jnp.minimum(gcum - gb, 0.0)
                ek = jnp.minimum(gb - gcum, 0.0)
            pair = _dot_nt(q * jnp.exp(dq), kin * jnp.exp(ek))
            sel = lower & (jnp.right_shift(txs, int(math.log2(m))) == 1)
            att = jnp.where(sel, pair, att)
        st = st_ref[...]
        o = _dot(att, vb) + _dot_nt(q * jnp.exp(gcum), st)
        g_end = gcum[c - 1:c, :]
        kd = kin * jnp.exp(g_end - gcum)
        st_ref[...] = st * jnp.exp(g_end) + _dot_tn(vb, kd)

        ms = jnp.mean(o * o, axis=-1, keepdims=True)
        on = o * lax.rsqrt(ms + EPS) * gn
        o_ref[rows, :] = (on * _silu(hg)).astype(BF16)

    @pl.when(tb == pl.num_programs(2) - 1)
    def _():
        sout_ref[0, 0] = st_ref[...].T


def hgrn(y, lb, gn, s0, bsz, t, c, tbk):
    ntb = t // tbk
    mst = jnp.asarray(_hgrn_consts(c), BF16)

    def col(base):
        return pl.BlockSpec((tbk, HG_DIM), lambda b, h, k: (b * ntb + k, base // HG_DIM + h))

    return pl.pallas_call(
        functools.partial(_hgrn_kernel, c=c, nchunk=tbk // c),
        grid=(bsz, HG_HEADS, ntb),
        in_specs=[col(COL_HQ), col(COL_HF), col(COL_HI), col(COL_HG),
                  pl.BlockSpec((1, HG_DIM), lambda b, h, k: (0, h)),
                  pl.BlockSpec((1, HG_DIM), lambda b, h, k: (0, 0)),
                  pl.BlockSpec((1, 1, HG_DIM, HG_DIM), lambda b, h, k: (b, h, 0, 0)),
                  pl.BlockSpec(mst.shape, lambda b, h, k: (0, 0))],
        out_specs=[pl.BlockSpec((tbk, HG_DIM), lambda b, h, k: (b * ntb + k, h)),
                   pl.BlockSpec((1, 1, HG_DIM, HG_DIM), lambda b, h, k: (b, h, 0, 0))],
        out_shape=[jax.ShapeDtypeStruct((bsz * t, HG_WIDTH), BF16),
                   jax.ShapeDtypeStruct((bsz, HG_HEADS, HG_DIM, HG_DIM), F32)],
        scratch_shapes=[pltpu.VMEM((HG_DIM, HG_DIM), F32)],
        compiler_params=_cp("parallel", "parallel", "arbitrary"),
        name="hgrn2_scan",
    )(y, y, y, y, lb, gn, s0, mst)


S5_SUB = 8


def _shift_in_group(x, d):
    n, w = x.shape
    x3 = x.reshape(n // S5_SUB, S5_SUB, w)
    rid = lax.broadcasted_iota(jnp.int32, x3.shape, 1)
    return jnp.where(rid >= d, pltpu.roll(x3, d, axis=1), 0.0).reshape(n, w)


def _s5_kernel(u_ref, bre_ref, bim_ref, adr_ref, adi_ref, atr_ref, ati_ref, h0r_ref, h0i_ref,
               cre_ref, cim_ref, d_ref, wg_ref, o_ref, hr_out, hi_out, cr_ref, ci_ref, *, ts):
    k = pl.program_id(1)

    @pl.when(k == 0)
    def _():
        cr_ref[...] = h0r_ref[0]
        ci_ref[...] = h0i_ref[0]

    u = u_ref[...]
    ub = u.astype(BF16)
    xr = jnp.dot(ub, bre_ref[...], preferred_element_type=F32)
    xi = jnp.dot(ub, bim_ref[...], preferred_element_type=F32)
    lev = 0
    d = 1
    while d < S5_SUB:
        ar = adr_ref[lev:lev + 1, :]
        ai = adi_ref[lev:lev + 1, :]
        sr = _shift_in_group(xr, d)
        si = _shift_in_group(xi, d)
        xr, xi = xr + ar * sr - ai * si, xi + ar * si + ai * sr
        d *= 2
        lev += 1
    cr = cr_ref[...]
    ci = ci_ref[...]
    atr = atr_ref[...]
    ati = ati_ref[...]
    hrs, his = [], []
    for r in range(ts // S5_SUB):
        gr = xr[r * S5_SUB:(r + 1) * S5_SUB]
        gi = xi[r * S5_SUB:(r + 1) * S5_SUB]
        gr, gi = gr + atr * cr - ati * ci, gi + atr * ci + ati * cr
        hrs.append(gr)
        his.append(gi)
        cr, ci = gr[S5_SUB - 1:S5_SUB], gi[S5_SUB - 1:S5_SUB]
    hr = jnp.concatenate(hrs, axis=0)
    hi = jnp.concatenate(his, axis=0)
    cr_ref[...] = cr
    ci_ref[...] = ci
    y = jnp.dot(hr.astype(BF16), cre_ref[...], preferred_element_type=F32) \
        - jnp.dot(hi.astype(BF16), cim_ref[...], preferred_element_type=F32)
    y = _gelu_tanh(y + d_ref[...] * u)
    o_ref[...] = (y * jax.nn.sigmoid(jnp.dot(y.astype(BF16), wg_ref[...], preferred_element_type=F32))).astype(BF16)

    @pl.when(k == pl.num_programs(1) - 1)
    def _():
        hr_out[0] = cr
        hi_out[0] = ci


def s5(y, prm, h0r, h0i, bsz, t, ts):
    nts = t // ts
    nlev = prm["adr"].shape[0]
    const = lambda shape: pl.BlockSpec(shape, lambda b, k: tuple(0 for _ in shape))
    state = pl.BlockSpec((1, 1, S5_CH), lambda b, k: (b, 0, 0))
    return pl.pallas_call(
        functools.partial(_s5_kernel, ts=ts),
        grid=(bsz, nts),
        in_specs=[pl.BlockSpec((ts, S5_WIDTH), lambda b, k: (b * nts + k, COL_SU // S5_WIDTH)),
                  const((S5_WIDTH, S5_CH)), const((S5_WIDTH, S5_CH)),
                  const((nlev, S5_CH)), const((nlev, S5_CH)),
                  const((S5_SUB, S5_CH)), const((S5_SUB, S5_CH)),
                  state, state,
                  const((S5_CH, S5_WIDTH)), const((S5_CH, S5_WIDTH)),
                  const((1, S5_WIDTH)), const((S5_WIDTH, S5_WIDTH))],
        out_specs=[pl.BlockSpec((ts, S5_WIDTH), lambda b, k: (b * nts + k, 0)), state, state],
        out_shape=[jax.ShapeDtypeStruct((bsz * t, S5_WIDTH), BF16),
                   jax.ShapeDtypeStruct((bsz, 1, S5_CH), F32),
                   jax.ShapeDtypeStruct((bsz, 1, S5_CH), F32)],
        scratch_shapes=[pltpu.VMEM((1, S5_CH), F32), pltpu.VMEM((1, S5_CH), F32)],
        compiler_params=_cp("parallel", "arbitrary"),
        name="s5_scan",
    )(y, prm["bre"], prm["bim"], prm["adr"], prm["adi"], prm["atr"], prm["ati"], h0r, h0i,
      prm["cre"], prm["cim"], prm["d"], prm["wglu"])


def _cmul(ar, ai, br, bi):
    return ar * br - ai * bi, ar * bi + ai * br


def s5_params(a_re, a_im, log_dt, b_re, b_im, c_re, c_im, d, w_glu, ts):
    step = jnp.exp(log_dt)[:, None]
    mag = jnp.exp(a_re * step)
    ab_re, ab_im = mag * jnp.cos(a_im * step), mag * jnp.sin(a_im * step)
    den = a_re * a_re + a_im * a_im
    z_re = ((ab_re - 1.0) * a_re + ab_im * a_im) / den
    z_im = (ab_im * a_re - (ab_re - 1.0) * a_im) / den
    bb_re = z_re[..., None] * b_re - z_im[..., None] * b_im
    bb_im = z_re[..., None] * b_im + z_im[..., None] * b_re
    eye = jnp.eye(S5_GROUPS, dtype=F32)
    bd_in = lambda w: jnp.einsum("gpn,gh->gnhp", w, eye).reshape(S5_WIDTH, S5_CH).astype(BF16)
    bd_out = lambda w: jnp.einsum("gnp,gh->gphn", w, eye).reshape(S5_CH, S5_WIDTH).astype(BF16)
    ar, ai = ab_re.reshape(1, S5_CH), ab_im.reshape(1, S5_CH)
    adr, adi = [ar], [ai]
    d2 = 2
    while d2 < ts:
        r, i = _cmul(adr[-1], adi[-1], adr[-1], adi[-1])
        adr.append(r)
        adi.append(i)
        d2 *= 2
    atr, ati = ar, ai
    n = 1
    lev = 0
    while n < ts:
        pr, pi = _cmul(atr, ati, adr[lev], adi[lev])
        atr, ati = jnp.concatenate([atr, pr], 0), jnp.concatenate([ati, pi], 0)
        n *= 2
        lev += 1
    return {"bre": bd_in(bb_re), "bim": bd_in(bb_im), "cre": bd_out(c_re), "cim": bd_out(c_im),
            "adr": jnp.concatenate(adr, 0), "adi": jnp.concatenate(adi, 0), "atr": atr, "ati": ati,
            "d": d.reshape(1, S5_WIDTH), "wglu": w_glu.astype(BF16)}


def prep_w_in(w):
    pad = jnp.zeros((w.shape[0], N_IN_PAD - N_IN), w.dtype)
    return jnp.concatenate([w[:, 3864:6936], w[:, 0:3840], w[:, 3840:3864], pad], axis=1).astype(BF16)


def _seg_rms(x, ones, w):
    ss = _dot2(x * x, ones)
    return x * lax.rsqrt(ss * (1.0 / HEAD_DIM) + EPS) * w


def _nsa_prep_kernel(nq_ref, kv_ref, wk_ref, wq_ref, wks_ref, wkw_ref, ones_ref, qn_ref, rows_ref, win_ref, cv_ref):
    qn_ref[...] = _seg_rms(nq_ref[...], ones_ref[...], wq_ref[...]).astype(BF16)
    ones1 = ones_ref[0:LANE, 0:LANE]
    kv = kv_ref[...]
    ksn = _seg_rms(kv[:, 2 * KV_W:3 * KV_W], ones1, wks_ref[...])
    rows_ref[...] = jnp.concatenate([kv[:, 0:2 * KV_W], ksn, kv[:, 3 * KV_W:4 * KV_W]], axis=1)
    cv_ref[...] = kv[:, 0:2 * KV_W].astype(BF16)
    wk = wk_ref[...]
    kwn = _seg_rms(wk[:, 0:KV_W], ones1, wkw_ref[...])
    win_ref[...] = jnp.concatenate([kwn, wk[:, KV_W:2 * KV_W]], axis=1)


def _block_ones(n, blk):
    r = np.arange(n) // blk
    return jnp.asarray((r[:, None] == r[None, :]).astype(np.float32), BF16)


def nsa_prep(y, qkn, tm):
    m = y.shape[0]
    wq = (jnp.tile(qkn[0], NSA_HEADS) * HEAD_DIM ** -0.5).reshape(1, NSA_WIDTH)
    wks = jnp.tile(qkn[2], NSA_KV).reshape(1, KV_W)
    wkw = jnp.tile(qkn[3], NSA_KV).reshape(1, KV_W)
    ones = _block_ones(NSA_WIDTH, HEAD_DIM)
    const = lambda shape: pl.BlockSpec(shape, lambda i: (0, 0))
    row = lambda w: pl.BlockSpec((tm, w), lambda i: (i, 0))
    return pl.pallas_call(
        _nsa_prep_kernel,
        grid=(m // tm,),
        in_specs=[pl.BlockSpec((tm, 512), lambda i: (i, COL_NQ // 512)),
                  pl.BlockSpec((tm, 512), lambda i: (i, COL_KV4 // 512)),
                  pl.BlockSpec((tm, 256), lambda i: (i, COL_WKV // 256)),
                  const((1, NSA_WIDTH)), const((1, KV_W)), const((1, KV_W)), const((NSA_WIDTH, NSA_WIDTH))],
        out_specs=[row(512), row(512), row(256), row(256)],
        out_shape=[jax.ShapeDtypeStruct((m, NSA_WIDTH), BF16), jax.ShapeDtypeStruct((m, 4 * KV_W), F32),
                   jax.ShapeDtypeStruct((m, 2 * KV_W), F32), jax.ShapeDtypeStruct((m, 2 * KV_W), BF16)],
        compiler_params=_cp("parallel"),
        name="nsa_prep",
    )(y, y, y, wq, wks, wkw, ones)


def _cmp_post_kernel(c_ref, cb_ref, w2_ref, wk_ref, ones_ref, k_ref, v_ref, *, transposed):
    c = c_ref[0]
    if transposed:
        nc = c.shape[1]
        ca, cb = c[0:512], c[512:1024]
        lid = lax.broadcasted_iota(jnp.int32, cb.shape, 1)
        cb_next = jnp.where(lid < nc - 1, pltpu.roll(cb, nc - 1, axis=1), 0.0)
        hid = _gelu_tanh(ca + cb_next + cb_ref[...])
        out = _dot(w2_ref[...], hid)
        kc = out[0:KV_W]
        hi, lo = _split2(kc * kc)
        ss = jnp.dot(ones_ref[...], hi, preferred_element_type=F32) + jnp.dot(ones_ref[...], lo, preferred_element_type=F32)
        k_ref[0] = (kc * lax.rsqrt(ss * (1.0 / HEAD_DIM) + EPS) * wk_ref[...]).astype(BF16)
        v_ref[0] = out[KV_W:2 * KV_W].astype(BF16)
    else:
        nc = c.shape[0]
        ca, cb = c[:, 0:512], c[:, 512:1024]
        rid = lax.broadcasted_iota(jnp.int32, cb.shape, 0)
        cb_next = jnp.where(rid < nc - 1, pltpu.roll(cb, nc - 1, axis=0), 0.0)
        hid = _gelu_tanh(ca + cb_next + cb_ref[...])
        out = _dot(hid, w2_ref[...])
        k_ref[0] = _seg_rms(out[:, 0:KV_W], ones_ref[...], wk_ref[...]).T.astype(BF16)
        v_ref[0] = out[:, KV_W:2 * KV_W].T.astype(BF16)


def cmp_post(cacb, cbias, w2blk, qkn1, transposed):
    bsz = cacb.shape[0]
    nc = cacb.shape[2] if transposed else cacb.shape[1]
    wk = jnp.tile(qkn1, NSA_KV)
    ones = _block_ones(KV_W, HEAD_DIM)
    if transposed:
        cbias, w2blk, wk = cbias.reshape(512, 1), w2blk.T, wk.reshape(KV_W, 1)
    else:
        wk = wk.reshape(1, KV_W)
    const = lambda a: pl.BlockSpec(a.shape, lambda b: (0, 0))
    out = pl.BlockSpec((1, KV_W, nc), lambda b: (b, 0, 0))
    return pl.pallas_call(
        functools.partial(_cmp_post_kernel, transposed=transposed),
        grid=(bsz,),
        in_specs=[pl.BlockSpec((1,) + cacb.shape[1:], lambda b: (b, 0, 0)),
                  const(cbias), const(w2blk), const(wk), const(ones)],
        out_specs=[out, out],
        out_shape=[jax.ShapeDtypeStruct((bsz, KV_W, nc), BF16)] * 2,
        compiler_params=_cp("parallel"),
        name="cmp_post",
    )(cacb, cbias, w2blk, wk, ones)


def _cmp_paged_kernel(pt_ref, *refs, pg):
    pages = refs[:pg]
    wt_ref, eye_ref, o_ref, tok_ref = refs[pg:]
    nhalf = 2 * KV_W // LANE
    for k in range(pg):
        tok = _dot_nt(eye_ref[...], pages[k][0])
        for c in range(nhalf):
            tok_ref[c, k * PAGE_SIZE:(k + 1) * PAGE_SIZE, :] = tok[:, c * LANE:(c + 1) * LANE]
    nch = pg * PAGE_SIZE // CMP_STRIDE
    acc = jnp.zeros((1024, nch), F32)
    for j in range(CMP_STRIDE):
        rj = jnp.concatenate([tok_ref[c, pl.ds(j, nch, stride=CMP_STRIDE), :] for c in range(nhalf)], axis=1)
        acc = acc + _dot_nt(wt_ref[j], rj)
    o_ref[0] = acc


def cmp_paged(cache_t, pt_flat, wab, bsz, npages, pg):
    nch = pg * PAGE_SIZE // CMP_STRIDE
    nc = npages * PAGE_SIZE // CMP_STRIDE
    wt = wab.reshape(CMP_STRIDE, 2 * KV_W, 1024).transpose(0, 2, 1)
    eye = jnp.eye(PAGE_SIZE, dtype=BF16)

    def page_spec(k):
        return pl.BlockSpec((1, 2 * KV_W, PAGE_SIZE), lambda b, p, pt: (pt[b * npages + p * pg + k], 0, 0))

    return pl.pallas_call(
        functools.partial(_cmp_paged_kernel, pg=pg),
        grid_spec=pltpu.PrefetchScalarGridSpec(
            num_scalar_prefetch=1, grid=(bsz, npages // pg),
            in_specs=[page_spec(k) for k in range(pg)] +
                     [pl.BlockSpec(wt.shape, lambda b, p, pt: (0, 0, 0)), pl.BlockSpec(eye.shape, lambda b, p, pt: (0, 0))],
            out_specs=pl.BlockSpec((1, 1024, nch), lambda b, p, pt: (b, 0, p)),
            scratch_shapes=[pltpu.VMEM((2 * KV_W // LANE, pg * PAGE_SIZE, LANE), F32)]),
        out_shape=jax.ShapeDtypeStruct((bsz, 1024, nc), F32),
        compiler_params=_cp("parallel", "arbitrary"),
        name="cmp_paged",
    )(pt_flat, *([cache_t] * pg), wt, eye)


def cmp_params(pe, w1, w2):
    w1r = w1.reshape(2, 2, 16, HEAD_DIM, CMP_HIDDEN)
    eye2 = jnp.eye(2, dtype=F32)
    wab = jnp.einsum("khjdc,kl,gm->jkgdhlmc", w1r, eye2, eye2).reshape(16 * 256, 1024).astype(BF16)
    cb = jnp.einsum("kf,kfc->kc", pe.reshape(2, CMP_LEN * HEAD_DIM), w1)
    cbias = jnp.broadcast_to(cb[:, None, :], (2, NSA_KV, CMP_HIDDEN)).reshape(1, 512)
    w2blk = jnp.einsum("kcd,kl,gm->kgclmd", w2, eye2, eye2).reshape(512, 256).astype(BF16)
    return wab, cbias, w2blk


def _cmpsel_kernel(qn_ref, kc_ref, vc_ref, bias_ref, ov_ref, ocmp_ref, selm_ref, *, tq, p0, n_sel, nselp, nsr):
    i = pl.program_id(0)
    q = qn_ref[...]
    kc = kc_ref[0]
    vc = vc_ref[0]
    nc = kc.shape[1]
    qpos = p0 + i * tq + lax.broadcasted_iota(jnp.int32, (tq, 1), 0)
    kend = lax.broadcasted_iota(jnp.int32, (1, nc), 1) * CMP_STRIDE + (CMP_LEN - 1)
    valid = qpos >= kend
    row_valid = (qpos >= CMP_LEN - 1).astype(F32)
    imp = [None, None]
    outs = []
    for h in range(NSA_HEADS):
        g = h // NSA_REP
        s = _dot(q[:, h * HEAD_DIM:(h + 1) * HEAD_DIM], kc[g * HEAD_DIM:(g + 1) * HEAD_DIM, :]) + bias_ref[h]
        s = jnp.where(valid, s, NEG)
        e = jnp.exp(s - jnp.max(s, axis=-1, keepdims=True))
        p = e / jnp.sum(e, axis=-1, keepdims=True) * row_valid
        outs.append(_dot_nt(p, vc[g * HEAD_DIM:(g + 1) * HEAD_DIM, :]))
        imp[g] = p if imp[g] is None else imp[g] + p
    ocmp_ref[...] = jnp.concatenate(outs, axis=1)

    jid = lax.broadcasted_iota(jnp.int32, (1, nselp), 1)
    cur = jnp.right_shift(qpos, int(math.log2(SEL_BLOCK)))
    forced = (jid == 0) | (jid == cur) | (jid == cur - 1)
    future = jid * SEL_BLOCK > qpos
    topn = min(SEL_TOPN, n_sel)
    for g in range(NSA_KV):
        score = _dot2(imp[g], ov_ref[...])
        score = jnp.where(forced, FORCE, jnp.where(future, -FORCE, score))
        if nsr:
            st = score.T[0:nsr]
            rid = lax.broadcasted_iota(jnp.int32, (nsr, tq), 0)
            cnt = jnp.zeros((nsr, tq), F32)
            for c in range(n_sel):
                tie = jnp.where(rid > c, 1.0, 0.0)
                row = st[c:c + 1, :]
                cnt = cnt + jnp.where(row > st, 1.0, jnp.where(row == st, tie, 0.0))
            sel = jnp.where(rid < n_sel, jnp.where(cnt < topn, 1.0, 0.0), 0.0)
            sel = jnp.concatenate([sel, jnp.zeros((nselp - nsr, tq), F32)], axis=0).T
            selm_ref[:, g * nselp:(g + 1) * nselp] = sel.astype(BF16)
        else:
            cnt = jnp.zeros((tq, nselp), F32)
            for c in range(n_sel):
                col = score[:, c:c + 1]
                tie = (jid > c).astype(F32)
                cnt = cnt + jnp.where(col > score, 1.0, jnp.where(col == score, tie, 0.0))
            sel = jnp.where(jid < n_sel, jnp.where(cnt < topn, 1.0, 0.0), 0.0)
            selm_ref[:, g * nselp:(g + 1) * nselp] = sel.astype(BF16)


def cmpsel(qn, kcmp, vcmp, biasc, overlap, bsz, t, tq, p0, n_sel, nsr=0):
    nt = t // tq
    nc = kcmp.shape[2]
    nselp = overlap.shape[1]
    assert CMP_STRIDE * (nc - 1) + CMP_LEN - 1 > p0 + t - 1
    assert not nsr or (tq == LANE and nselp == LANE)
    sel_spec = pl.BlockSpec((tq, NSA_KV * nselp), lambda i, b: (b * nt + i, 0))
    sel_shape = jax.ShapeDtypeStruct((bsz * t, NSA_KV * nselp), BF16)
    return pl.pallas_call(
        functools.partial(_cmpsel_kernel, tq=tq, p0=p0, n_sel=n_sel, nselp=nselp, nsr=nsr),
        grid=(nt, bsz),
        in_specs=[pl.BlockSpec((tq, NSA_WIDTH), lambda i, b: (b * nt + i, 0)),
                  pl.BlockSpec((1, KV_W, nc), lambda i, b: (b, 0, 0)),
                  pl.BlockSpec((1, KV_W, nc), lambda i, b: (b, 0, 0)),
                  pl.BlockSpec((NSA_HEADS, tq, nc), lambda i, b: (0, i, 0)),
                  pl.BlockSpec((nc, nselp), lambda i, b: (0, 0))],
        out_specs=[pl.BlockSpec((tq, NSA_WIDTH), lambda i, b: (b * nt + i, 0)), sel_spec],
        out_shape=[jax.ShapeDtypeStruct((bsz * t, NSA_WIDTH), F32), sel_shape],
        compiler_params=_cp("arbitrary", "arbitrary"),
        name="cmp_attn_select",
    )(qn, kcmp, vcmp, biasc, overlap)


def _stack_heads(q, g):
    return jnp.concatenate([q[:, (NSA_REP * g + r) * HEAD_DIM:(NSA_REP * g + r + 1) * HEAD_DIM]
                            for r in range(NSA_REP)], axis=0)


def _unstack_heads(o_groups, tq):
    return jnp.concatenate([o[r * tq:(r + 1) * tq] for o in o_groups for r in range(NSA_REP)], axis=1)


def _online_update(carry, s, v, v_transposed=False):
    m, l, acc = carry
    m_new = jnp.maximum(m, jnp.max(s, axis=-1, keepdims=True))
    a = jnp.exp(m - m_new)
    p = jnp.exp(s - m_new)
    pv = _dot_nt(p, v) if v_transposed else _dot(p, v)
    return m_new, a * l + jnp.sum(p, axis=-1, keepdims=True), a * acc + pv


def _gated_sum(ng, ge, ocmp, osel, owin):
    gx = _dot2(jax.nn.sigmoid(ng), ge)
    return gx[:, 0:512] * ocmp + gx[:, 512:1024] * osel + gx[:, 1024:1536] * owin


def _prompt_attn_kernel(qn_ref, ks_ref, vs_ref, selm_ref, e_ref, bnear_ref, kw_ref, vw_ref, bw_ref,
                        ocmp_ref, ng_ref, ge_ref, oc_ref, madd_ref, *, tq, nselp, nprev, kt):
    i = pl.program_id(1)
    q = qn_ref[...]
    rows = NSA_REP * tq
    n_far = jnp.maximum(i - 1, 0)
    n_macro = n_far // kt
    init1 = (jnp.full((rows, 1), -jnp.inf, F32), jnp.zeros((rows, 1), F32), jnp.zeros((rows, HEAD_DIM), F32))
    init = (init1,) * NSA_KV
    heads = [slice(NSA_REP * g, NSA_REP * (g + 1)) for g in range(NSA_KV)]
    cols = [slice(g * HEAD_DIM, (g + 1) * HEAD_DIM) for g in range(NSA_KV)]
    qgs = [_stack_heads(q, g) for g in range(NSA_KV)]
    for g in range(NSA_KV):
        hit = jnp.dot(selm_ref[:, g * nselp:(g + 1) * nselp], e_ref[...], preferred_element_type=F32)
        madd_ref[g] = (hit - 1.0) * (-NEG)

    def step(r0, width, carries, near):
        out = []
        for g in range(NSA_KV):
            add = madd_ref[g, :, pl.ds(r0, width)][None]
            if near is not None:
                add = add + bnear_ref[heads[g], :, near]
            s = _dot_nt(qgs[g], ks_ref[pl.ds(r0, width), cols[g]]).reshape(NSA_REP, tq, width) + add
            out.append(_online_update(carries[g], s.reshape(rows, width), vs_ref[pl.ds(r0, width), cols[g]]))
        return tuple(out)

    def far(j, ntiles, c):
        return step(pl.multiple_of(j * tq, tq), ntiles * tq, c, None)

    carries = lax.fori_loop(0, n_macro, lambda j, c: far(j * kt, kt, c), init)
    done = n_macro * kt
    w = kt // 2
    while w >= 1:
        take = jnp.bitwise_and(n_far - done, w) != 0
        carries = lax.cond(take, functools.partial(far, done, w), lambda c: c, carries)
        done = done + jnp.where(take, w, 0)
        w //= 2
    carries = lax.cond(
        i >= 1,
        lambda c: step(pl.multiple_of((i - 1) * tq, tq), 2 * tq, c, slice(0, 2 * tq)),
        lambda c: step(0, tq, c, slice(tq, 2 * tq)),
        carries)
    o_sel = [acc / l for _, l, acc in carries]

    o_win = []
    nk = (nprev + 1) * tq
    r0 = pl.multiple_of(i * tq, tq)
    kval = lax.broadcasted_iota(jnp.int32, (1, 1, nk), 2) >= (nprev - i) * tq
    for g in range(NSA_KV):
        s = _dot_nt(qgs[g], kw_ref[0, pl.ds(r0, nk), cols[g]]).reshape(NSA_REP, tq, nk) + bw_ref[heads[g]]
        s = jnp.where(kval, s, NEG).reshape(rows, nk)
        _, l, acc = _online_update(init1, s, vw_ref[0, pl.ds(r0, nk), cols[g]])
        o_win.append(acc / l)

    oc = _gated_sum(ng_ref[...], ge_ref[...], ocmp_ref[...], _unstack_heads(o_sel, tq), _unstack_heads(o_win, tq))
    oc_ref[...] = oc.astype(BF16)


def _rel_bucket(dist):
    n = jnp.maximum(dist, 0)
    exact = N_BUCKETS // 2
    nf = jnp.maximum(n, 1).astype(F32)
    large = exact + (jnp.log(nf / exact) / math.log(MAX_DIST / exact) * (N_BUCKETS - exact)).astype(jnp.int32)
    return jnp.where(n < exact, n, jnp.minimum(large, N_BUCKETS - 1))


def _bias_last(rel_bias, dist, valid=None):
    onehot = (_rel_bucket(dist)[..., None] == jnp.arange(N_BUCKETS)).astype(F32)
    b = jnp.einsum("...k,kh->...h", onehot, rel_bias.astype(F32), precision=lax.Precision.HIGHEST)
    if valid is not None:
        b = jnp.where(valid[..., None], b, NEG)
    return b


def _bias_table(rel_bias, dist, valid=None):
    return jnp.moveaxis(_bias_last(rel_bias, dist, valid), -1, 0)


def _bias_table_t(rel_bias, dist, valid=None):
    b = jnp.swapaxes(_bias_last(rel_bias, dist, valid), -1, -2)
    return b.reshape(b.shape[:-2] + (b.shape[-2] * b.shape[-1],))


def _gate_expand():
    ge = np.zeros((LANE, 3 * NSA_WIDTH), np.float32)
    for br in range(3):
        for h in range(NSA_HEADS):
            ge[br * NSA_HEADS + h, br * NSA_WIDTH + h * HEAD_DIM: br * NSA_WIDTH + (h + 1) * HEAD_DIM] = 1.0
    return jnp.asarray(ge, BF16)


def _block_expand(nselp, length):
    e = (np.arange(nselp)[:, None] == (np.arange(length) // SEL_BLOCK)[None, :]).astype(np.float32)
    return jnp.asarray(e, BF16)


def prompt_attn_tables(rel_bias, tq):
    nprev = WINDOW // tq
    nk = (nprev + 1) * tq
    ar = jnp.arange
    dn = tq + ar(tq)[:, None] - ar(2 * tq)[None, :]
    dw = ar(tq)[:, None] + nprev * tq - ar(nk)[None, :]
    assert tq + 1 >= MAX_DIST
    far = rel_bias[N_BUCKETS - 1].astype(F32)[:, None, None]
    return {"near": _bias_table(rel_bias, dn, dn >= 0) - far,
            "win": _bias_table(rel_bias, dw, (dw >= 0) & (dw < WINDOW))}


def prompt_attn(qn, rows, selm, ocmp, y, winrows, tabs, bsz, t, tq, kt):
    nt = t // tq
    nselp = selm.shape[1] // NSA_KV
    nprev = WINDOW // tq
    nk = (nprev + 1) * tq
    winp = jnp.pad(winrows.reshape(bsz, t, 2 * KV_W), ((0, 0), (nprev * tq, 0), (0, 0)))
    const = lambda shape: pl.BlockSpec(shape, lambda b, i: tuple(0 for _ in shape))
    tile = lambda w, c: pl.BlockSpec((tq, w), lambda b, i: (b * nt + i, c))
    return pl.pallas_call(
        functools.partial(_prompt_attn_kernel, tq=tq, nselp=nselp, nprev=nprev, kt=kt),
        grid=(bsz, nt),
        in_specs=[tile(NSA_WIDTH, 0),
                  pl.BlockSpec((t, KV_W), lambda b, i: (b, 2)),
                  pl.BlockSpec((t, KV_W), lambda b, i: (b, 3)),
                  tile(NSA_KV * nselp, 0),
                  const((nselp, t)), const((NSA_HEADS, tq, 2 * tq)),
                  pl.BlockSpec((1, t + nprev * tq, KV_W), lambda b, i: (b, 0, 0)),
                  pl.BlockSpec((1, t + nprev * tq, KV_W), lambda b, i: (b, 0, 1)),
                  const((NSA_HEADS, tq, nk)),
                  tile(NSA_WIDTH, 0),
                  tile(LANE, COL_NGATE // LANE),
                  const((LANE, 3 * NSA_WIDTH))],
        out_specs=tile(NSA_WIDTH, 0),
        out_shape=jax.ShapeDtypeStruct((bsz * t, NSA_WIDTH), BF16),
        scratch_shapes=[pltpu.VMEM((NSA_KV, tq, t), F32)],
        compiler_params=_cp("parallel", "arbitrary"),
        name="prompt_sel_win_attn",
    )(qn, rows, rows, selm, _block_expand(nselp, t), tabs["near"], winp, winp, tabs["win"], ocmp, y, _gate_expand())


def _decode_attn_kernel(*refs, t, nselp, masked, pg):
    refs = list(refs[1:] if pg else refs)
    qn_ref = refs.pop(0)
    past = [refs.pop(0) for _ in range(pg if pg else 2)]
    bp_ref, kn_ref, vn_ref, bn_ref = (refs.pop(0) for _ in range(4))
    if masked:
        selm_ref, e_ref = refs.pop(0), refs.pop(0)
    o_ref, m_ref, l_ref, acc_ref = refs
    if pg:
        j, bi, last = pl.program_id(0), pl.program_id(1), pl.num_programs(0) - 1
    else:
        j, bi, last = pl.program_id(1), 0, pl.num_programs(1) - 1
    rows = NSA_REP * t

    @pl.when(j == 0)
    def _():
        m_ref[bi] = jnp.full(m_ref.shape[1:], -jnp.inf, F32)
        l_ref[bi] = jnp.zeros(l_ref.shape[1:], F32)
        acc_ref[bi] = jnp.zeros(acc_ref.shape[1:], F32)

    q = qn_ref[...]
    tk = bp_ref.shape[2]
    state = [(m_ref[bi, g], l_ref[bi, g], acc_ref[bi, g]) for g in range(NSA_KV)]
    for g in range(NSA_KV):
        hs = slice(NSA_REP * g, NSA_REP * (g + 1))
        cs = slice(g * HEAD_DIM, (g + 1) * HEAD_DIM)
        qg = _stack_heads(q, g)
        if pg:
            kt = jnp.concatenate([p[0, g * HEAD_DIM:(g + 1) * HEAD_DIM, :] for p in past], axis=1)
            vt = jnp.concatenate([p[0, KV_W + g * HEAD_DIM:KV_W + (g + 1) * HEAD_DIM, :] for p in past], axis=1)
            s = _dot(qg, kt)
        else:
            s = _dot_nt(qg, past[0][0][:, cs])
        s = s.reshape(NSA_REP, t, tk) + bp_ref[hs]
        if masked:
            msk = jnp.dot(selm_ref[0, 0, g], e_ref[...], preferred_element_type=F32) > 0.5
            s = jnp.where(msk[None], s, NEG)
        if pg:
            state[g] = _online_update(state[g], s.reshape(rows, tk), vt, v_transposed=True)
        else:
            state[g] = _online_update(state[g], s.reshape(rows, tk), past[1][0][:, cs])
    for g in range(NSA_KV):
        m_ref[bi, g], l_ref[bi, g], acc_ref[bi, g] = state[g]

    @pl.when(j < last)
    def _():
        o_ref[0] = jnp.zeros(o_ref.shape[1:], F32)

    @pl.when(j == last)
    def _():
        outs = []
        for g in range(NSA_KV):
            hs = slice(NSA_REP * g, NSA_REP * (g + 1))
            cs = slice(g * HEAD_DIM, (g + 1) * HEAD_DIM)
            qg = _stack_heads(q, g)
            s = _dot_nt(qg, kn_ref[:, cs]).reshape(NSA_REP, t, t) + bn_ref[hs]
            _, l, acc = _online_update((m_ref[bi, g], l_ref[bi, g], acc_ref[bi, g]), s.reshape(rows, t), vn_ref[:, cs])
            outs.append(acc / l)
        o_ref[0] = _unstack_heads(outs, t)


def decode_attn(qn, kv_past, bias_past, new_rows, new_cols, bias_new, bsz, t, tk, selm=None, paged=None):
    masked = selm is not None
    assert not masked or paged
    nselp = selm.shape[1] // NSA_KV if masked else 0
    if paged:
        pt_flat, npages, pg = paged
        assert tk == pg * PAGE_SIZE
        nkt = npages // pg
        ix = lambda f: (lambda j, b, pt: f(b, j))

        def page_spec(k):
            return pl.BlockSpec((1, 2 * KV_W, PAGE_SIZE), lambda j, b, pt: (pt[b * npages + j * pg + k], 1, 0))

        past_specs, past_args = [page_spec(k) for k in range(pg)], [kv_past] * pg
    else:
        pg = 0
        nkt = kv_past.shape[1] // tk
        ix = lambda f: f
        past_specs = [pl.BlockSpec((1, tk, KV_W), lambda b, j: (b, j, 0)),
                      pl.BlockSpec((1, tk, KV_W), lambda b, j: (b, j, 1))]
        past_args = [kv_past, kv_past]
    in_specs = ([pl.BlockSpec((t, NSA_WIDTH), ix(lambda b, j: (b, 0)))] + past_specs +
                [pl.BlockSpec((NSA_HEADS, t, tk), ix(lambda b, j: (0, 0, j))),
                 pl.BlockSpec((t, KV_W), ix(lambda b, j: (b, new_cols[0]))),
                 pl.BlockSpec((t, KV_W), ix(lambda b, j: (b, new_cols[1]))),
                 pl.BlockSpec((NSA_HEADS, t, t), ix(lambda b, j: (0, 0, 0)))])
    args = [qn] + past_args + [bias_past, new_rows, new_rows, bias_new]
    if masked:
        nbt = tk // SEL_BLOCK
        sel_t = jnp.stack([selm[:, g * nselp:g * nselp + nkt * nbt].reshape(bsz, t, nkt, nbt) for g in range(NSA_KV)])
        sel_t = sel_t.transpose(1, 3, 0, 2, 4)
        in_specs += [pl.BlockSpec((1, 1, NSA_KV, t, nbt), ix(lambda b, j: (b, j, 0, 0, 0))),
                     pl.BlockSpec((nbt, tk), ix(lambda b, j: (0, 0)))]
        args += [sel_t, _block_expand(nbt, tk)]
    rows = NSA_REP * t
    nslab = nkt if paged else 1
    out_spec = pl.BlockSpec((1, t, NSA_WIDTH), ix(lambda b, j: (j if paged else 0, b, 0)))
    nst = bsz if paged else 1
    scratch = [pltpu.VMEM((nst, NSA_KV, rows, 1), F32), pltpu.VMEM((nst, NSA_KV, rows, 1), F32),
               pltpu.VMEM((nst, NSA_KV, rows, HEAD_DIM), F32)]
    body = functools.partial(_decode_attn_kernel, t=t, nselp=nselp, masked=masked, pg=pg)
    common = dict(out_shape=jax.ShapeDtypeStruct((nslab, bsz * t, NSA_WIDTH), F32),
                  name="decode_sel_attn" if masked else "decode_win_attn")
    if paged:
        grid_spec = pltpu.PrefetchScalarGridSpec(num_scalar_prefetch=1, grid=(nkt, bsz), in_specs=in_specs,
                                                 out_specs=out_spec, scratch_shapes=scratch)
        out = pl.pallas_call(body, grid_spec=grid_spec, compiler_params=_cp("arbitrary", "arbitrary"),
                             **common)(pt_flat, *args)
    else:
        out = pl.pallas_call(body, grid=(bsz, nkt), in_specs=in_specs, out_specs=out_spec, scratch_shapes=scratch,
                             compiler_params=_cp("parallel", "arbitrary"), **common)(*args)
    return out[nslab - 1]


def _combine_kernel(ng_ref, ge_ref, ocmp_ref, osel_ref, owin_ref, oc_ref):
    oc_ref[...] = _gated_sum(ng_ref[...], ge_ref[...], ocmp_ref[...], osel_ref[...], owin_ref[...]).astype(BF16)


def combine(y, ocmp, osel, owin):
    m = ocmp.shape[0]
    full = pl.BlockSpec((m, NSA_WIDTH), lambda i: (0, 0))
    return pl.pallas_call(
        _combine_kernel,
        grid=(1,),
        in_specs=[pl.BlockSpec((m, LANE), lambda i: (0, COL_NGATE // LANE)),
                  pl.BlockSpec((LANE, 3 * NSA_WIDTH), lambda i: (0, 0)), full, full, full],
        out_specs=full,
        out_shape=jax.ShapeDtypeStruct((m, NSA_WIDTH), BF16),
        compiler_params=_cp("arbitrary"),
        name="nsa_combine",
    )(y, _gate_expand(), ocmp, osel, owin)


def _overlap(nc, n_sel, nselp):
    cs = np.arange(nc) * CMP_STRIDE
    ss = np.arange(nselp) * SEL_BLOCK
    ov = (cs[:, None] < ss[None, :] + SEL_BLOCK) & (cs[:, None] + CMP_LEN > ss[None, :])
    ov &= (np.arange(nc) < nc - 1)[:, None] & (np.arange(nselp) < n_sel)[None, :]
    return jnp.asarray(ov.astype(np.float32), BF16)


def _round_up(x, m):
    return -(-x // m) * m


def position_tables(rel_bias, t, p0, lwin):
    nc = (t if p0 == 0 else p0) // CMP_STRIDE
    ar = jnp.arange
    qpos = p0 + ar(t)
    kend = ar(nc) * CMP_STRIDE + CMP_LEN - 1
    tabs = {"cmp": _bias_table(rel_bias, qpos[:, None] - kend[None, :])}
    if p0 == 0:
        tabs["attn"] = prompt_attn_tables(rel_bias, min(t, 128))
    else:
        dn = ar(t)[:, None] - ar(t)[None, :]
        tabs["new"] = _bias_table(rel_bias, dn, dn >= 0)
        tabs["sel"] = _bias_table(rel_bias, qpos[:, None] - ar(p0)[None, :])
        dw = qpos[:, None] - (p0 - lwin + ar(lwin))[None, :]
        tabs["win"] = _bias_table(rel_bias, dw, (dw >= 0) & (dw < WINDOW))
    return tabs


def layer(x, past, lw, tabs, bsz, t):
    m = bsz * t
    prompt = past is None
    tm = min(512, m)
    y = rms_matmul(x, lw["norm_mix"], lw["w_in"], min(1024, m), 1408)

    c = math.gcd(t, HG_CHUNK)
    s0 = jnp.zeros((bsz, HG_HEADS, HG_DIM, HG_DIM), F32) if prompt else past["hgrn"]
    oa, s_hg = hgrn(y, lw["lb"], lw["hg_norm"], s0, bsz, t, c, min(t, 256))

    ts = min(t, 256)
    if prompt:
        h0r = h0i = jnp.zeros((bsz, 1, S5_CH), F32)
    else:
        h0r, h0i = past["s5r"].reshape(bsz, 1, S5_CH), past["s5i"].reshape(bsz, 1, S5_CH)
    ob, s5r, s5i = s5(y, lw["s5"], h0r, h0i, bsz, t, ts)

    qn, rows, winrows, cv = nsa_prep(y, lw["qkn"], tm)
    if prompt:
        p0, nc = 0, t // CMP_STRIDE
        chunks = cv.reshape(m // CMP_STRIDE, CMP_STRIDE * 2 * KV_W)
        cacb = matmul(chunks, lw["cmp_wab"], min(512, chunks.shape[0])).reshape(bsz, nc, 1024)
    else:
        npages = past["npages"]
        pg = min(16, npages)
        p0 = npages * PAGE_SIZE
        nc = p0 // CMP_STRIDE
        cacb = cmp_paged(past["cache_t"], past["pt_flat"], lw["cmp_wab"], bsz, npages, min(32, npages))
    kcmp, vcmp = cmp_post(cacb, lw["cmp_bias"], lw["cmp_w2"], lw["qkn"][1], transposed=not prompt)
    n_sel = -(-(p0 + t) // SEL_BLOCK)
    nselp = _round_up(n_sel, LANE)
    tq = min(t, 128)
    nsr = _round_up(n_sel, 8) if prompt else 0
    ocmp, selm = cmpsel(qn, kcmp, vcmp, tabs["cmp"], _overlap(nc, n_sel, nselp), bsz, t, tq, p0, n_sel, nsr)
    if prompt:
        oc = prompt_attn(qn, rows, selm, ocmp, y, winrows, tabs["attn"], bsz, t, tq, 4)
        lw_ = min(WINDOW, t)
        new_win = winrows.reshape(bsz, t, 2 * KV_W)[:, t - lw_:]
    else:
        osel = decode_attn(qn, past["cache_t"], tabs["sel"], rows, (2, 3), tabs["new"], bsz, t, pg * PAGE_SIZE,
                           selm=selm, paged=(past["pt_flat"], npages, pg))
        win = past["win"]
        owin = decode_attn(qn, win, tabs["win"], winrows, (0, 1), tabs["new"], bsz, t, win.shape[1])
        oc = combine(y, ocmp, osel, owin)
        new_win = jnp.concatenate([win, winrows.reshape(bsz, t, 2 * KV_W)], axis=1)[:, t:]

    x1 = merge(x, y, oa, ob, oc, lw["w_branch"], lw["w_out"], tm)
    x2 = ffn(x1, lw["norm_ffn"], lw["w_gate_up"], lw["w_down"], tm, 1408)
    return x2, (rows, new_win, s_hg, s5r, s5i)


def layer_weights(l, norm_mix, w_in, lower_bounds, hg_out_norm, s5_a_re, s5_a_im, s5_log_dt, s5_b_re, s5_b_im,
                  s5_c_re, s5_c_im, s5_d, s5_w_glu, nsa_qk_norm, cmp_pe, cmp_w1, cmp_w2, w_branch, w_out, norm_ffn,
                  w_gate_up, w_down):
    wab, cbias, w2blk = cmp_params(cmp_pe[l], cmp_w1[l], cmp_w2[l])
    s5p = s5_params(s5_a_re[l], s5_a_im[l], s5_log_dt[l], s5_b_re[l], s5_b_im[l], s5_c_re[l], s5_c_im[l],
                    s5_d[l], s5_w_glu[l], S5_SUB)
    return {"norm_mix": norm_mix[l].reshape(1, D_MODEL), "w_in": prep_w_in(w_in[l]),
            "lb": lower_bounds[l].reshape(1, HG_WIDTH), "hg_norm": hg_out_norm[l].reshape(1, HG_DIM),
            "s5": s5p, "qkn": nsa_qk_norm[l], "cmp_wab": wab, "cmp_bias": cbias, "cmp_w2": w2blk,
            "w_branch": w_branch[l].astype(BF16), "w_out": w_out[l].astype(BF16),
            "norm_ffn": norm_ffn[l].reshape(1, D_MODEL), "w_gate_up": w_gate_up[l].astype(BF16),
            "w_down": w_down[l].astype(BF16)}


def kernel(x_prompt, x_sample, cache_nsa_kv, cache_win_kv, state_hgrn, state_s5_re, state_s5_im, page_table,
           norm_mix, w_in, hg_lb_logits, hg_out_norm, s5_a_re, s5_a_im, s5_log_dt, s5_b_re, s5_b_im,
           s5_c_re, s5_c_im, s5_d, s5_w_glu, nsa_qk_norm, cmp_pe, cmp_w1, cmp_w2, rel_bias,
           w_branch, w_out, norm_ffn, w_gate_up, w_down):
    depth = w_in.shape[0]
    bp, tp, d = x_prompt.shape
    bs, tsm, _ = x_sample.shape
    n_phys = cache_nsa_kv.shape[1]
    npages = page_table.shape[1]
    lb_sm = jax.nn.softmax(hg_lb_logits.astype(F32), axis=0)
    lower_bounds = jnp.cumsum(lb_sm, axis=0) - lb_sm[0]
    hp = x_prompt.reshape(bp * tp, d)
    hs = x_sample.reshape(bs * tsm, d)
    cache_t = jnp.transpose(cache_nsa_kv, (0, 1, 3, 4, 5, 2)).reshape(depth * n_phys, 4 * KV_W, PAGE_SIZE)
    tabs_p = position_tables(rel_bias, tp, 0, 0)
    tabs_s = position_tables(rel_bias, tsm, npages * PAGE_SIZE, cache_win_kv.shape[2])
    st_p, st_s = [], []
    for l in range(depth):
        lw = layer_weights(l, norm_mix, w_in, lower_bounds, hg_out_norm, s5_a_re, s5_a_im,
                           s5_log_dt, s5_b_re, s5_b_im, s5_c_re, s5_c_im, s5_d, s5_w_glu, nsa_qk_norm, cmp_pe,
                           cmp_w1, cmp_w2, w_branch, w_out, norm_ffn, w_gate_up, w_down)
        past = {"cache_t": cache_t, "pt_flat": (page_table + l * n_phys).reshape(-1).astype(jnp.int32),
                "npages": npages, "win": cache_win_kv[l].reshape(bs, -1, 2 * KV_W), "hgrn": state_hgrn[l],
                "s5r": state_s5_re[l], "s5i": state_s5_im[l]}
        hp, sp = layer(hp, None, lw, tabs_p, bp, tp)
        hs, ss = layer(hs, past, lw, tabs_s, bs, tsm)
        st_p.append(sp)
        st_s.append(ss)

    def stack(states, k, shape):
        return jnp.stack([s[k].reshape(shape) for s in states])

    kvs = (4, NSA_KV, HEAD_DIM)
    return (hp.reshape(bp, tp, d), hs.reshape(bs, tsm, d),
            stack(st_p, 0, (bp, tp) + kvs), stack(st_s, 0, (bs, tsm) + kvs),
            stack(st_p, 1, (bp, -1, 2, NSA_KV, HEAD_DIM)), stack(st_s, 1, (bs, -1, 2, NSA_KV, HEAD_DIM)),
            stack(st_p, 2, (bp, HG_HEADS, HG_DIM, HG_DIM)), stack(st_s, 2, (bs, HG_HEADS, HG_DIM, HG_DIM)),
            stack(st_p, 3, (bp, S5_GROUPS, S5_STATE)), stack(st_p, 4, (bp, S5_GROUPS, S5_STATE)),
            stack(st_s, 3, (bs, S5_GROUPS, S5_STATE)), stack(st_s, 4, (bs, S5_GROUPS, S5_STATE)))
```

```python
import functools
import math

import numpy as np
import jax
import jax.numpy as jnp
from jax import lax
from jax.experimental import pallas as pl
from jax.experimental.pallas import tpu as pltpu

F32 = jnp.float32
BF16 = jnp.bfloat16

D_MODEL = 1024
HG_HEADS = 4
HG_DIM = 128
HG_WIDTH = HG_HEADS * HG_DIM
HG_CHUNK = 64
LB_FLOOR = 1e-30
S5_GROUP = 16
S5_GROUPS = 32
S5_WIDTH = S5_GROUP * S5_GROUPS
S5_STATE = 64
S5_CH = S5_GROUPS * S5_STATE
NSA_HEADS = 8
NSA_KV = 2
NSA_REP = NSA_HEADS // NSA_KV
HEAD_DIM = 64
NSA_WIDTH = NSA_HEADS * HEAD_DIM
KV_W = NSA_KV * HEAD_DIM
CMP_LEN = 32
CMP_STRIDE = 16
CMP_HIDDEN = 128
SEL_BLOCK = 64
SEL_TOPN = 16
WINDOW = 512
N_BUCKETS = 32
MAX_DIST = 128
PAGE_SIZE = 128
D_FF = 2816
NEG = -1e30
FORCE = 1e9
EPS = 1e-6

COL_GA, COL_GB, COL_GC = 0, 1024, 2048
COL_HQ, COL_HF, COL_HI, COL_HG = 3072, 3584, 4096, 4608
COL_SU, COL_NQ, COL_KV4, COL_WKV, COL_NGATE = 5120, 5632, 6144, 6656, 6912
N_IN_PAD = 7040
N_IN = 6936

LANE = 128
VMEM_LIMIT = 56 * 1024 * 1024


def _cp(*sem):
    return pltpu.CompilerParams(dimension_semantics=sem, vmem_limit_bytes=VMEM_LIMIT)


def _dot(a, b):
    return jnp.dot(a.astype(BF16), b.astype(BF16), preferred_element_type=F32)


def _dot_nt(a, b):
    return lax.dot_general(a.astype(BF16), b.astype(BF16), (((1,), (1,)), ((), ())), preferred_element_type=F32)


def _dot_tn(a, b):
    return lax.dot_general(a.astype(BF16), b.astype(BF16), (((0,), (0,)), ((), ())), preferred_element_type=F32)


def _split2(x):
    hi = x.astype(BF16)
    lo = (x - hi.astype(F32)).astype(BF16)
    return hi, lo


def _dot2(x, w):
    hi, lo = _split2(x)
    return jnp.dot(hi, w, preferred_element_type=F32) + jnp.dot(lo, w, preferred_element_type=F32)


def _silu(x):
    return x * jax.nn.sigmoid(x)


def _gelu_tanh(x):
    return 0.5 * x * (1.0 + jnp.tanh(math.sqrt(2.0 / math.pi) * (x + 0.044715 * (x * x * x))))


def _rms_mm_kernel(x_ref, g_ref, w_ref, o_ref, xn_ref):
    @pl.when(pl.program_id(1) == 0)
    def _():
        x = x_ref[...]
        ms = jnp.mean(x * x, axis=-1, keepdims=True)
        xn_ref[...] = (x * lax.rsqrt(ms + EPS) * g_ref[...]).astype(BF16)

    o_ref[...] = jnp.dot(xn_ref[...], w_ref[...], preferred_element_type=F32)


def rms_matmul(x, g, w, tm, tn):
    m, k = x.shape
    n = w.shape[1]
    return pl.pallas_call(
        _rms_mm_kernel,
        grid=(m // tm, n // tn),
        in_specs=[pl.BlockSpec((tm, k), lambda i, j: (i, 0)),
                  pl.BlockSpec((1, k), lambda i, j: (0, 0)),
                  pl.BlockSpec((k, tn), lambda i, j: (0, j))],
        out_specs=pl.BlockSpec((tm, tn), lambda i, j: (i, j)),
        out_shape=jax.ShapeDtypeStruct((m, n), F32),
        scratch_shapes=[pltpu.VMEM((tm, k), BF16)],
        compiler_params=_cp("parallel", "arbitrary"),
        name="rms_in_proj",
    )(x, g, w)


def _mm_kernel(x_ref, w_ref, o_ref):
    o_ref[...] = jnp.dot(x_ref[...], w_ref[...], preferred_element_type=F32)


def matmul(x, w, tm):
    m, k = x.shape
    n = w.shape[1]
    return pl.pallas_call(
        _mm_kernel,
        grid=(m // tm,),
        in_specs=[pl.BlockSpec((tm, k), lambda i: (i, 0)),
                  pl.BlockSpec((k, n), lambda i: (0, 0))],
        out_specs=pl.BlockSpec((tm, n), lambda i: (i, 0)),
        out_shape=jax.ShapeDtypeStruct((m, n), F32),
        compiler_params=_cp("parallel"),
        name="cmp_matmul",
    )(x, w)


def _merge_kernel(x_ref, ga_ref, gb_ref, gc_ref, oa_ref, ob_ref, oc_ref, wb_ref, wo_ref, o_ref):
    m = jax.nn.sigmoid(ga_ref[...]) * jnp.dot(oa_ref[...], wb_ref[0], preferred_element_type=F32)
    m = m + jax.nn.sigmoid(gb_ref[...]) * jnp.dot(ob_ref[...], wb_ref[1], preferred_element_type=F32)
    m = m + jax.nn.sigmoid(gc_ref[...]) * jnp.dot(oc_ref[...], wb_ref[2], preferred_element_type=F32)
    o_ref[...] = x_ref[...] + jnp.dot(m.astype(BF16), wo_ref[...], preferred_element_type=F32)


def merge(x, y, oa, ob, oc, wb, wo, tm):
    m, d = x.shape
    return pl.pallas_call(
        _merge_kernel,
        grid=(m // tm,),
        in_specs=[pl.BlockSpec((tm, d), lambda i: (i, 0)),
                  pl.BlockSpec((tm, d), lambda i: (i, COL_GA // D_MODEL)),
                  pl.BlockSpec((tm, d), lambda i: (i, COL_GB // D_MODEL)),
                  pl.BlockSpec((tm, d), lambda i: (i, COL_GC // D_MODEL)),
                  pl.BlockSpec((tm, 512), lambda i: (i, 0)),
                  pl.BlockSpec((tm, 512), lambda i: (i, 0)),
                  pl.BlockSpec((tm, 512), lambda i: (i, 0)),
                  pl.BlockSpec((3, 512, d), lambda i: (0, 0, 0)),
                  pl.BlockSpec((d, d), lambda i: (0, 0))],
        out_specs=pl.BlockSpec((tm, d), lambda i: (i, 0)),
        out_shape=jax.ShapeDtypeStruct((m, d), F32),
        compiler_params=_cp("parallel"),
        name="merge_out_proj",
    )(x, y, y, y, oa, ob, oc, wb, wo)


def _ffn_kernel(x_ref, g_ref, wg_ref, wu_ref, wd_ref, o_ref, xn_ref):
    @pl.when(pl.program_id(1) == 0)
    def _():
        x = x_ref[...]
        ms = jnp.mean(x * x, axis=-1, keepdims=True)
        xn_ref[...] = (x * lax.rsqrt(ms + EPS) * g_ref[...]).astype(BF16)
        o_ref[...] = x

    xn = xn_ref[...]
    gate = jnp.dot(xn, wg_ref[...], preferred_element_type=F32)
    up = jnp.dot(xn, wu_ref[...], preferred_element_type=F32)
    h = (_silu(gate) * up).astype(BF16)
    o_ref[...] += jnp.dot(h, wd_ref[...], preferred_element_type=F32)


def ffn(x, g, wgu, wd, tm, tf):
    m, d = x.shape
    nf = D_FF // tf
    return pl.pallas_call(
        _ffn_kernel,
        grid=(m // tm, nf),
        in_specs=[pl.BlockSpec((tm, d), lambda i, f: (i, 0)),
                  pl.BlockSpec((1, d), lambda i, f: (0, 0)),
                  pl.BlockSpec((d, tf), lambda i, f: (0, f)),
                  pl.BlockSpec((d, tf), lambda i, f: (0, nf + f)),
                  pl.BlockSpec((tf, d), lambda i, f: (f, 0))],
        out_specs=pl.BlockSpec((tm, d), lambda i, f: (i, 0)),
        out_shape=jax.ShapeDtypeStruct((m, d), F32),
        scratch_shapes=[pltpu.VMEM((tm, d), BF16)],
        compiler_params=_cp("parallel", "arbitrary"),
        name="swiglu_ffn",
    )(x, g, wgu, wgu, wd)


def _hgrn_levels(c):
    levels = [m for m in (32, 16, 8, 4, 2, 1) if m < c]
    return [m for m in levels if m < 8], [m for m in levels if m >= 8]


def _hgrn_consts(c):
    mats = [np.tril(np.ones((c, c), np.float32))]
    r = np.arange(c)
    for m in _hgrn_levels(c)[0]:
        pos = r % (2 * m)
        mid = (r // (2 * m)) * 2 * m + m
        up = np.zeros((c, c), np.float32)
        lo = np.zeros((c, c), np.float32)
        for t in range(c):
            if pos[t] >= m:
                up[t, mid[t]:t + 1] = 1.0
            else:
                lo[t, t + 1:mid[t]] = 1.0
        mats += [up, lo]
    return np.concatenate(mats, axis=0)


def _hgrn_kernel(hq_ref, hf_ref, hi_ref, hg_ref, lb_ref, gn_ref, s0_ref, mst_ref, o_ref, sout_ref, st_ref,
                 *, c, nchunk):
    tb = pl.program_id(2)

    @pl.when(tb == 0)
    def _():
        st_ref[...] = s0_ref[0, 0].T

    small, big = _hgrn_levels(c)
    rid = lax.broadcasted_iota(jnp.int32, (c, HG_DIM), 0)
    lb = lb_ref[...]
    log_lb = jnp.log(jnp.maximum(lb, LB_FLOOR))
    log_1m = jnp.log1p(-lb)
    one_m = 1.0 - lb
    gn = gn_ref[...]
    mst = mst_ref[...]
    ti = lax.broadcasted_iota(jnp.int32, (c, c), 0)
    si = lax.broadcasted_iota(jnp.int32, (c, c), 1)
    txs = jnp.bitwise_xor(ti, si)
    lower = ti > si

    for ch in range(nchunk):
        rows = pl.ds(ch * c, c)
        fp = hf_ref[rows, :]
        hq = hq_ref[rows, :]
        v = hi_ref[rows, :]
        hg = hg_ref[rows, :]
        log_sig = -(jnp.maximum(-fp, 0.0) + jnp.log1p(jnp.exp(-jnp.abs(fp))))
        b = log_1m + log_sig
        g = jnp.maximum(log_lb, b) + jnp.log1p(jnp.exp(-jnp.abs(log_lb - b)))
        kin = one_m * jax.nn.sigmoid(-fp)
        q = _silu(hq)

        g1 = g.astype(BF16)
        r1 = g - g1.astype(F32)
        g2 = r1.astype(BF16)
        g3 = (r1 - g2.astype(F32)).astype(BF16)
        gs = jnp.concatenate([g1, g2, g3], axis=1)
        rr = jnp.dot(mst, gs, preferred_element_type=F32)
        rr = rr[:, 0:LANE] + rr[:, LANE:2 * LANE] + rr[:, 2 * LANE:3 * LANE]
        gcum = rr[0:c]

        vb = v.astype(BF16)
        att = jnp.where(ti == si, _dot_nt(q, kin), 0.0)
        for li, m in enumerate(small + big):
            if m in small:
                dq = rr[(1 + 2 * li) * c:(2 + 2 * li) * c]
                ek = rr[(2 + 2 * li) * c:(3 + 2 * li) * c]
            else:
                gb = gcum[m - 1:m, :]
                for blk in range(1, c // (2 * m)):
                    gb = jnp.where(rid >= blk * 2 * m, gcum[blk * 2 * m + m - 1:blk * 2 * m + m, :], gb)
                dq = jnp.minimum(gcum - gb, 0.0)
                ek = jnp.minimum(gb - gcum, 0.0)
            pair = _dot_nt(q * jnp.exp(dq), kin * jnp.exp(ek))
            sel = lower & (jnp.right_shift(txs, int(math.log2(m))) == 1)
            att = jnp.where(sel, pair, att)
        st = st_ref[...]
        o = _dot(att, vb) + _dot_nt(q * jnp.exp(gcum), st)
        g_end = gcum[c - 1:c, :]
        kd = kin * jnp.exp(g_end - gcum)
        st_ref[...] = st * jnp.exp(g_end) + _dot_tn(vb, kd)

        ms = jnp.mean(o * o, axis=-1, keepdims=True)
        on = o * lax.rsqrt(ms + EPS) * gn
        o_ref[rows, :] = (on * _silu(hg)).astype(BF16)

    @pl.when(tb == pl.num_programs(2) - 1)
    def _():
        sout_ref[0, 0] = st_ref[...].T


def hgrn(y, lb, gn, s0, bsz, t, c, tbk):
    ntb = t // tbk
    mst = jnp.asarray(_hgrn_consts(c), BF16)

    def col(base):
        return pl.BlockSpec((tbk, HG_DIM), lambda b, h, k: (b * ntb + k, base // HG_DIM + h))

    return pl.pallas_call(
        functools.partial(_hgrn_kernel, c=c, nchunk=tbk // c),
        grid=(bsz, HG_HEADS, ntb),
        in_specs=[col(COL_HQ), col(COL_HF), col(COL_HI), col(COL_HG),
                  pl.BlockSpec((1, HG_DIM), lambda b, h, k: (0, h)),
                  pl.BlockSpec((1, HG_DIM), lambda b, h, k: (0, 0)),
                  pl.BlockSpec((1, 1, HG_DIM, HG_DIM), lambda b, h, k: (b, h, 0, 0)),
                  pl.BlockSpec(mst.shape, lambda b, h, k: (0, 0))],
        out_specs=[pl.BlockSpec((tbk, HG_DIM), lambda b, h, k: (b * ntb + k, h)),
                   pl.BlockSpec((1, 1, HG_DIM, HG_DIM), lambda b, h, k: (b, h, 0, 0))],
        out_shape=[jax.ShapeDtypeStruct((bsz * t, HG_WIDTH), BF16),
                   jax.ShapeDtypeStruct((bsz, HG_HEADS, HG_DIM, HG_DIM), F32)],
        scratch_shapes=[pltpu.VMEM((HG_DIM, HG_DIM), F32)],
        compiler_params=_cp("parallel", "parallel", "arbitrary"),
        name="hgrn2_scan",
    )(y, y, y, y, lb, gn, s0, mst)


S5_SUB = 8


def _shift_in_group(x, d):
    n, w = x.shape
    x3 = x.reshape(n // S5_SUB, S5_SUB, w)
    rid = lax.broadcasted_iota(jnp.int32, x3.shape, 1)
    return jnp.where(rid >= d, pltpu.roll(x3, d, axis=1), 0.0).reshape(n, w)


def _s5_kernel(u_ref, bre_ref, bim_ref, adr_ref, adi_ref, atr_ref, ati_ref, h0r_ref, h0i_ref,
               cre_ref, cim_ref, d_ref, wg_ref, o_ref, hr_out, hi_out, cr_ref, ci_ref, *, ts):
    k = pl.program_id(1)

    @pl.when(k == 0)
    def _():
        cr_ref[...] = h0r_ref[0]
        ci_ref[...] = h0i_ref[0]

    u = u_ref[...]
    ub = u.astype(BF16)
    xr = jnp.dot(ub, bre_ref[...], preferred_element_type=F32)
    xi = jnp.dot(ub, bim_ref[...], preferred_element_type=F32)
    lev = 0
    d = 1
    while d < S5_SUB:
        ar = adr_ref[lev:lev + 1, :]
        ai = adi_ref[lev:lev + 1, :]
        sr = _shift_in_group(xr, d)
        si = _shift_in_group(xi, d)
        xr, xi = xr + ar * sr - ai * si, xi + ar * si + ai * sr
        d *= 2
        lev += 1
    cr = cr_ref[...]
    ci = ci_ref[...]
    atr = atr_ref[...]
    ati = ati_ref[...]
    hrs, his = [], []
    for r in range(ts // S5_SUB):
        gr = xr[r * S5_SUB:(r + 1) * S5_SUB]
        gi = xi[r * S5_SUB:(r + 1) * S5_SUB]
        gr, gi = gr + atr * cr - ati * ci, gi + atr * ci + ati * cr
        hrs.append(gr)
        his.append(gi)
        cr, ci = gr[S5_SUB - 1:S5_SUB], gi[S5_SUB - 1:S5_SUB]
    hr = jnp.concatenate(hrs, axis=0)
    hi = jnp.concatenate(his, axis=0)
    cr_ref[...] = cr
    ci_ref[...] = ci
    y = jnp.dot(hr.astype(BF16), cre_ref[...], preferred_element_type=F32) \
        - jnp.dot(hi.astype(BF16), cim_ref[...], preferred_element_type=F32)
    y = _gelu_tanh(y + d_ref[...] * u)
    o_ref[...] = (y * jax.nn.sigmoid(jnp.dot(y.astype(BF16), wg_ref[...], preferred_element_type=F32))).astype(BF16)

    @pl.when(k == pl.num_programs(1) - 1)
    def _():
        hr_out[0] = cr
        hi_out[0] = ci


def s5(y, prm, h0r, h0i, bsz, t, ts):
    nts = t // ts
    nlev = prm["adr"].shape[0]
    const = lambda shape: pl.BlockSpec(shape, lambda b, k: tuple(0 for _ in shape))
    state = pl.BlockSpec((1, 1, S5_CH), lambda b, k: (b, 0, 0))
    return pl.pallas_call(
        functools.partial(_s5_kernel, ts=ts),
        grid=(bsz, nts),
        in_specs=[pl.BlockSpec((ts, S5_WIDTH), lambda b, k: (b * nts + k, COL_SU // S5_WIDTH)),
                  const((S5_WIDTH, S5_CH)), const((S5_WIDTH, S5_CH)),
                  const((nlev, S5_CH)), const((nlev, S5_CH)),
                  const((S5_SUB, S5_CH)), const((S5_SUB, S5_CH)),
                  state, state,
                  const((S5_CH, S5_WIDTH)), const((S5_CH, S5_WIDTH)),
                  const((1, S5_WIDTH)), const((S5_WIDTH, S5_WIDTH))],
        out_specs=[pl.BlockSpec((ts, S5_WIDTH), lambda b, k: (b * nts + k, 0)), state, state],
        out_shape=[jax.ShapeDtypeStruct((bsz * t, S5_WIDTH), BF16),
                   jax.ShapeDtypeStruct((bsz, 1, S5_CH), F32),
                   jax.ShapeDtypeStruct((bsz, 1, S5_CH), F32)],
        scratch_shapes=[pltpu.VMEM((1, S5_CH), F32), pltpu.VMEM((1, S5_CH), F32)],
        compiler_params=_cp("parallel", "arbitrary"),
        name="s5_scan",
    )(y, prm["bre"], prm["bim"], prm["adr"], prm["adi"], prm["atr"], prm["ati"], h0r, h0i,
      prm["cre"], prm["cim"], prm["d"], prm["wglu"])


def _cmul(ar, ai, br, bi):
    return ar * br - ai * bi, ar * bi + ai * br


def s5_params(a_re, a_im, log_dt, b_re, b_im, c_re, c_im, d, w_glu, ts):
    step = jnp.exp(log_dt)[:, None]
    mag = jnp.exp(a_re * step)
    ab_re, ab_im = mag * jnp.cos(a_im * step), mag * jnp.sin(a_im * step)
    den = a_re * a_re + a_im * a_im
    z_re = ((ab_re - 1.0) * a_re + ab_im * a_im) / den
    z_im = (ab_im * a_re - (ab_re - 1.0) * a_im) / den
    bb_re = z_re[..., None] * b_re - z_im[..., None] * b_im
    bb_im = z_re[..., None] * b_im + z_im[..., None] * b_re
    eye = jnp.eye(S5_GROUPS, dtype=F32)
    bd_in = lambda w: jnp.einsum("gpn,gh->gnhp", w, eye).reshape(S5_WIDTH, S5_CH).astype(BF16)
    bd_out = lambda w: jnp.einsum("gnp,gh->gphn", w, eye).reshape(S5_CH, S5_WIDTH).astype(BF16)
    ar, ai = ab_re.reshape(1, S5_CH), ab_im.reshape(1, S5_CH)
    adr, adi = [ar], [ai]
    d2 = 2
    while d2 < ts:
        r, i = _cmul(adr[-1], adi[-1], adr[-1], adi[-1])
        adr.append(r)
        adi.append(i)
        d2 *= 2
    atr, ati = ar, ai
    n = 1
    lev = 0
    while n < ts:
        pr, pi = _cmul(atr, ati, adr[lev], adi[lev])
        atr, ati = jnp.concatenate([atr, pr], 0), jnp.concatenate([ati, pi], 0)
        n *= 2
        lev += 1
    return {"bre": bd_in(bb_re), "bim": bd_in(bb_im), "cre": bd_out(c_re), "cim": bd_out(c_im),
            "adr": jnp.concatenate(adr, 0), "adi": jnp.concatenate(adi, 0), "atr": atr, "ati": ati,
            "d": d.reshape(1, S5_WIDTH), "wglu": w_glu.astype(BF16)}


def prep_w_in(w):
    pad = jnp.zeros((w.shape[0], N_IN_PAD - N_IN), w.dtype)
    return jnp.concatenate([w[:, 3864:6936], w[:, 0:3840], w[:, 3840:3864], pad], axis=1).astype(BF16)


def _seg_rms(x, ones, w):
    ss = _dot2(x * x, ones)
    return x * lax.rsqrt(ss * (1.0 / HEAD_DIM) + EPS) * w


def _kv_operands(k, v):
    tm = k.shape[0]
    zero, one = jnp.zeros((tm, HEAD_DIM), F32), jnp.ones((tm, HEAD_DIM), F32)
    parts = []
    for g in range(NSA_KV):
        parts += [k[:, g * HEAD_DIM:(g + 1) * HEAD_DIM], zero]
    for g in range(NSA_KV):
        parts += [v[:, g * HEAD_DIM:(g + 1) * HEAD_DIM], one]
    return jnp.concatenate(parts, axis=1).astype(BF16)


def _nsa_prep_kernel(nq_ref, kv_ref, wk_ref, wq_ref, wks_ref, wkw_ref, ones_ref,
                     qn_ref, rows_ref, win_ref, cv_ref, selx_ref, winx_ref):
    qn_ref[...] = _seg_rms(nq_ref[...], ones_ref[...], wq_ref[...]).astype(BF16)
    ones1 = ones_ref[0:LANE, 0:LANE]
    kv = kv_ref[...]
    ksn = _seg_rms(kv[:, 2 * KV_W:3 * KV_W], ones1, wks_ref[...])
    rows_ref[...] = jnp.concatenate([kv[:, 0:2 * KV_W], ksn, kv[:, 3 * KV_W:4 * KV_W]], axis=1)
    cv_ref[...] = kv[:, 0:2 * KV_W].astype(BF16)
    selx_ref[...] = _kv_operands(ksn, kv[:, 3 * KV_W:4 * KV_W])
    wk = wk_ref[...]
    kwn = _seg_rms(wk[:, 0:KV_W], ones1, wkw_ref[...])
    win_ref[...] = jnp.concatenate([kwn, wk[:, KV_W:2 * KV_W]], axis=1)
    winx_ref[...] = _kv_operands(kwn, wk[:, KV_W:2 * KV_W])


def _block_ones(n, blk):
    r = np.arange(n) // blk
    return jnp.asarray((r[:, None] == r[None, :]).astype(np.float32), BF16)


def nsa_prep(y, qkn, tm):
    m = y.shape[0]
    wq = (jnp.tile(qkn[0], NSA_HEADS) * HEAD_DIM ** -0.5).reshape(1, NSA_WIDTH)
    wks = jnp.tile(qkn[2], NSA_KV).reshape(1, KV_W)
    wkw = jnp.tile(qkn[3], NSA_KV).reshape(1, KV_W)
    ones = _block_ones(NSA_WIDTH, HEAD_DIM)
    const = lambda shape: pl.BlockSpec(shape, lambda i: (0, 0))
    row = lambda w: pl.BlockSpec((tm, w), lambda i: (i, 0))
    return pl.pallas_call(
        _nsa_prep_kernel,
        grid=(m // tm,),
        in_specs=[pl.BlockSpec((tm, 512), lambda i: (i, COL_NQ // 512)),
                  pl.BlockSpec((tm, 512), lambda i: (i, COL_KV4 // 512)),
                  pl.BlockSpec((tm, 256), lambda i: (i, COL_WKV // 256)),
                  const((1, NSA_WIDTH)), const((1, KV_W)), const((1, KV_W)), const((NSA_WIDTH, NSA_WIDTH))],
        out_specs=[row(512), row(512), row(256), row(256), row(512), row(512)],
        out_shape=[jax.ShapeDtypeStruct((m, NSA_WIDTH), BF16), jax.ShapeDtypeStruct((m, 4 * KV_W), F32),
                   jax.ShapeDtypeStruct((m, 2 * KV_W), F32), jax.ShapeDtypeStruct((m, 2 * KV_W), BF16),
                   jax.ShapeDtypeStruct((m, 4 * LANE), BF16), jax.ShapeDtypeStruct((m, 4 * LANE), BF16)],
        compiler_params=_cp("parallel"),
        name="nsa_prep",
    )(y, y, y, wq, wks, wkw, ones)


def _cmp_post_kernel(c_ref, cb_ref, w2_ref, wk_ref, ones_ref, k_ref, v_ref, *, transposed):
    c = c_ref[0]
    if transposed:
        nc = c.shape[1]
        ca, cb = c[0:512], c[512:1024]
        lid = lax.broadcasted_iota(jnp.int32, cb.shape, 1)
        cb_next = jnp.where(lid < nc - 1, pltpu.roll(cb, nc - 1, axis=1), 0.0)
        hid = _gelu_tanh(ca + cb_next + cb_ref[...])
        out = _dot(w2_ref[...], hid)
        kc = out[0:KV_W]
        hi, lo = _split2(kc * kc)
        ss = jnp.dot(ones_ref[...], hi, preferred_element_type=F32) + jnp.dot(ones_ref[...], lo, preferred_element_type=F32)
        k_ref[0] = (kc * lax.rsqrt(ss * (1.0 / HEAD_DIM) + EPS) * wk_ref[...]).astype(BF16)
        v_ref[0] = out[KV_W:2 * KV_W].astype(BF16)
    else:
        nc = c.shape[0]
        ca, cb = c[:, 0:512], c[:, 512:1024]
        rid = lax.broadcasted_iota(jnp.int32, cb.shape, 0)
        cb_next = jnp.where(rid < nc - 1, pltpu.roll(cb, nc - 1, axis=0), 0.0)
        hid = _gelu_tanh(ca + cb_next + cb_ref[...])
        out = _dot(hid, w2_ref[...])
        k_ref[0] = _seg_rms(out[:, 0:KV_W], ones_ref[...], wk_ref[...]).T.astype(BF16)
        v_ref[0] = out[:, KV_W:2 * KV_W].T.astype(BF16)


def cmp_post(cacb, cbias, w2blk, qkn1, transposed):
    bsz = cacb.shape[0]
    nc = cacb.shape[2] if transposed else cacb.shape[1]
    wk = jnp.tile(qkn1, NSA_KV)
    ones = _block_ones(KV_W, HEAD_DIM)
    if transposed:
        cbias, w2blk, wk = cbias.reshape(512, 1), w2blk.T, wk.reshape(KV_W, 1)
    else:
        wk = wk.reshape(1, KV_W)
    const = lambda a: pl.BlockSpec(a.shape, lambda b: (0, 0))
    out = pl.BlockSpec((1, KV_W, nc), lambda b: (b, 0, 0))
    return pl.pallas_call(
        functools.partial(_cmp_post_kernel, transposed=transposed),
        grid=(bsz,),
        in_specs=[pl.BlockSpec((1,) + cacb.shape[1:], lambda b: (b, 0, 0)),
                  const(cbias), const(w2blk), const(wk), const(ones)],
        out_specs=[out, out],
        out_shape=[jax.ShapeDtypeStruct((bsz, KV_W, nc), BF16)] * 2,
        compiler_params=_cp("parallel"),
        name="cmp_post",
    )(cacb, cbias, w2blk, wk, ones)


def _cmp_paged_kernel(pt_ref, *refs, pg):
    pages = refs[:pg]
    wt_ref, eye_ref, o_ref, tok_ref = refs[pg:]
    nhalf = 2 * KV_W // LANE
    for k in range(pg):
        tok = _dot_nt(eye_ref[...], pages[k][0])
        for c in range(nhalf):
            tok_ref[c, k * PAGE_SIZE:(k + 1) * PAGE_SIZE, :] = tok[:, c * LANE:(c + 1) * LANE]
    nch = pg * PAGE_SIZE // CMP_STRIDE
    acc = jnp.zeros((1024, nch), F32)
    for j in range(CMP_STRIDE):
        rj = jnp.concatenate([tok_ref[c, pl.ds(j, nch, stride=CMP_STRIDE), :] for c in range(nhalf)], axis=1)
        acc = acc + _dot_nt(wt_ref[j], rj)
    o_ref[0] = acc


def cmp_paged(cache_t, pt_flat, wab, bsz, npages, pg):
    nch = pg * PAGE_SIZE // CMP_STRIDE
    nc = npages * PAGE_SIZE // CMP_STRIDE
    wt = wab.reshape(CMP_STRIDE, 2 * KV_W, 1024).transpose(0, 2, 1)
    eye = jnp.eye(PAGE_SIZE, dtype=BF16)

    def page_spec(k):
        return pl.BlockSpec((1, 2 * KV_W, PAGE_SIZE), lambda b, p, pt: (pt[b * npages + p * pg + k], 0, 0))

    return pl.pallas_call(
        functools.partial(_cmp_paged_kernel, pg=pg),
        grid_spec=pltpu.PrefetchScalarGridSpec(
            num_scalar_prefetch=1, grid=(bsz, npages // pg),
            in_specs=[page_spec(k) for k in range(pg)] +
                     [pl.BlockSpec(wt.shape, lambda b, p, pt: (0, 0, 0)), pl.BlockSpec(eye.shape, lambda b, p, pt: (0, 0))],
            out_specs=pl.BlockSpec((1, 1024, nch), lambda b, p, pt: (b, 0, p)),
            scratch_shapes=[pltpu.VMEM((2 * KV_W // LANE, pg * PAGE_SIZE, LANE), F32)]),
        out_shape=jax.ShapeDtypeStruct((bsz, 1024, nc), F32),
        compiler_params=_cp("parallel", "arbitrary"),
        name="cmp_paged",
    )(pt_flat, *([cache_t] * pg), wt, eye)


def cmp_params(pe, w1, w2):
    w1r = w1.reshape(2, 2, 16, HEAD_DIM, CMP_HIDDEN)
    eye2 = jnp.eye(2, dtype=F32)
    wab = jnp.einsum("khjdc,kl,gm->jkgdhlmc", w1r, eye2, eye2).reshape(16 * 256, 1024).astype(BF16)
    cb = jnp.einsum("kf,kfc->kc", pe.reshape(2, CMP_LEN * HEAD_DIM), w1)
    cbias = jnp.broadcast_to(cb[:, None, :], (2, NSA_KV, CMP_HIDDEN)).reshape(1, 512)
    w2blk = jnp.einsum("kcd,kl,gm->kgclmd", w2, eye2, eye2).reshape(512, 256).astype(BF16)
    return wab, cbias, w2blk


def _cmpsel_kernel(qn_ref, kc_ref, vc_ref, bias_ref, ov_ref, ocmp_ref, selm_ref, *, tq, p0, n_sel, nselp, nsr):
    i = pl.program_id(0)
    q = qn_ref[...]
    kc = kc_ref[0]
    vc = vc_ref[0]
    nc = kc.shape[1]
    qpos = p0 + i * tq + lax.broadcasted_iota(jnp.int32, (tq, 1), 0)
    kend = lax.broadcasted_iota(jnp.int32, (1, nc), 1) * CMP_STRIDE + (CMP_LEN - 1)
    valid = qpos >= kend
    row_valid = (qpos >= CMP_LEN - 1).astype(F32)
    imp = [None, None]
    outs = []
    for h in range(NSA_HEADS):
        g = h // NSA_REP
        s = _dot(q[:, h * HEAD_DIM:(h + 1) * HEAD_DIM], kc[g * HEAD_DIM:(g + 1) * HEAD_DIM, :]) + bias_ref[h]
        s = jnp.where(valid, s, NEG)
        e = jnp.exp(s - jnp.max(s, axis=-1, keepdims=True))
        p = e / jnp.sum(e, axis=-1, keepdims=True) * row_valid
        outs.append(_dot_nt(p, vc[g * HEAD_DIM:(g + 1) * HEAD_DIM, :]))
        imp[g] = p if imp[g] is None else imp[g] + p
    ocmp_ref[...] = jnp.concatenate(outs, axis=1)

    jid = lax.broadcasted_iota(jnp.int32, (1, nselp), 1)
    cur = jnp.right_shift(qpos, int(math.log2(SEL_BLOCK)))
    forced = (jid == 0) | (jid == cur) | (jid == cur - 1)
    future = jid * SEL_BLOCK > qpos
    topn = min(SEL_TOPN, n_sel)
    for g in range(NSA_KV):
        score = _dot2(imp[g], ov_ref[...])
        score = jnp.where(forced, FORCE, jnp.where(future, -FORCE, score))
        if nsr:
            st = score.T[0:nsr]
            rid = lax.broadcasted_iota(jnp.int32, (nsr, tq), 0)
            cnt = jnp.zeros((nsr, tq), F32)
            for c in range(n_sel):
                tie = jnp.where(rid > c, 1.0, 0.0)
                row = st[c:c + 1, :]
                cnt = cnt + jnp.where(row > st, 1.0, jnp.where(row == st, tie, 0.0))
            sel = jnp.where(rid < n_sel, jnp.where(cnt < topn, 1.0, 0.0), 0.0)
            sel = jnp.concatenate([sel, jnp.zeros((nselp - nsr, tq), F32)], axis=0).T
            selm_ref[:, g * nselp:(g + 1) * nselp] = sel.astype(BF16)
        else:
            cnt = jnp.zeros((tq, nselp), F32)
            for c in range(n_sel):
                col = score[:, c:c + 1]
                tie = (jid > c).astype(F32)
                cnt = cnt + jnp.where(col > score, 1.0, jnp.where(col == score, tie, 0.0))
            sel = jnp.where(jid < n_sel, jnp.where(cnt < topn, 1.0, 0.0), 0.0)
            selm_ref[:, g * nselp:(g + 1) * nselp] = sel.astype(BF16)


def cmpsel(qn, kcmp, vcmp, biasc, overlap, bsz, t, tq, p0, n_sel, nsr=0):
    nt = t // tq
    nc = kcmp.shape[2]
    nselp = overlap.shape[1]
    assert CMP_STRIDE * (nc - 1) + CMP_LEN - 1 > p0 + t - 1
    assert not nsr or (tq == LANE and nselp == LANE)
    sel_spec = pl.BlockSpec((tq, NSA_KV * nselp), lambda i, b: (b * nt + i, 0))
    sel_shape = jax.ShapeDtypeStruct((bsz * t, NSA_KV * nselp), BF16)
    return pl.pallas_call(
        functools.partial(_cmpsel_kernel, tq=tq, p0=p0, n_sel=n_sel, nselp=nselp, nsr=nsr),
        grid=(nt, bsz),
        in_specs=[pl.BlockSpec((tq, NSA_WIDTH), lambda i, b: (b * nt + i, 0)),
                  pl.BlockSpec((1, KV_W, nc), lambda i, b: (b, 0, 0)),
                  pl.BlockSpec((1, KV_W, nc), lambda i, b: (b, 0, 0)),
                  pl.BlockSpec((NSA_HEADS, tq, nc), lambda i, b: (0, i, 0)),
                  pl.BlockSpec((nc, nselp), lambda i, b: (0, 0))],
        out_specs=[pl.BlockSpec((tq, NSA_WIDTH), lambda i, b: (b * nt + i, 0)), sel_spec],
        out_shape=[jax.ShapeDtypeStruct((bsz * t, NSA_WIDTH), F32), sel_shape],
        compiler_params=_cp("arbitrary", "arbitrary"),
        name="cmp_attn_select",
    )(qn, kcmp, vcmp, biasc, overlap)


def _stack_heads(q, g):
    return jnp.concatenate([q[:, (NSA_REP * g + r) * HEAD_DIM:(NSA_REP * g + r + 1) * HEAD_DIM]
                            for r in range(NSA_REP)], axis=0)


def _unstack_heads(o_groups, tq):
    return jnp.concatenate([o[r * tq:(r + 1) * tq] for o in o_groups for r in range(NSA_REP)], axis=1)


def _online_update(carry, s, v, v_transposed=False):
    m, l, acc = carry
    m_new = jnp.maximum(m, jnp.max(s, axis=-1, keepdims=True))
    a = jnp.exp(m - m_new)
    p = jnp.exp(s - m_new)
    pv = _dot_nt(p, v) if v_transposed else _dot(p, v)
    return m_new, a * l + jnp.sum(p, axis=-1, keepdims=True), a * acc + pv


def _gated_sum(ng, ge, ocmp, osel, owin):
    gx = _dot2(jax.nn.sigmoid(ng), ge)
    return gx[:, 0:512] * ocmp + gx[:, 512:1024] * osel + gx[:, 1024:1536] * owin


def _online_update_fused(carry, s, vx):
    m, acc = carry
    m_new = jnp.maximum(m, jnp.max(s, axis=-1, keepdims=True))
    p = jnp.exp(s - m_new)
    return m_new, jnp.exp(m - m_new) * acc + jnp.dot(p.astype(BF16), vx, preferred_element_type=F32)


def _finish_fused(carry):
    acc = carry[1]
    return acc[:, 0:HEAD_DIM] / acc[:, HEAD_DIM:HEAD_DIM + 1]


def _prompt_attn_kernel(qn_ref, sx_ref, selm_ref, e_ref, bnear_ref, wx_ref, bw_ref,
                        ocmp_ref, ng_ref, ge_ref, oc_ref, madd_ref, *, tq, nselp, nprev, kt):
    i = pl.program_id(1)
    q = qn_ref[...]
    rows = NSA_REP * tq
    n_far = jnp.maximum(i - 1, 0)
    n_macro = n_far // kt
    init1 = (jnp.full((rows, 1), -jnp.inf, F32), jnp.zeros((rows, LANE), F32))
    init = (init1,) * NSA_KV
    heads = [slice(NSA_REP * g, NSA_REP * (g + 1)) for g in range(NSA_KV)]
    kcol = [slice(g * LANE, (g + 1) * LANE) for g in range(NSA_KV)]
    vcol = [slice((NSA_KV + g) * LANE, (NSA_KV + g + 1) * LANE) for g in range(NSA_KV)]
    zpad = jnp.zeros((rows, LANE - HEAD_DIM), BF16)
    qgs = [jnp.concatenate([_stack_heads(q, g), zpad], axis=1) for g in range(NSA_KV)]
    for g in range(NSA_KV):
        hit = jnp.dot(selm_ref[:, g * nselp:(g + 1) * nselp], e_ref[...], preferred_element_type=F32)
        madd_ref[g] = (hit - 1.0) * (-NEG)

    def step(r0, width, carries, near):
        out = []
        for g in range(NSA_KV):
            add = madd_ref[g, :, pl.ds(r0, width)][None]
            if near is not None:
                add = add + bnear_ref[heads[g], :, near]
            s = _dot_nt(qgs[g], sx_ref[pl.ds(r0, width), kcol[g]]).reshape(NSA_REP, tq, width) + add
            out.append(_online_update_fused(carries[g], s.reshape(rows, width), sx_ref[pl.ds(r0, width), vcol[g]]))
        return tuple(out)

    def far(j, ntiles, c):
        return step(pl.multiple_of(j * tq, tq), ntiles * tq, c, None)

    carries = lax.fori_loop(0, n_macro, lambda j, c: far(j * kt, kt, c), init)
    done = n_macro * kt
    w = kt // 2
    while w >= 1:
        take = jnp.bitwise_and(n_far - done, w) != 0
        carries = lax.cond(take, functools.partial(far, done, w), lambda c: c, carries)
        done = done + jnp.where(take, w, 0)
        w //= 2
    carries = lax.cond(
        i >= 1,
        lambda c: step(pl.multiple_of((i - 1) * tq, tq), 2 * tq, c, slice(0, 2 * tq)),
        lambda c: step(0, tq, c, slice(tq, 2 * tq)),
        carries)
    o_sel = [_finish_fused(c) for c in carries]

    o_win = []
    nk = (nprev + 1) * tq
    r0 = pl.multiple_of(i * tq, tq)
    kval = lax.broadcasted_iota(jnp.int32, (1, 1, nk), 2) >= (nprev - i) * tq
    for g in range(NSA_KV):
        s = _dot_nt(qgs[g], wx_ref[0, pl.ds(r0, nk), kcol[g]]).reshape(NSA_REP, tq, nk) + bw_ref[heads[g]]
        s = jnp.where(kval, s, NEG).reshape(rows, nk)
        o_win.append(_finish_fused(_online_update_fused(init1, s, wx_ref[0, pl.ds(r0, nk), vcol[g]])))

    oc = _gated_sum(ng_ref[...], ge_ref[...], ocmp_ref[...], _unstack_heads(o_sel, tq), _unstack_heads(o_win, tq))
    oc_ref[...] = oc.astype(BF16)


def _rel_bucket(dist):
    n = jnp.maximum(dist, 0)
    exact = N_BUCKETS // 2
    nf = jnp.maximum(n, 1).astype(F32)
    large = exact + (jnp.log(nf / exact) / math.log(MAX_DIST / exact) * (N_BUCKETS - exact)).astype(jnp.int32)
    return jnp.where(n < exact, n, jnp.minimum(large, N_BUCKETS - 1))


def _bias_last(rel_bias, dist, valid=None):
    onehot = (_rel_bucket(dist)[..., None] == jnp.arange(N_BUCKETS)).astype(F32)
    b = jnp.einsum("...k,kh->...h", onehot, rel_bias.astype(F32), precision=lax.Precision.HIGHEST)
    if valid is not None:
        b = jnp.where(valid[..., None], b, NEG)
    return b


def _bias_table(rel_bias, dist, valid=None):
    return jnp.moveaxis(_bias_last(rel_bias, dist, valid), -1, 0)


def _bias_table_t(rel_bias, dist, valid=None):
    b = jnp.swapaxes(_bias_last(rel_bias, dist, valid), -1, -2)
    return b.reshape(b.shape[:-2] + (b.shape[-2] * b.shape[-1],))


def _gate_expand():
    ge = np.zeros((LANE, 3 * NSA_WIDTH), np.float32)
    for br in range(3):
        for h in range(NSA_HEADS):
            ge[br * NSA_HEADS + h, br * NSA_WIDTH + h * HEAD_DIM: br * NSA_WIDTH + (h + 1) * HEAD_DIM] = 1.0
    return jnp.asarray(ge, BF16)


def _block_expand(nselp, length):
    e = (np.arange(nselp)[:, None] == (np.arange(length) // SEL_BLOCK)[None, :]).astype(np.float32)
    return jnp.asarray(e, BF16)


def prompt_attn_tables(rel_bias, tq):
    nprev = WINDOW // tq
    nk = (nprev + 1) * tq
    ar = jnp.arange
    dn = tq + ar(tq)[:, None] - ar(2 * tq)[None, :]
    dw = ar(tq)[:, None] + nprev * tq - ar(nk)[None, :]
    assert tq + 1 >= MAX_DIST
    far = rel_bias[N_BUCKETS - 1].astype(F32)[:, None, None]
    return {"near": _bias_table(rel_bias, dn, dn >= 0) - far,
            "win": _bias_table(rel_bias, dw, (dw >= 0) & (dw < WINDOW))}


def prompt_attn(qn, selx, selm, ocmp, y, winx, tabs, bsz, t, tq, kt):
    nt = t // tq
    nselp = selm.shape[1] // NSA_KV
    nprev = WINDOW // tq
    nk = (nprev + 1) * tq
    winp = jnp.pad(winx.reshape(bsz, t, 4 * LANE), ((0, 0), (nprev * tq, 0), (0, 0)))
    const = lambda shape: pl.BlockSpec(shape, lambda b, i: tuple(0 for _ in shape))
    tile = lambda w, c: pl.BlockSpec((tq, w), lambda b, i: (b * nt + i, c))
    return pl.pallas_call(
        functools.partial(_prompt_attn_kernel, tq=tq, nselp=nselp, nprev=nprev, kt=kt),
        grid=(bsz, nt),
        in_specs=[tile(NSA_WIDTH, 0),
                  pl.BlockSpec((t, 4 * LANE), lambda b, i: (b, 0)),
                  tile(NSA_KV * nselp, 0),
                  const((nselp, t)), const((NSA_HEADS, tq, 2 * tq)),
                  pl.BlockSpec((1, t + nprev * tq, 4 * LANE), lambda b, i: (b, 0, 0)),
                  const((NSA_HEADS, tq, nk)),
                  tile(NSA_WIDTH, 0),
                  tile(LANE, COL_NGATE // LANE),
                  const((LANE, 3 * NSA_WIDTH))],
        out_specs=tile(NSA_WIDTH, 0),
        out_shape=jax.ShapeDtypeStruct((bsz * t, NSA_WIDTH), BF16),
        scratch_shapes=[pltpu.VMEM((NSA_KV, tq, t), F32)],
        compiler_params=_cp("parallel", "arbitrary"),
        name="prompt_sel_win_attn",
    )(qn, selx, selm, _block_expand(nselp, t), tabs["near"], winp, tabs["win"], ocmp, y, _gate_expand())


def _decode_attn_kernel(*refs, t, nselp, masked, pg):
    refs = list(refs[1:] if pg else refs)
    qn_ref = refs.pop(0)
    past = [refs.pop(0) for _ in range(pg if pg else 2)]
    bp_ref, kn_ref, vn_ref, bn_ref = (refs.pop(0) for _ in range(4))
    if masked:
        selm_ref, e_ref = refs.pop(0), refs.pop(0)
    o_ref, m_ref, l_ref, acc_ref = refs
    if pg:
        j, bi, last = pl.program_id(0), pl.program_id(1), pl.num_programs(0) - 1
    else:
        j, bi, last = pl.program_id(1), 0, pl.num_programs(1) - 1
    rows = NSA_REP * t

    @pl.when(j == 0)
    def _():
        m_ref[bi] = jnp.full(m_ref.shape[1:], -jnp.inf, F32)
        l_ref[bi] = jnp.zeros(l_ref.shape[1:], F32)
        acc_ref[bi] = jnp.zeros(acc_ref.shape[1:], F32)

    q = qn_ref[...]
    tk = bp_ref.shape[2]
    state = [(m_ref[bi, g], l_ref[bi, g], acc_ref[bi, g]) for g in range(NSA_KV)]
    for g in range(NSA_KV):
        hs = slice(NSA_REP * g, NSA_REP * (g + 1))
        cs = slice(g * HEAD_DIM, (g + 1) * HEAD_DIM)
        qg = _stack_heads(q, g)
        if pg:
            kt = jnp.concatenate([p[0, g * HEAD_DIM:(g + 1) * HEAD_DIM, :] for p in past], axis=1)
            vt = jnp.concatenate([p[0, KV_W + g * HEAD_DIM:KV_W + (g + 1) * HEAD_DIM, :] for p in past], axis=1)
            s = _dot(qg, kt)
        else:
            s = _dot_nt(qg, past[0][0][:, cs])
        s = s.reshape(NSA_REP, t, tk) + bp_ref[hs]
        if masked:
            msk = jnp.dot(selm_ref[0, 0, g], e_ref[...], preferred_element_type=F32) > 0.5
            s = jnp.where(msk[None], s, NEG)
        if pg:
            state[g] = _online_update(state[g], s.reshape(rows, tk), vt, v_transposed=True)
        else:
            state[g] = _online_update(state[g], s.reshape(rows, tk), past[1][0][:, cs])
    for g in range(NSA_KV):
        m_ref[bi, g], l_ref[bi, g], acc_ref[bi, g] = state[g]

    @pl.when(j < last)
    def _():
        o_ref[0] = jnp.zeros(o_ref.shape[1:], F32)

    @pl.when(j == last)
    def _():
        outs = []
        for g in range(NSA_KV):
            hs = slice(NSA_REP * g, NSA_REP * (g + 1))
            cs = slice(g * HEAD_DIM, (g + 1) * HEAD_DIM)
            qg = _stack_heads(q, g)
            s = _dot_nt(qg, kn_ref[:, cs]).reshape(NSA_REP, t, t) + bn_ref[hs]
            _, l, acc = _online_update((m_ref[bi, g], l_ref[bi, g], acc_ref[bi, g]), s.reshape(rows, t), vn_ref[:, cs])
            outs.append(acc / l)
        o_ref[0] = _unstack_heads(outs, t)


def decode_attn(qn, kv_past, bias_past, new_rows, new_cols, bias_new, bsz, t, tk, selm=None, paged=None):
    masked = selm is not None
    assert not masked or paged
    nselp = selm.shape[1] // NSA_KV if masked else 0
    if paged:
        pt_flat, npages, pg = paged
        assert tk == pg * PAGE_SIZE
        nkt = npages // pg
        ix = lambda f: (lambda j, b, pt: f(b, j))

        def page_spec(k):
            return pl.BlockSpec((1, 2 * KV_W, PAGE_SIZE), lambda j, b, pt: (pt[b * npages + j * pg + k], 1, 0))

        past_specs, past_args = [page_spec(k) for k in range(pg)], [kv_past] * pg
    else:
        pg = 0
        nkt = kv_past.shape[1] // tk
        ix = lambda f: f
        past_specs = [pl.BlockSpec((1, tk, KV_W), lambda b, j: (b, j, 0)),
                      pl.BlockSpec((1, tk, KV_W), lambda b, j: (b, j, 1))]
        past_args = [kv_past, kv_past]
    in_specs = ([pl.BlockSpec((t, NSA_WIDTH), ix(lambda b, j: (b, 0)))] + past_specs +
                [pl.BlockSpec((NSA_HEADS, t, tk), ix(lambda b, j: (0, 0, j))),
                 pl.BlockSpec((t, KV_W), ix(lambda b, j: (b, new_cols[0]))),
                 pl.BlockSpec((t, KV_W), ix(lambda b, j: (b, new_cols[1]))),
                 pl.BlockSpec((NSA_HEADS, t, t), ix(lambda b, j: (0, 0, 0)))])
    args = [qn] + past_args + [bias_past, new_rows, new_rows, bias_new]
    if masked:
        nbt = tk // SEL_BLOCK
        sel_t = jnp.stack([selm[:, g * nselp:g * nselp + nkt * nbt].reshape(bsz, t, nkt, nbt) for g in range(NSA_KV)])
        sel_t = sel_t.transpose(1, 3, 0, 2, 4)
        in_specs += [pl.BlockSpec((1, 1, NSA_KV, t, nbt), ix(lambda b, j: (b, j, 0, 0, 0))),
                     pl.BlockSpec((nbt, tk), ix(lambda b, j: (0, 0)))]
        args += [sel_t, _block_expand(nbt, tk)]
    rows = NSA_REP * t
    nslab = nkt if paged else 1
    out_spec = pl.BlockSpec((1, t, NSA_WIDTH), ix(lambda b, j: (j if paged else 0, b, 0)))
    nst = bsz if paged else 1
    scratch = [pltpu.VMEM((nst, NSA_KV, rows, 1), F32), pltpu.VMEM((nst, NSA_KV, rows, 1), F32),
               pltpu.VMEM((nst, NSA_KV, rows, HEAD_DIM), F32)]
    body = functools.partial(_decode_attn_kernel, t=t, nselp=nselp, masked=masked, pg=pg)
    common = dict(out_shape=jax.ShapeDtypeStruct((nslab, bsz * t, NSA_WIDTH), F32),
                  name="decode_sel_attn" if masked else "decode_win_attn")
    if paged:
        grid_spec = pltpu.PrefetchScalarGridSpec(num_scalar_prefetch=1, grid=(nkt, bsz), in_specs=in_specs,
                                                 out_specs=out_spec, scratch_shapes=scratch)
        out = pl.pallas_call(body, grid_spec=grid_spec, compiler_params=_cp("arbitrary", "arbitrary"),
                             **common)(pt_flat, *args)
    else:
        out = pl.pallas_call(body, grid=(bsz, nkt), in_specs=in_specs, out_specs=out_spec, scratch_shapes=scratch,
                             compiler_params=_cp("parallel", "arbitrary"), **common)(*args)
    return out[nslab - 1]


def _combine_kernel(ng_ref, ge_ref, ocmp_ref, osel_ref, owin_ref, oc_ref):
    oc_ref[...] = _gated_sum(ng_ref[...], ge_ref[...], ocmp_ref[...], osel_ref[...], owin_ref[...]).astype(BF16)


def combine(y, ocmp, osel, owin):
    m = ocmp.shape[0]
    full = pl.BlockSpec((m, NSA_WIDTH), lambda i: (0, 0))
    return pl.pallas_call(
        _combine_kernel,
        grid=(1,),
        in_specs=[pl.BlockSpec((m, LANE), lambda i: (0, COL_NGATE // LANE)),
                  pl.BlockSpec((LANE, 3 * NSA_WIDTH), lambda i: (0, 0)), full, full, full],
        out_specs=full,
        out_shape=jax.ShapeDtypeStruct((m, NSA_WIDTH), BF16),
        compiler_params=_cp("arbitrary"),
        name="nsa_combine",
    )(y, _gate_expand(), ocmp, osel, owin)


def _overlap(nc, n_sel, nselp):
    cs = np.arange(nc) * CMP_STRIDE
    ss = np.arange(nselp) * SEL_BLOCK
    ov = (cs[:, None] < ss[None, :] + SEL_BLOCK) & (cs[:, None] + CMP_LEN > ss[None, :])
    ov &= (np.arange(nc) < nc - 1)[:, None] & (np.arange(nselp) < n_sel)[None, :]
    return jnp.asarray(ov.astype(np.float32), BF16)


def _round_up(x, m):
    return -(-x // m) * m


def position_tables(rel_bias, t, p0, lwin):
    nc = (t if p0 == 0 else p0) // CMP_STRIDE
    ar = jnp.arange
    qpos = p0 + ar(t)
    kend = ar(nc) * CMP_STRIDE + CMP_LEN - 1
    tabs = {"cmp": _bias_table(rel_bias, qpos[:, None] - kend[None, :])}
    if p0 == 0:
        tabs["attn"] = prompt_attn_tables(rel_bias, min(t, 128))
    else:
        dn = ar(t)[:, None] - ar(t)[None, :]
        tabs["new"] = _bias_table(rel_bias, dn, dn >= 0)
        tabs["sel"] = _bias_table(rel_bias, qpos[:, None] - ar(p0)[None, :])
        dw = qpos[:, None] - (p0 - lwin + ar(lwin))[None, :]
        tabs["win"] = _bias_table(rel_bias, dw, (dw >= 0) & (dw < WINDOW))
    return tabs


def layer(x, past, lw, tabs, bsz, t):
    m = bsz * t
    prompt = past is None
    tm = min(512, m)
    y = rms_matmul(x, lw["norm_mix"], lw["w_in"], min(1024, m), 1408)

    c = math.gcd(t, HG_CHUNK)
    s0 = jnp.zeros((bsz, HG_HEADS, HG_DIM, HG_DIM), F32) if prompt else past["hgrn"]
    oa, s_hg = hgrn(y, lw["lb"], lw["hg_norm"], s0, bsz, t, c, min(t, 1024))

    ts = min(t, 256)
    if prompt:
        h0r = h0i = jnp.zeros((bsz, 1, S5_CH), F32)
    else:
        h0r, h0i = past["s5r"].reshape(bsz, 1, S5_CH), past["s5i"].reshape(bsz, 1, S5_CH)
    ob, s5r, s5i = s5(y, lw["s5"], h0r, h0i, bsz, t, ts)

    qn, rows, winrows, cv, selx, winx = nsa_prep(y, lw["qkn"], tm)
    if prompt:
        p0, nc = 0, t // CMP_STRIDE
        chunks = cv.reshape(m // CMP_STRIDE, CMP_STRIDE * 2 * KV_W)
        cacb = matmul(chunks, lw["cmp_wab"], min(512, chunks.shape[0])).reshape(bsz, nc, 1024)
    else:
        npages = past["npages"]
        pg = min(16, npages)
        p0 = npages * PAGE_SIZE
        nc = p0 // CMP_STRIDE
        cacb = cmp_paged(past["cache_t"], past["pt_flat"], lw["cmp_wab"], bsz, npages, min(32, npages))
    kcmp, vcmp = cmp_post(cacb, lw["cmp_bias"], lw["cmp_w2"], lw["qkn"][1], transposed=not prompt)
    n_sel = -(-(p0 + t) // SEL_BLOCK)
    nselp = _round_up(n_sel, LANE)
    tq = min(t, 128)
    nsr = _round_up(n_sel, 8) if prompt else 0
    ocmp, selm = cmpsel(qn, kcmp, vcmp, tabs["cmp"], _overlap(nc, n_sel, nselp), bsz, t, tq, p0, n_sel, nsr)
    if prompt:
        oc = prompt_attn(qn, selx, selm, ocmp, y, winx, tabs["attn"], bsz, t, tq, 4)
        lw_ = min(WINDOW, t)
        new_win = winrows.reshape(bsz, t, 2 * KV_W)[:, t - lw_:]
    else:
        osel = decode_attn(qn, past["cache_t"], tabs["sel"], rows, (2, 3), tabs["new"], bsz, t, pg * PAGE_SIZE,
                           selm=selm, paged=(past["pt_flat"], npages, pg))
        win = past["win"]
        owin = decode_attn(qn, win, tabs["win"], winrows, (0, 1), tabs["new"], bsz, t, win.shape[1])
        oc = combine(y, ocmp, osel, owin)
        new_win = jnp.concatenate([win, winrows.reshape(bsz, t, 2 * KV_W)], axis=1)[:, t:]

    x1 = merge(x, y, oa, ob, oc, lw["w_branch"], lw["w_out"], tm)
    x2 = ffn(x1, lw["norm_ffn"], lw["w_gate_up"], lw["w_down"], min(1024, m), 1408)
    return x2, (rows, new_win, s_hg, s5r, s5i)


def layer_weights(l, norm_mix, w_in, lower_bounds, hg_out_norm, s5_a_re, s5_a_im, s5_log_dt, s5_b_re, s5_b_im,
                  s5_c_re, s5_c_im, s5_d, s5_w_glu, nsa_qk_norm, cmp_pe, cmp_w1, cmp_w2, w_branch, w_out, norm_ffn,
                  w_gate_up, w_down):
    wab, cbias, w2blk = cmp_params(cmp_pe[l], cmp_w1[l], cmp_w2[l])
    s5p = s5_params(s5_a_re[l], s5_a_im[l], s5_log_dt[l], s5_b_re[l], s5_b_im[l], s5_c_re[l], s5_c_im[l],
                    s5_d[l], s5_w_glu[l], S5_SUB)
    return {"norm_mix": norm_mix[l].reshape(1, D_MODEL), "w_in": prep_w_in(w_in[l]),
            "lb": lower_bounds[l].reshape(1, HG_WIDTH), "hg_norm": hg_out_norm[l].reshape(1, HG_DIM),
            "s5": s5p, "qkn": nsa_qk_norm[l], "cmp_wab": wab, "cmp_bias": cbias, "cmp_w2": w2blk,
            "w_branch": w_branch[l].astype(BF16), "w_out": w_out[l].astype(BF16),
            "norm_ffn": norm_ffn[l].reshape(1, D_MODEL), "w_gate_up": w_gate_up[l].astype(BF16),
            "w_down": w_down[l].astype(BF16)}


def kernel(x_prompt, x_sample, cache_nsa_kv, cache_win_kv, state_hgrn, state_s5_re, state_s5_im, page_table,
           norm_mix, w_in, hg_lb_logits, hg_out_norm, s5_a_re, s5_a_im, s5_log_dt, s5_b_re, s5_b_im,
           s5_c_re, s5_c_im, s5_d, s5_w_glu, nsa_qk_norm, cmp_pe, cmp_w1, cmp_w2, rel_bias,
           w_branch, w_out, norm_ffn, w_gate_up, w_down):
    depth = w_in.shape[0]
    bp, tp, d = x_prompt.shape
    bs, tsm, _ = x_sample.shape
    n_phys = cache_nsa_kv.shape[1]
    npages = page_table.shape[1]
    lb_sm = jax.nn.softmax(hg_lb_logits.astype(F32), axis=0)
    lower_bounds = jnp.cumsum(lb_sm, axis=0) - lb_sm[0]
    hp = x_prompt.reshape(bp * tp, d)
    hs = x_sample.reshape(bs * tsm, d)
    cache_t = jnp.transpose(cache_nsa_kv, (0, 1, 3, 4, 5, 2)).reshape(depth * n_phys, 4 * KV_W, PAGE_SIZE)
    tabs_p = position_tables(rel_bias, tp, 0, 0)
    tabs_s = position_tables(rel_bias, tsm, npages * PAGE_SIZE, cache_win_kv.shape[2])
    st_p, st_s = [], []
    for l in range(depth):
        lw = layer_weights(l, norm_mix, w_in, lower_bounds, hg_out_norm, s5_a_re, s5_a_im,
                           s5_log_dt, s5_b_re, s5_b_im, s5_c_re, s5_c_im, s5_d, s5_w_glu, nsa_qk_norm, cmp_pe,
                           cmp_w1, cmp_w2, w_branch, w_out, norm_ffn, w_gate_up, w_down)
        past = {"cache_t": cache_t, "pt_flat": (page_table + l * n_phys).reshape(-1).astype(jnp.int32),
                "npages": npages, "win": cache_win_kv[l].reshape(bs, -1, 2 * KV_W), "hgrn": state_hgrn[l],
                "s5r": state_s5_re[l], "s5i": state_s5_im[l]}
        hp, sp = layer(hp, None, lw, tabs_p, bp, tp)
        hs, ss = layer(hs, past, lw, tabs_s, bs, tsm)
        st_p.append(sp)
        st_s.append(ss)

    def stack(states, k, shape):
        return jnp.stack([s[k].reshape(shape) for s in states])

    kvs = (4, NSA_KV, HEAD_DIM)
    return (hp.reshape(bp, tp, d), hs.reshape(bs, tsm, d),
            stack(st_p, 0, (bp, tp) + kvs), stack(st_s, 0, (bs, tsm) + kvs),
            stack(st_p, 1, (bp, -1, 2, NSA_KV, HEAD_DIM)), stack(st_s, 1, (bs, -1, 2, NSA_KV, HEAD_DIM)),
            stack(st_p, 2, (bp, HG_HEADS, HG_DIM, HG_DIM)), stack(st_s, 2, (bs, HG_HEADS, HG_DIM, HG_DIM)),
            stack(st_p, 3, (bp, S5_GROUPS, S5_STATE)), stack(st_p, 4, (bp, S5_GROUPS, S5_STATE)),
            stack(st_s, 3, (bs, S5_GROUPS, S5_STATE)), stack(st_s, 4, (bs, S5_GROUPS, S5_STATE)))
```

```python
import functools
import math

import numpy as np
import jax
import jax.numpy as jnp
from jax import lax
from jax.experimental import pallas as pl
from jax.experimental.pallas import tpu as pltpu

F32 = jnp.float32
BF16 = jnp.bfloat16

D_MODEL = 1024
HG_HEADS = 4
HG_DIM = 128
HG_WIDTH = HG_HEADS * HG_DIM
HG_CHUNK = 64
LB_FLOOR = 1e-30
S5_GROUP = 16
S5_GROUPS = 32
S5_WIDTH = S5_GROUP * S5_GROUPS
S5_STATE = 64
S5_CH = S5_GROUPS * S5_STATE
NSA_HEADS = 8
NSA_KV = 2
NSA_REP = NSA_HEADS // NSA_KV
HEAD_DIM = 64
NSA_WIDTH = NSA_HEADS * HEAD_DIM
KV_W = NSA_KV * HEAD_DIM
CMP_LEN = 32
CMP_STRIDE = 16
CMP_HIDDEN = 128
SEL_BLOCK = 64
SEL_TOPN = 16
WINDOW = 512
N_BUCKETS = 32
MAX_DIST = 128
PAGE_SIZE = 128
D_FF = 2816
NEG = -1e30
FORCE = 1e9
EPS = 1e-6

COL_GA, COL_GB, COL_GC = 0, 1024, 2048
COL_HQ, COL_HF, COL_HI, COL_HG = 3072, 3584, 4096, 4608
COL_SU, COL_NQ, COL_KV4, COL_WKV, COL_NGATE = 5120, 5632, 6144, 6656, 6912
N_IN_PAD = 7040
N_IN = 6936

LANE = 128
VMEM_LIMIT = 56 * 1024 * 1024


def _cp(*sem):
    return pltpu.CompilerParams(dimension_semantics=sem, vmem_limit_bytes=VMEM_LIMIT)


def _dot(a, b):
    return jnp.dot(a.astype(BF16), b.astype(BF16), preferred_element_type=F32)


def _dot_nt(a, b):
    return lax.dot_general(a.astype(BF16), b.astype(BF16), (((1,), (1,)), ((), ())), preferred_element_type=F32)


def _dot_tn(a, b):
    return lax.dot_general(a.astype(BF16), b.astype(BF16), (((0,), (0,)), ((), ())), preferred_element_type=F32)


def _split2(x):
    hi = x.astype(BF16)
    lo = (x - hi.astype(F32)).astype(BF16)
    return hi, lo


def _dot2(x, w):
    hi, lo = _split2(x)
    return jnp.dot(hi, w, preferred_element_type=F32) + jnp.dot(lo, w, preferred_element_type=F32)


def _silu(x):
    return x * jax.nn.sigmoid(x)


def _gelu_tanh(x):
    return 0.5 * x * (1.0 + jnp.tanh(math.sqrt(2.0 / math.pi) * (x + 0.044715 * (x * x * x))))


def _rms_mm_kernel(x_ref, g_ref, w_ref, o_ref, xn_ref):
    @pl.when(pl.program_id(1) == 0)
    def _():
        x = x_ref[...]
        ms = jnp.mean(x * x, axis=-1, keepdims=True)
        xn_ref[...] = (x * lax.rsqrt(ms + EPS) * g_ref[...]).astype(BF16)

    o_ref[...] = jnp.dot(xn_ref[...], w_ref[...], preferred_element_type=F32)


def rms_matmul(x, g, w, tm, tn):
    m, k = x.shape
    n = w.shape[1]
    return pl.pallas_call(
        _rms_mm_kernel,
        grid=(m // tm, n // tn),
        in_specs=[pl.BlockSpec((tm, k), lambda i, j: (i, 0)),
                  pl.BlockSpec((1, k), lambda i, j: (0, 0)),
                  pl.BlockSpec((k, tn), lambda i, j: (0, j))],
        out_specs=pl.BlockSpec((tm, tn), lambda i, j: (i, j)),
        out_shape=jax.ShapeDtypeStruct((m, n), F32),
        scratch_shapes=[pltpu.VMEM((tm, k), BF16)],
        compiler_params=_cp("parallel", "arbitrary"),
        name="rms_in_proj",
    )(x, g, w)


def _mm_kernel(x_ref, w_ref, o_ref):
    o_ref[...] = jnp.dot(x_ref[...], w_ref[...], preferred_element_type=F32)


def matmul(x, w, tm):
    m, k = x.shape
    n = w.shape[1]
    return pl.pallas_call(
        _mm_kernel,
        grid=(m // tm,),
        in_specs=[pl.BlockSpec((tm, k), lambda i: (i, 0)),
                  pl.BlockSpec((k, n), lambda i: (0, 0))],
        out_specs=pl.BlockSpec((tm, n), lambda i: (i, 0)),
        out_shape=jax.ShapeDtypeStruct((m, n), F32),
        compiler_params=_cp("parallel"),
        name="cmp_matmul",
    )(x, w)


def _merge_kernel(x_ref, ga_ref, gb_ref, gc_ref, oa_ref, ob_ref, oc_ref, wb_ref, wo_ref, o_ref):
    m = jax.nn.sigmoid(ga_ref[...]) * jnp.dot(oa_ref[...], wb_ref[0], preferred_element_type=F32)
    m = m + jax.nn.sigmoid(gb_ref[...]) * jnp.dot(ob_ref[...], wb_ref[1], preferred_element_type=F32)
    m = m + jax.nn.sigmoid(gc_ref[...]) * jnp.dot(oc_ref[...], wb_ref[2], preferred_element_type=F32)
    o_ref[...] = x_ref[...] + jnp.dot(m.astype(BF16), wo_ref[...], preferred_element_type=F32)


def merge(x, y, oa, ob, oc, wb, wo, tm):
    m, d = x.shape
    return pl.pallas_call(
        _merge_kernel,
        grid=(m // tm,),
        in_specs=[pl.BlockSpec((tm, d), lambda i: (i, 0)),
                  pl.BlockSpec((tm, d), lambda i: (i, COL_GA // D_MODEL)),
                  pl.BlockSpec((tm, d), lambda i: (i, COL_GB // D_MODEL)),
                  pl.BlockSpec((tm, d), lambda i: (i, COL_GC // D_MODEL)),
                  pl.BlockSpec((tm, 512), lambda i: (i, 0)),
                  pl.BlockSpec((tm, 512), lambda i: (i, 0)),
                  pl.BlockSpec((tm, 512), lambda i: (i, 0)),
                  pl.BlockSpec((3, 512, d), lambda i: (0, 0, 0)),
                  pl.BlockSpec((d, d), lambda i: (0, 0))],
        out_specs=pl.BlockSpec((tm, d), lambda i: (i, 0)),
        out_shape=jax.ShapeDtypeStruct((m, d), F32),
        compiler_params=_cp("parallel"),
        name="merge_out_proj",
    )(x, y, y, y, oa, ob, oc, wb, wo)


def _ffn_kernel(x_ref, g_ref, wg_ref, wu_ref, wd_ref, o_ref, xn_ref):
    @pl.when(pl.program_id(1) == 0)
    def _():
        x = x_ref[...]
        ms = jnp.mean(x * x, axis=-1, keepdims=True)
        xn_ref[...] = (x * lax.rsqrt(ms + EPS) * g_ref[...]).astype(BF16)
        o_ref[...] = x

    xn = xn_ref[...]
    gate = jnp.dot(xn, wg_ref[...], preferred_element_type=F32)
    up = jnp.dot(xn, wu_ref[...], preferred_element_type=F32)
    h = (_silu(gate) * up).astype(BF16)
    o_ref[...] += jnp.dot(h, wd_ref[...], preferred_element_type=F32)


def ffn(x, g, wgu, wd, tm, tf):
    m, d = x.shape
    nf = D_FF // tf
    return pl.pallas_call(
        _ffn_kernel,
        grid=(m // tm, nf),
        in_specs=[pl.BlockSpec((tm, d), lambda i, f: (i, 0)),
                  pl.BlockSpec((1, d), lambda i, f: (0, 0)),
                  pl.BlockSpec((d, tf), lambda i, f: (0, f)),
                  pl.BlockSpec((d, tf), lambda i, f: (0, nf + f)),
                  pl.BlockSpec((tf, d), lambda i, f: (f, 0))],
        out_specs=pl.BlockSpec((tm, d), lambda i, f: (i, 0)),
        out_shape=jax.ShapeDtypeStruct((m, d), F32),
        scratch_shapes=[pltpu.VMEM((tm, d), BF16)],
        compiler_params=_cp("parallel", "arbitrary"),
        name="swiglu_ffn",
    )(x, g, wgu, wgu, wd)


def _hgrn_levels(c):
    levels = [m for m in (32, 16, 8, 4, 2, 1) if m < c]
    return [m for m in levels if m < 8], [m for m in levels if m >= 8]


def _hgrn_consts(c):
    mats = [np.tril(np.ones((c, c), np.float32))]
    r = np.arange(c)
    for m in _hgrn_levels(c)[0]:
        pos = r % (2 * m)
        mid = (r // (2 * m)) * 2 * m + m
        up = np.zeros((c, c), np.float32)
        lo = np.zeros((c, c), np.float32)
        for t in range(c):
            if pos[t] >= m:
                up[t, mid[t]:t + 1] = 1.0
            else:
                lo[t, t + 1:mid[t]] = 1.0
        mats += [up, lo]
    return np.concatenate(mats, axis=0)


def _hgrn_kernel(hq_ref, hf_ref, hi_ref, hg_ref, lb_ref, gn_ref, s0_ref, mst_ref, o_ref, sout_ref, st_ref,
                 *, c, nchunk):
    tb = pl.program_id(2)

    @pl.when(tb == 0)
    def _():
        st_ref[...] = s0_ref[0, 0].T

    small, big = _hgrn_levels(c)
    rid = lax.broadcasted_iota(jnp.int32, (c, HG_DIM), 0)
    lb = lb_ref[...]
    log_lb = jnp.log(jnp.maximum(lb, LB_FLOOR))
    log_1m = jnp.log1p(-lb)
    one_m = 1.0 - lb
    gn = gn_ref[...]
    mst = mst_ref[...]
    ti = lax.broadcasted_iota(jnp.int32, (c, c), 0)
    si = lax.broadcasted_iota(jnp.int32, (c, c), 1)
    txs = jnp.bitwise_xor(ti, si)
    lower = ti > si

    for ch in range(nchunk):
        rows = pl.ds(ch * c, c)
        fp = hf_ref[rows, :]
        hq = hq_ref[rows, :]
        v = hi_ref[rows, :]
        hg = hg_ref[rows, :]
        log_sig = -(jnp.maximum(-fp, 0.0) + jnp.log1p(jnp.exp(-jnp.abs(fp))))
        b = log_1m + log_sig
        g = jnp.maximum(log_lb, b) + jnp.log1p(jnp.exp(-jnp.abs(log_lb - b)))
        kin = one_m * jax.nn.sigmoid(-fp)
        q = _silu(hq)

        g1 = g.astype(BF16)
        r1 = g - g1.astype(F32)
        g2 = r1.astype(BF16)
        g3 = (r1 - g2.astype(F32)).astype(BF16)
        gs = jnp.concatenate([g1, g2, g3], axis=1)
        rr = jnp.dot(mst, gs, preferred_element_type=F32)
        rr = rr[:, 0:LANE] + rr[:, LANE:2 * LANE] + rr[:, 2 * LANE:3 * LANE]
        gcum = rr[0:c]

        vb = v.astype(BF16)
        att = jnp.where(ti == si, _dot_nt(q, kin), 0.0)
        for li, m in enumerate(small + big):
            if m in small:
                dq = rr[(1 + 2 * li) * c:(2 + 2 * li) * c]
                ek = rr[(2 + 2 * li) * c:(3 + 2 * li) * c]
            else:
                gb = gcum[m - 1:m, :]
                for blk in range(1, c // (2 * m)):
                    gb = jnp.where(rid >= blk * 2 * m, gcum[blk * 2 * m + m - 1:blk * 2 * m + m, :], gb)
                dq = jnp.minimum(gcum - gb, 0.0)
                ek = jnp.minimum(gb - gcum, 0.0)
            pair = _dot_nt(q * jnp.exp(dq), kin * jnp.exp(ek))
            sel = lower & (jnp.right_shift(txs, int(math.log2(m))) == 1)
            att = jnp.where(sel, pair, att)
        st = st_ref[...]
        o = _dot(att, vb) + _dot_nt(q * jnp.exp(gcum), st)
        g_end = gcum[c - 1:c, :]
        kd = kin * jnp.exp(g_end - gcum)
        st_ref[...] = st * jnp.exp(g_end) + _dot_tn(vb, kd)

        ms = jnp.mean(o * o, axis=-1, keepdims=True)
        on = o * lax.rsqrt(ms + EPS) * gn
        o_ref[rows, :] = (on * _silu(hg)).astype(BF16)

    @pl.when(tb == pl.num_programs(2) - 1)
    def _():
        sout_ref[0, 0] = st_ref[...].T


def hgrn(y, lb, gn, s0, bsz, t, c, tbk):
    ntb = t // tbk
    mst = jnp.asarray(_hgrn_consts(c), BF16)

    def col(base):
        return pl.BlockSpec((tbk, HG_DIM), lambda b, h, k: (b * ntb + k, base // HG_DIM + h))

    return pl.pallas_call(
        functools.partial(_hgrn_kernel, c=c, nchunk=tbk // c),
        grid=(bsz, HG_HEADS, ntb),
        in_specs=[col(COL_HQ), col(COL_HF), col(COL_HI), col(COL_HG),
                  pl.BlockSpec((1, HG_DIM), lambda b, h, k: (0, h)),
                  pl.BlockSpec((1, HG_DIM), lambda b, h, k: (0, 0)),
                  pl.BlockSpec((1, 1, HG_DIM, HG_DIM), lambda b, h, k: (b, h, 0, 0)),
                  pl.BlockSpec(mst.shape, lambda b, h, k: (0, 0))],
        out_specs=[pl.BlockSpec((tbk, HG_DIM), lambda b, h, k: (b * ntb + k, h)),
                   pl.BlockSpec((1, 1, HG_DIM, HG_DIM), lambda b, h, k: (b, h, 0, 0))],
        out_shape=[jax.ShapeDtypeStruct((bsz * t, HG_WIDTH), BF16),
                   jax.ShapeDtypeStruct((bsz, HG_HEADS, HG_DIM, HG_DIM), F32)],
        scratch_shapes=[pltpu.VMEM((HG_DIM, HG_DIM), F32)],
        compiler_params=_cp("parallel", "parallel", "arbitrary"),
        name="hgrn2_scan",
    )(y, y, y, y, lb, gn, s0, mst)


S5_SUB = 8


def _shift_in_group(x, d):
    n, w = x.shape
    x3 = x.reshape(n // S5_SUB, S5_SUB, w)
    rid = lax.broadcasted_iota(jnp.int32, x3.shape, 1)
    return jnp.where(rid >= d, pltpu.roll(x3, d, axis=1), 0.0).reshape(n, w)


def _s5_kernel(u_ref, bre_ref, bim_ref, adr_ref, adi_ref, atr_ref, ati_ref, h0r_ref, h0i_ref,
               cre_ref, cim_ref, d_ref, wg_ref, o_ref, hr_out, hi_out, cr_ref, ci_ref, *, ts):
    k = pl.program_id(1)

    @pl.when(k == 0)
    def _():
        cr_ref[...] = h0r_ref[0]
        ci_ref[...] = h0i_ref[0]

    u = u_ref[...]
    ub = u.astype(BF16)
    xr = jnp.dot(ub, bre_ref[...], preferred_element_type=F32)
    xi = jnp.dot(ub, bim_ref[...], preferred_element_type=F32)
    lev = 0
    d = 1
    while d < S5_SUB:
        ar = adr_ref[lev:lev + 1, :]
        ai = adi_ref[lev:lev + 1, :]
        sr = _shift_in_group(xr, d)
        si = _shift_in_group(xi, d)
        xr, xi = xr + ar * sr - ai * si, xi + ar * si + ai * sr
        d *= 2
        lev += 1
    cr = cr_ref[...]
    ci = ci_ref[...]
    atr = atr_ref[...]
    ati = ati_ref[...]
    hrs, his = [], []
    for r in range(ts // S5_SUB):
        gr = xr[r * S5_SUB:(r + 1) * S5_SUB]
        gi = xi[r * S5_SUB:(r + 1) * S5_SUB]
        gr, gi = gr + atr * cr - ati * ci, gi + atr * ci + ati * cr
        hrs.append(gr)
        his.append(gi)
        cr, ci = gr[S5_SUB - 1:S5_SUB], gi[S5_SUB - 1:S5_SUB]
    hr = jnp.concatenate(hrs, axis=0)
    hi = jnp.concatenate(his, axis=0)
    cr_ref[...] = cr
    ci_ref[...] = ci
    y = jnp.dot(hr.astype(BF16), cre_ref[...], preferred_element_type=F32) \
        - jnp.dot(hi.astype(BF16), cim_ref[...], preferred_element_type=F32)
    y = _gelu_tanh(y + d_ref[...] * u)
    o_ref[...] = (y * jax.nn.sigmoid(jnp.dot(y.astype(BF16), wg_ref[...], preferred_element_type=F32))).astype(BF16)

    @pl.when(k == pl.num_programs(1) - 1)
    def _():
        hr_out[0] = cr
        hi_out[0] = ci


def s5(y, prm, h0r, h0i, bsz, t, ts):
    nts = t // ts
    nlev = prm["adr"].shape[0]
    const = lambda shape: pl.BlockSpec(shape, lambda b, k: tuple(0 for _ in shape))
    state = pl.BlockSpec((1, 1, S5_CH), lambda b, k: (b, 0, 0))
    return pl.pallas_call(
        functools.partial(_s5_kernel, ts=ts),
        grid=(bsz, nts),
        in_specs=[pl.BlockSpec((ts, S5_WIDTH), lambda b, k: (b * nts + k, COL_SU // S5_WIDTH)),
                  const((S5_WIDTH, S5_CH)), const((S5_WIDTH, S5_CH)),
                  const((nlev, S5_CH)), const((nlev, S5_CH)),
                  const((S5_SUB, S5_CH)), const((S5_SUB, S5_CH)),
                  state, state,
                  const((S5_CH, S5_WIDTH)), const((S5_CH, S5_WIDTH)),
                  const((1, S5_WIDTH)), const((S5_WIDTH, S5_WIDTH))],
        out_specs=[pl.BlockSpec((ts, S5_WIDTH), lambda b, k: (b * nts + k, 0)), state, state],
        out_shape=[jax.ShapeDtypeStruct((bsz * t, S5_WIDTH), BF16),
                   jax.ShapeDtypeStruct((bsz, 1, S5_CH), F32),
                   jax.ShapeDtypeStruct((bsz, 1, S5_CH), F32)],
        scratch_shapes=[pltpu.VMEM((1, S5_CH), F32), pltpu.VMEM((1, S5_CH), F32)],
        compiler_params=_cp("parallel", "arbitrary"),
        name="s5_scan",
    )(y, prm["bre"], prm["bim"], prm["adr"], prm["adi"], prm["atr"], prm["ati"], h0r, h0i,
      prm["cre"], prm["cim"], prm["d"], prm["wglu"])


def _cmul(ar, ai, br, bi):
    return ar * br - ai * bi, ar * bi + ai * br


def s5_params(a_re, a_im, log_dt, b_re, b_im, c_re, c_im, d, w_glu, ts):
    step = jnp.exp(log_dt)[:, None]
    mag = jnp.exp(a_re * step)
    ab_re, ab_im = mag * jnp.cos(a_im * step), mag * jnp.sin(a_im * step)
    den = a_re * a_re + a_im * a_im
    z_re = ((ab_re - 1.0) * a_re + ab_im * a_im) / den
    z_im = (ab_im * a_re - (ab_re - 1.0) * a_im) / den
    bb_re = z_re[..., None] * b_re - z_im[..., None] * b_im
    bb_im = z_re[..., None] * b_im + z_im[..., None] * b_re
    eye = jnp.eye(S5_GROUPS, dtype=F32)
    bd_in = lambda w: jnp.einsum("gpn,gh->gnhp", w, eye).reshape(S5_WIDTH, S5_CH).astype(BF16)
    bd_out = lambda w: jnp.einsum("gnp,gh->gphn", w, eye).reshape(S5_CH, S5_WIDTH).astype(BF16)
    ar, ai = ab_re.reshape(1, S5_CH), ab_im.reshape(1, S5_CH)
    adr, adi = [ar], [ai]
    d2 = 2
    while d2 < ts:
        r, i = _cmul(adr[-1], adi[-1], adr[-1], adi[-1])
        adr.append(r)
        adi.append(i)
        d2 *= 2
    atr, ati = ar, ai
    n = 1
    lev = 0
    while n < ts:
        pr, pi = _cmul(atr, ati, adr[lev], adi[lev])
        atr, ati = jnp.concatenate([atr, pr], 0), jnp.concatenate([ati, pi], 0)
        n *= 2
        lev += 1
    return {"bre": bd_in(bb_re), "bim": bd_in(bb_im), "cre": bd_out(c_re), "cim": bd_out(c_im),
            "adr": jnp.concatenate(adr, 0), "adi": jnp.concatenate(adi, 0), "atr": atr, "ati": ati,
            "d": d.reshape(1, S5_WIDTH), "wglu": w_glu.astype(BF16)}


def prep_w_in(w):
    pad = jnp.zeros((w.shape[0], N_IN_PAD - N_IN), w.dtype)
    return jnp.concatenate([w[:, 3864:6936], w[:, 0:3840], w[:, 3840:3864], pad], axis=1).astype(BF16)


def _seg_rms(x, ones, w):
    ss = _dot2(x * x, ones)
    return x * lax.rsqrt(ss * (1.0 / HEAD_DIM) + EPS) * w


def _kv_operands(k, v):
    tm = k.shape[0]
    zero, one = jnp.zeros((tm, HEAD_DIM), F32), jnp.ones((tm, HEAD_DIM), F32)
    parts = []
    for g in range(NSA_KV):
        parts += [k[:, g * HEAD_DIM:(g + 1) * HEAD_DIM], zero]
    for g in range(NSA_KV):
        parts += [v[:, g * HEAD_DIM:(g + 1) * HEAD_DIM], one]
    return jnp.concatenate(parts, axis=1).astype(BF16)


def _nsa_prep_kernel(nq_ref, kv_ref, wk_ref, wq_ref, wks_ref, wkw_ref, ones_ref,
                     qn_ref, rows_ref, win_ref, cv_ref, selx_ref, winx_ref):
    qn_ref[...] = _seg_rms(nq_ref[...], ones_ref[...], wq_ref[...]).astype(BF16)
    ones1 = ones_ref[0:LANE, 0:LANE]
    kv = kv_ref[...]
    ksn = _seg_rms(kv[:, 2 * KV_W:3 * KV_W], ones1, wks_ref[...])
    rows_ref[...] = jnp.concatenate([kv[:, 0:2 * KV_W], ksn, kv[:, 3 * KV_W:4 * KV_W]], axis=1)
    cv_ref[...] = kv[:, 0:2 * KV_W].astype(BF16)
    selx_ref[...] = _kv_operands(ksn, kv[:, 3 * KV_W:4 * KV_W])
    wk = wk_ref[...]
    kwn = _seg_rms(wk[:, 0:KV_W], ones1, wkw_ref[...])
    win_ref[...] = jnp.concatenate([kwn, wk[:, KV_W:2 * KV_W]], axis=1)
    winx_ref[...] = _kv_operands(kwn, wk[:, KV_W:2 * KV_W])


def _block_ones(n, blk):
    r = np.arange(n) // blk
    return jnp.asarray((r[:, None] == r[None, :]).astype(np.float32), BF16)


def nsa_prep(y, qkn, tm):
    m = y.shape[0]
    wq = (jnp.tile(qkn[0], NSA_HEADS) * HEAD_DIM ** -0.5).reshape(1, NSA_WIDTH)
    wks = jnp.tile(qkn[2], NSA_KV).reshape(1, KV_W)
    wkw = jnp.tile(qkn[3], NSA_KV).reshape(1, KV_W)
    ones = _block_ones(NSA_WIDTH, HEAD_DIM)
    const = lambda shape: pl.BlockSpec(shape, lambda i: (0, 0))
    row = lambda w: pl.BlockSpec((tm, w), lambda i: (i, 0))
    return pl.pallas_call(
        _nsa_prep_kernel,
        grid=(m // tm,),
        in_specs=[pl.BlockSpec((tm, 512), lambda i: (i, COL_NQ // 512)),
                  pl.BlockSpec((tm, 512), lambda i: (i, COL_KV4 // 512)),
                  pl.BlockSpec((tm, 256), lambda i: (i, COL_WKV // 256)),
                  const((1, NSA_WIDTH)), const((1, KV_W)), const((1, KV_W)), const((NSA_WIDTH, NSA_WIDTH))],
        out_specs=[row(512), row(512), row(256), row(256), row(512), row(512)],
        out_shape=[jax.ShapeDtypeStruct((m, NSA_WIDTH), BF16), jax.ShapeDtypeStruct((m, 4 * KV_W), F32),
                   jax.ShapeDtypeStruct((m, 2 * KV_W), F32), jax.ShapeDtypeStruct((m, 2 * KV_W), BF16),
                   jax.ShapeDtypeStruct((m, 4 * LANE), BF16), jax.ShapeDtypeStruct((m, 4 * LANE), BF16)],
        compiler_params=_cp("parallel"),
        name="nsa_prep",
    )(y, y, y, wq, wks, wkw, ones)


def _cmp_post_kernel(c_ref, cb_ref, w2_ref, wk_ref, ones_ref, k_ref, v_ref):
    c = c_ref[0]
    nc = c.shape[0]
    ca, cb = c[:, 0:512], c[:, 512:1024]
    rid = lax.broadcasted_iota(jnp.int32, cb.shape, 0)
    cb_next = jnp.where(rid < nc - 1, pltpu.roll(cb, nc - 1, axis=0), 0.0)
    hid = _gelu_tanh(ca + cb_next + cb_ref[...])
    out = _dot(hid, w2_ref[...])
    k_ref[0] = _seg_rms(out[:, 0:KV_W], ones_ref[...], wk_ref[...]).T.astype(BF16)
    v_ref[0] = out[:, KV_W:2 * KV_W].T.astype(BF16)


def cmp_post(cacb, cbias, w2blk, qkn1):
    bsz, nc, _ = cacb.shape
    wk = jnp.tile(qkn1, NSA_KV).reshape(1, KV_W)
    ones = _block_ones(KV_W, HEAD_DIM)
    const = lambda a: pl.BlockSpec(a.shape, lambda b: (0, 0))
    out = pl.BlockSpec((1, KV_W, nc), lambda b: (b, 0, 0))
    return pl.pallas_call(
        _cmp_post_kernel,
        grid=(bsz,),
        in_specs=[pl.BlockSpec((1,) + cacb.shape[1:], lambda b: (b, 0, 0)),
                  const(cbias), const(w2blk), const(wk), const(ones)],
        out_specs=[out, out],
        out_shape=[jax.ShapeDtypeStruct((bsz, KV_W, nc), BF16)] * 2,
        compiler_params=_cp("parallel"),
        name="cmp_post",
    )(cacb, cbias, w2blk, wk, ones)


def _cmp_paged_kernel(pt_ref, *refs, pg):
    pages = refs[:pg]
    wt_ref, perm_ref, o_ref, tok_ref = refs[pg:]
    cpp = PAGE_SIZE // CMP_STRIDE
    for k in range(pg):
        tok = _dot_nt(perm_ref[...], pages[k][0])
        for j in range(CMP_STRIDE):
            tok_ref[j, k * cpp:(k + 1) * cpp, :] = tok[j * cpp:(j + 1) * cpp, :]
    nch = pg * cpp
    nblk = 2 * KV_W // HEAD_DIM
    accs = [jnp.zeros((nch, 2 * CMP_HIDDEN), F32) for _ in range(nblk)]
    for j in range(CMP_STRIDE):
        rj = tok_ref[j]
        for c in range(nblk):
            accs[c] = accs[c] + _dot(rj[:, c * HEAD_DIM:(c + 1) * HEAD_DIM], wt_ref[j, c])
    o_ref[0] = jnp.concatenate([accs[c][:, half * CMP_HIDDEN:(half + 1) * CMP_HIDDEN]
                                for half in range(2) for c in range(nblk)], axis=1)


def cmp_paged(cache_t, pt_flat, wab, bsz, npages, pg):
    nch = pg * PAGE_SIZE // CMP_STRIDE
    nc = npages * PAGE_SIZE // CMP_STRIDE
    nblk = 2 * KV_W // HEAD_DIM
    w6 = wab.reshape(CMP_STRIDE, nblk, HEAD_DIM, 2, nblk, CMP_HIDDEN)
    wt = jnp.stack([w6[:, c, :, :, c, :] for c in range(nblk)], axis=1).reshape(CMP_STRIDE, nblk, HEAD_DIM, 2 * CMP_HIDDEN)
    cpp = PAGE_SIZE // CMP_STRIDE
    r = np.arange(PAGE_SIZE)
    src = (r % cpp) * CMP_STRIDE + r // cpp
    perm = jnp.asarray((src[:, None] == r[None, :]).astype(np.float32), BF16)

    def page_spec(k):
        return pl.BlockSpec((1, 2 * KV_W, PAGE_SIZE), lambda b, p, pt: (pt[b * npages + p * pg + k], 0, 0))

    return pl.pallas_call(
        functools.partial(_cmp_paged_kernel, pg=pg),
        grid_spec=pltpu.PrefetchScalarGridSpec(
            num_scalar_prefetch=1, grid=(bsz, npages // pg),
            in_specs=[page_spec(k) for k in range(pg)] +
                     [pl.BlockSpec(wt.shape, lambda b, p, pt: (0, 0, 0, 0)), pl.BlockSpec(perm.shape, lambda b, p, pt: (0, 0))],
            out_specs=pl.BlockSpec((1, nch, 1024), lambda b, p, pt: (b, p, 0)),
            scratch_shapes=[pltpu.VMEM((CMP_STRIDE, nch, 2 * KV_W), F32)]),
        out_shape=jax.ShapeDtypeStruct((bsz, nc, 1024), F32),
        compiler_params=_cp("parallel", "arbitrary"),
        name="cmp_paged",
    )(pt_flat, *([cache_t] * pg), wt, perm)


def cmp_params(pe, w1, w2):
    w1r = w1.reshape(2, 2, 16, HEAD_DIM, CMP_HIDDEN)
    eye2 = jnp.eye(2, dtype=F32)
    wab = jnp.einsum("khjdc,kl,gm->jkgdhlmc", w1r, eye2, eye2).reshape(16 * 256, 1024).astype(BF16)
    cb = jnp.einsum("kf,kfc->kc", pe.reshape(2, CMP_LEN * HEAD_DIM), w1)
    cbias = jnp.broadcast_to(cb[:, None, :], (2, NSA_KV, CMP_HIDDEN)).reshape(1, 512)
    w2blk = jnp.einsum("kcd,kl,gm->kgclmd", w2, eye2, eye2).reshape(512, 256).astype(BF16)
    return wab, cbias, w2blk


def _cmpsel_kernel(qn_ref, kc_ref, vc_ref, bias_ref, ov_ref, ocmp_ref, selm_ref, *, tq, p0, n_sel, nselp, nsr):
    i = pl.program_id(0)
    q = qn_ref[...]
    kc = kc_ref[0]
    vc = vc_ref[0]
    nc = kc.shape[1]
    qpos = p0 + i * tq + lax.broadcasted_iota(jnp.int32, (tq, 1), 0)
    kend = lax.broadcasted_iota(jnp.int32, (1, nc), 1) * CMP_STRIDE + (CMP_LEN - 1)
    valid = qpos >= kend
    row_valid = (qpos >= CMP_LEN - 1).astype(F32)
    imp = [None, None]
    outs = []
    for h in range(NSA_HEADS):
        g = h // NSA_REP
        s = _dot(q[:, h * HEAD_DIM:(h + 1) * HEAD_DIM], kc[g * HEAD_DIM:(g + 1) * HEAD_DIM, :]) + bias_ref[h]
        s = jnp.where(valid, s, NEG)
        e = jnp.exp(s - jnp.max(s, axis=-1, keepdims=True))
        p = e / jnp.sum(e, axis=-1, keepdims=True) * row_valid
        outs.append(_dot_nt(p, vc[g * HEAD_DIM:(g + 1) * HEAD_DIM, :]))
        imp[g] = p if imp[g] is None else imp[g] + p
    ocmp_ref[...] = jnp.concatenate(outs, axis=1)

    jid = lax.broadcasted_iota(jnp.int32, (1, nselp), 1)
    cur = jnp.right_shift(qpos, int(math.log2(SEL_BLOCK)))
    forced = (jid == 0) | (jid == cur) | (jid == cur - 1)
    future = jid * SEL_BLOCK > qpos
    topn = min(SEL_TOPN, n_sel)
    for g in range(NSA_KV):
        score = _dot2(imp[g], ov_ref[...])
        score = jnp.where(forced, FORCE, jnp.where(future, -FORCE, score))
        if nsr:
            st = score.T[0:nsr]
            rid = lax.broadcasted_iota(jnp.int32, (nsr, tq), 0)
            cnt = jnp.zeros((nsr, tq), F32)
            for c in range(n_sel):
                tie = jnp.where(rid > c, 1.0, 0.0)
                row = st[c:c + 1, :]
                cnt = cnt + jnp.where(row > st, 1.0, jnp.where(row == st, tie, 0.0))
            sel = jnp.where(rid < n_sel, jnp.where(cnt < topn, 1.0, 0.0), 0.0)
            sel = jnp.concatenate([sel, jnp.zeros((nselp - nsr, tq), F32)], axis=0).T
            selm_ref[:, g * nselp:(g + 1) * nselp] = sel.astype(BF16)
        else:
            cnt = jnp.zeros((tq, nselp), F32)
            for c in range(n_sel):
                col = score[:, c:c + 1]
                tie = (jid > c).astype(F32)
                cnt = cnt + jnp.where(col > score, 1.0, jnp.where(col == score, tie, 0.0))
            sel = jnp.where(jid < n_sel, jnp.where(cnt < topn, 1.0, 0.0), 0.0)
            selm_ref[:, g * nselp:(g + 1) * nselp] = sel.astype(BF16)


def cmpsel(qn, kcmp, vcmp, biasc, overlap, bsz, t, tq, p0, n_sel, nsr=0):
    nt = t // tq
    nc = kcmp.shape[2]
    nselp = overlap.shape[1]
    assert CMP_STRIDE * (nc - 1) + CMP_LEN - 1 > p0 + t - 1
    assert not nsr or (tq == LANE and nselp == LANE)
    sel_spec = pl.BlockSpec((tq, NSA_KV * nselp), lambda i, b: (b * nt + i, 0))
    sel_shape = jax.ShapeDtypeStruct((bsz * t, NSA_KV * nselp), BF16)
    return pl.pallas_call(
        functools.partial(_cmpsel_kernel, tq=tq, p0=p0, n_sel=n_sel, nselp=nselp, nsr=nsr),
        grid=(nt, bsz),
        in_specs=[pl.BlockSpec((tq, NSA_WIDTH), lambda i, b: (b * nt + i, 0)),
                  pl.BlockSpec((1, KV_W, nc), lambda i, b: (b, 0, 0)),
                  pl.BlockSpec((1, KV_W, nc), lambda i, b: (b, 0, 0)),
                  pl.BlockSpec((NSA_HEADS, tq, nc), lambda i, b: (0, i, 0)),
                  pl.BlockSpec((nc, nselp), lambda i, b: (0, 0))],
        out_specs=[pl.BlockSpec((tq, NSA_WIDTH), lambda i, b: (b * nt + i, 0)), sel_spec],
        out_shape=[jax.ShapeDtypeStruct((bsz * t, NSA_WIDTH), F32), sel_shape],
        compiler_params=_cp("arbitrary", "arbitrary"),
        name="cmp_attn_select",
    )(qn, kcmp, vcmp, biasc, overlap)


def _stack_heads(q, g):
    return jnp.concatenate([q[:, (NSA_REP * g + r) * HEAD_DIM:(NSA_REP * g + r + 1) * HEAD_DIM]
                            for r in range(NSA_REP)], axis=0)


def _unstack_heads(o_groups, tq):
    return jnp.concatenate([o[r * tq:(r + 1) * tq] for o in o_groups for r in range(NSA_REP)], axis=1)


def _online_update(carry, s, v, v_transposed=False):
    m, l, acc = carry
    m_new = jnp.maximum(m, jnp.max(s, axis=-1, keepdims=True))
    a = jnp.exp(m - m_new)
    p = jnp.exp(s - m_new)
    pv = _dot_nt(p, v) if v_transposed else _dot(p, v)
    return m_new, a * l + jnp.sum(p, axis=-1, keepdims=True), a * acc + pv


def _gated_sum(ng, ge, ocmp, osel, owin):
    gx = _dot2(jax.nn.sigmoid(ng), ge)
    return gx[:, 0:512] * ocmp + gx[:, 512:1024] * osel + gx[:, 1024:1536] * owin


def _online_update_fused(carry, s, vx):
    m, acc = carry
    m_new = jnp.maximum(m, jnp.max(s, axis=-1, keepdims=True))
    p = jnp.exp(s - m_new)
    return m_new, jnp.exp(m - m_new) * acc + jnp.dot(p.astype(BF16), vx, preferred_element_type=F32)


def _finish_fused(carry):
    acc = carry[1]
    return acc[:, 0:HEAD_DIM] / acc[:, HEAD_DIM:HEAD_DIM + 1]


def _prompt_attn_kernel(qn_ref, sx_ref, selm_ref, e_ref, bnear_ref, wx_ref, bw_ref,
                        ocmp_ref, ng_ref, ge_ref, oc_ref, madd_ref, *, tq, nselp, nprev, kt):
    i = pl.program_id(1)
    q = qn_ref[...]
    rows = NSA_REP * tq
    n_far = jnp.maximum(i - 1, 0)
    n_macro = n_far // kt
    init1 = (jnp.full((rows, 1), -jnp.inf, F32), jnp.zeros((rows, LANE), F32))
    init = (init1,) * NSA_KV
    heads = [slice(NSA_REP * g, NSA_REP * (g + 1)) for g in range(NSA_KV)]
    kcol = [slice(g * LANE, (g + 1) * LANE) for g in range(NSA_KV)]
    vcol = [slice((NSA_KV + g) * LANE, (NSA_KV + g + 1) * LANE) for g in range(NSA_KV)]
    zpad = jnp.zeros((rows, LANE - HEAD_DIM), BF16)
    qgs = [jnp.concatenate([_stack_heads(q, g), zpad], axis=1) for g in range(NSA_KV)]
    for g in range(NSA_KV):
        hit = jnp.dot(selm_ref[:, g * nselp:(g + 1) * nselp], e_ref[...], preferred_element_type=F32)
        madd_ref[g] = (hit - 1.0) * (-NEG)

    def step(r0, width, carries, near):
        out = []
        for g in range(NSA_KV):
            add = madd_ref[g, :, pl.ds(r0, width)][None]
            if near is not None:
                add = add + bnear_ref[heads[g], :, near]
            s = _dot_nt(qgs[g], sx_ref[pl.ds(r0, width), kcol[g]]).reshape(NSA_REP, tq, width) + add
            out.append(_online_update_fused(carries[g], s.reshape(rows, width), sx_ref[pl.ds(r0, width), vcol[g]]))
        return tuple(out)

    def far(j, ntiles, c):
        return step(pl.multiple_of(j * tq, tq), ntiles * tq, c, None)

    carries = lax.fori_loop(0, n_macro, lambda j, c: far(j * kt, kt, c), init)
    done = n_macro * kt
    w = kt // 2
    while w >= 1:
        take = jnp.bitwise_and(n_far - done, w) != 0
        carries = lax.cond(take, functools.partial(far, done, w), lambda c: c, carries)
        done = done + jnp.where(take, w, 0)
        w //= 2
    carries = lax.cond(
        i >= 1,
        lambda c: step(pl.multiple_of((i - 1) * tq, tq), 2 * tq, c, slice(0, 2 * tq)),
        lambda c: step(0, tq, c, slice(tq, 2 * tq)),
        carries)
    o_sel = [_finish_fused(c) for c in carries]

    o_win = []
    nk = (nprev + 1) * tq
    r0 = pl.multiple_of(i * tq, tq)
    kval = lax.broadcasted_iota(jnp.int32, (1, 1, nk), 2) >= (nprev - i) * tq
    for g in range(NSA_KV):
        s = _dot_nt(qgs[g], wx_ref[0, pl.ds(r0, nk), kcol[g]]).reshape(NSA_REP, tq, nk) + bw_ref[heads[g]]
        s = jnp.where(kval, s, NEG).reshape(rows, nk)
        o_win.append(_finish_fused(_online_update_fused(init1, s, wx_ref[0, pl.ds(r0, nk), vcol[g]])))

    oc = _gated_sum(ng_ref[...], ge_ref[...], ocmp_ref[...], _unstack_heads(o_sel, tq), _unstack_heads(o_win, tq))
    oc_ref[...] = oc.astype(BF16)


def _rel_bucket(dist):
    n = jnp.maximum(dist, 0)
    exact = N_BUCKETS // 2
    nf = jnp.maximum(n, 1).astype(F32)
    large = exact + (jnp.log(nf / exact) / math.log(MAX_DIST / exact) * (N_BUCKETS - exact)).astype(jnp.int32)
    return jnp.where(n < exact, n, jnp.minimum(large, N_BUCKETS - 1))


def _bias_last(rel_bias, dist, valid=None):
    onehot = (_rel_bucket(dist)[..., None] == jnp.arange(N_BUCKETS)).astype(F32)
    b = jnp.einsum("...k,kh->...h", onehot, rel_bias.astype(F32), precision=lax.Precision.HIGHEST)
    if valid is not None:
        b = jnp.where(valid[..., None], b, NEG)
    return b


def _bias_table(rel_bias, dist, valid=None):
    return jnp.moveaxis(_bias_last(rel_bias, dist, valid), -1, 0)


def _bias_table_t(rel_bias, dist, valid=None):
    b = jnp.swapaxes(_bias_last(rel_bias, dist, valid), -1, -2)
    return b.reshape(b.shape[:-2] + (b.shape[-2] * b.shape[-1],))


def _gate_expand():
    ge = np.zeros((LANE, 3 * NSA_WIDTH), np.float32)
    for br in range(3):
        for h in range(NSA_HEADS):
            ge[br * NSA_HEADS + h, br * NSA_WIDTH + h * HEAD_DIM: br * NSA_WIDTH + (h + 1) * HEAD_DIM] = 1.0
    return jnp.asarray(ge, BF16)


def _block_expand(nselp, length):
    e = (np.arange(nselp)[:, None] == (np.arange(length) // SEL_BLOCK)[None, :]).astype(np.float32)
    return jnp.asarray(e, BF16)


def prompt_attn_tables(rel_bias, tq):
    nprev = WINDOW // tq
    nk = (nprev + 1) * tq
    ar = jnp.arange
    dn = tq + ar(tq)[:, None] - ar(2 * tq)[None, :]
    dw = ar(tq)[:, None] + nprev * tq - ar(nk)[None, :]
    assert tq + 1 >= MAX_DIST
    far = rel_bias[N_BUCKETS - 1].astype(F32)[:, None, None]
    return {"near": _bias_table(rel_bias, dn, dn >= 0) - far,
            "win": _bias_table(rel_bias, dw, (dw >= 0) & (dw < WINDOW))}


def prompt_attn(qn, selx, selm, ocmp, y, winx, tabs, bsz, t, tq, kt):
    nt = t // tq
    nselp = selm.shape[1] // NSA_KV
    nprev = WINDOW // tq
    nk = (nprev + 1) * tq
    winp = jnp.pad(winx.reshape(bsz, t, 4 * LANE), ((0, 0), (nprev * tq, 0), (0, 0)))
    const = lambda shape: pl.BlockSpec(shape, lambda b, i: tuple(0 for _ in shape))
    tile = lambda w, c: pl.BlockSpec((tq, w), lambda b, i: (b * nt + i, c))
    return pl.pallas_call(
        functools.partial(_prompt_attn_kernel, tq=tq, nselp=nselp, nprev=nprev, kt=kt),
        grid=(bsz, nt),
        in_specs=[tile(NSA_WIDTH, 0),
                  pl.BlockSpec((t, 4 * LANE), lambda b, i: (b, 0)),
                  tile(NSA_KV * nselp, 0),
                  const((nselp, t)), const((NSA_HEADS, tq, 2 * tq)),
                  pl.BlockSpec((1, t + nprev * tq, 4 * LANE), lambda b, i: (b, 0, 0)),
                  const((NSA_HEADS, tq, nk)),
                  tile(NSA_WIDTH, 0),
                  tile(LANE, COL_NGATE // LANE),
                  const((LANE, 3 * NSA_WIDTH))],
        out_specs=tile(NSA_WIDTH, 0),
        out_shape=jax.ShapeDtypeStruct((bsz * t, NSA_WIDTH), BF16),
        scratch_shapes=[pltpu.VMEM((NSA_KV, tq, t), F32)],
        compiler_params=_cp("parallel", "arbitrary"),
        name="prompt_sel_win_attn",
    )(qn, selx, selm, _block_expand(nselp, t), tabs["near"], winp, tabs["win"], ocmp, y, _gate_expand())


def _decode_attn_kernel(*refs, t, nselp, masked, pg):
    refs = list(refs[1:] if pg else refs)
    qn_ref = refs.pop(0)
    past = [refs.pop(0) for _ in range(pg if pg else 2)]
    bp_ref, kn_ref, vn_ref, bn_ref = (refs.pop(0) for _ in range(4))
    if masked:
        selm_ref, e_ref = refs.pop(0), refs.pop(0)
    o_ref, m_ref, l_ref, acc_ref = refs
    if pg:
        j, bi, last = pl.program_id(0), pl.program_id(1), pl.num_programs(0) - 1
    else:
        j, bi, last = pl.program_id(1), 0, pl.num_programs(1) - 1
    rows = NSA_REP * t

    @pl.when(j == 0)
    def _():
        m_ref[bi] = jnp.full(m_ref.shape[1:], -jnp.inf, F32)
        l_ref[bi] = jnp.zeros(l_ref.shape[1:], F32)
        acc_ref[bi] = jnp.zeros(acc_ref.shape[1:], F32)

    q = qn_ref[...]
    tk = bp_ref.shape[2]
    state = [(m_ref[bi, g], l_ref[bi, g], acc_ref[bi, g]) for g in range(NSA_KV)]
    for g in range(NSA_KV):
        hs = slice(NSA_REP * g, NSA_REP * (g + 1))
        cs = slice(g * HEAD_DIM, (g + 1) * HEAD_DIM)
        qg = _stack_heads(q, g)
        if pg:
            kt = jnp.concatenate([p[0, g * HEAD_DIM:(g + 1) * HEAD_DIM, :] for p in past], axis=1)
            vt = jnp.concatenate([p[0, KV_W + g * HEAD_DIM:KV_W + (g + 1) * HEAD_DIM, :] for p in past], axis=1)
            s = _dot(qg, kt)
        else:
            s = _dot_nt(qg, past[0][0][:, cs])
        s = s.reshape(NSA_REP, t, tk) + bp_ref[hs]
        if masked:
            msk = jnp.dot(selm_ref[0, 0, g], e_ref[...], preferred_element_type=F32) > 0.5
            s = jnp.where(msk[None], s, NEG)
        if pg:
            state[g] = _online_update(state[g], s.reshape(rows, tk), vt, v_transposed=True)
        else:
            state[g] = _online_update(state[g], s.reshape(rows, tk), past[1][0][:, cs])
    for g in range(NSA_KV):
        m_ref[bi, g], l_ref[bi, g], acc_ref[bi, g] = state[g]

    @pl.when(j < last)
    def _():
        o_ref[0] = jnp.zeros(o_ref.shape[1:], F32)

    @pl.when(j == last)
    def _():
        outs = []
        for g in range(NSA_KV):
            hs = slice(NSA_REP * g, NSA_REP * (g + 1))
            cs = slice(g * HEAD_DIM, (g + 1) * HEAD_DIM)
            qg = _stack_heads(q, g)
            s = _dot_nt(qg, kn_ref[:, cs]).reshape(NSA_REP, t, t) + bn_ref[hs]
            _, l, acc = _online_update((m_ref[bi, g], l_ref[bi, g], acc_ref[bi, g]), s.reshape(rows, t), vn_ref[:, cs])
            outs.append(acc / l)
        o_ref[0] = _unstack_heads(outs, t)


def decode_attn(qn, kv_past, bias_past, new_rows, new_cols, bias_new, bsz, t, tk, selm=None, paged=None):
    masked = selm is not None
    assert not masked or paged
    nselp = selm.shape[1] // NSA_KV if masked else 0
    if paged:
        pt_flat, npages, pg = paged
        assert tk == pg * PAGE_SIZE
        nkt = npages // pg
        ix = lambda f: (lambda j, b, pt: f(b, j))

        def page_spec(k):
            return pl.BlockSpec((1, 2 * KV_W, PAGE_SIZE), lambda j, b, pt: (pt[b * npages + j * pg + k], 1, 0))

        past_specs, past_args = [page_spec(k) for k in range(pg)], [kv_past] * pg
    else:
        pg = 0
        nkt = kv_past.shape[1] // tk
        ix = lambda f: f
        past_specs = [pl.BlockSpec((1, tk, KV_W), lambda b, j: (b, j, 0)),
                      pl.BlockSpec((1, tk, KV_W), lambda b, j: (b, j, 1))]
        past_args = [kv_past, kv_past]
    in_specs = ([pl.BlockSpec((t, NSA_WIDTH), ix(lambda b, j: (b, 0)))] + past_specs +
                [pl.BlockSpec((NSA_HEADS, t, tk), ix(lambda b, j: (0, 0, j))),
                 pl.BlockSpec((t, KV_W), ix(lambda b, j: (b, new_cols[0]))),
                 pl.BlockSpec((t, KV_W), ix(lambda b, j: (b, new_cols[1]))),
                 pl.BlockSpec((NSA_HEADS, t, t), ix(lambda b, j: (0, 0, 0)))])
    args = [qn] + past_args + [bias_past, new_rows, new_rows, bias_new]
    if masked:
        nbt = tk // SEL_BLOCK
        sel_t = jnp.stack([selm[:, g * nselp:g * nselp + nkt * nbt].reshape(bsz, t, nkt, nbt) for g in range(NSA_KV)])
        sel_t = sel_t.transpose(1, 3, 0, 2, 4)
        in_specs += [pl.BlockSpec((1, 1, NSA_KV, t, nbt), ix(lambda b, j: (b, j, 0, 0, 0))),
                     pl.BlockSpec((nbt, tk), ix(lambda b, j: (0, 0)))]
        args += [sel_t, _block_expand(nbt, tk)]
    rows = NSA_REP * t
    nslab = nkt if paged else 1
    out_spec = pl.BlockSpec((1, t, NSA_WIDTH), ix(lambda b, j: (j if paged else 0, b, 0)))
    nst = bsz if paged else 1
    scratch = [pltpu.VMEM((nst, NSA_KV, rows, 1), F32), pltpu.VMEM((nst, NSA_KV, rows, 1), F32),
               pltpu.VMEM((nst, NSA_KV, rows, HEAD_DIM), F32)]
    body = functools.partial(_decode_attn_kernel, t=t, nselp=nselp, masked=masked, pg=pg)
    common = dict(out_shape=jax.ShapeDtypeStruct((nslab, bsz * t, NSA_WIDTH), F32),
                  name="decode_sel_attn" if masked else "decode_win_attn")
    if paged:
        grid_spec = pltpu.PrefetchScalarGridSpec(num_scalar_prefetch=1, grid=(nkt, bsz), in_specs=in_specs,
                                                 out_specs=out_spec, scratch_shapes=scratch)
        out = pl.pallas_call(body, grid_spec=grid_spec, compiler_params=_cp("arbitrary", "arbitrary"),
                             **common)(pt_flat, *args)
    else:
        out = pl.pallas_call(body, grid=(bsz, nkt), in_specs=in_specs, out_specs=out_spec, scratch_shapes=scratch,
                             compiler_params=_cp("parallel", "arbitrary"), **common)(*args)
    return out[nslab - 1]


def _combine_kernel(ng_ref, ge_ref, ocmp_ref, osel_ref, owin_ref, oc_ref):
    oc_ref[...] = _gated_sum(ng_ref[...], ge_ref[...], ocmp_ref[...], osel_ref[...], owin_ref[...]).astype(BF16)


def combine(y, ocmp, osel, owin):
    m = ocmp.shape[0]
    full = pl.BlockSpec((m, NSA_WIDTH), lambda i: (0, 0))
    return pl.pallas_call(
        _combine_kernel,
        grid=(1,),
        in_specs=[pl.BlockSpec((m, LANE), lambda i: (0, COL_NGATE // LANE)),
                  pl.BlockSpec((LANE, 3 * NSA_WIDTH), lambda i: (0, 0)), full, full, full],
        out_specs=full,
        out_shape=jax.ShapeDtypeStruct((m, NSA_WIDTH), BF16),
        compiler_params=_cp("arbitrary"),
        name="nsa_combine",
    )(y, _gate_expand(), ocmp, osel, owin)


def _overlap(nc, n_sel, nselp):
    cs = np.arange(nc) * CMP_STRIDE
    ss = np.arange(nselp) * SEL_BLOCK
    ov = (cs[:, None] < ss[None, :] + SEL_BLOCK) & (cs[:, None] + CMP_LEN > ss[None, :])
    ov &= (np.arange(nc) < nc - 1)[:, None] & (np.arange(nselp) < n_sel)[None, :]
    return jnp.asarray(ov.astype(np.float32), BF16)


def _round_up(x, m):
    return -(-x // m) * m


def position_tables(rel_bias, t, p0, lwin):
    nc = (t if p0 == 0 else p0) // CMP_STRIDE
    ar = jnp.arange
    qpos = p0 + ar(t)
    kend = ar(nc) * CMP_STRIDE + CMP_LEN - 1
    tabs = {"cmp": _bias_table(rel_bias, qpos[:, None] - kend[None, :])}
    if p0 == 0:
        tabs["attn"] = prompt_attn_tables(rel_bias, min(t, 256))
    else:
        dn = ar(t)[:, None] - ar(t)[None, :]
        tabs["new"] = _bias_table(rel_bias, dn, dn >= 0)
        tabs["sel"] = _bias_table(rel_bias, qpos[:, None] - ar(p0)[None, :])
        dw = qpos[:, None] - (p0 - lwin + ar(lwin))[None, :]
        tabs["win"] = _bias_table(rel_bias, dw, (dw >= 0) & (dw < WINDOW))
    return tabs


def layer(x, past, lw, tabs, bsz, t):
    m = bsz * t
    prompt = past is None
    tm = min(512, m)
    y = rms_matmul(x, lw["norm_mix"], lw["w_in"], min(1024, m), 1408)

    c = math.gcd(t, HG_CHUNK)
    s0 = jnp.zeros((bsz, HG_HEADS, HG_DIM, HG_DIM), F32) if prompt else past["hgrn"]
    oa, s_hg = hgrn(y, lw["lb"], lw["hg_norm"], s0, bsz, t, c, min(t, 1024))

    ts = min(t, 256)
    if prompt:
        h0r = h0i = jnp.zeros((bsz, 1, S5_CH), F32)
    else:
        h0r, h0i = past["s5r"].reshape(bsz, 1, S5_CH), past["s5i"].reshape(bsz, 1, S5_CH)
    ob, s5r, s5i = s5(y, lw["s5"], h0r, h0i, bsz, t, ts)

    qn, rows, winrows, cv, selx, winx = nsa_prep(y, lw["qkn"], tm)
    if prompt:
        p0, nc = 0, t // CMP_STRIDE
        chunks = cv.reshape(m // CMP_STRIDE, CMP_STRIDE * 2 * KV_W)
        cacb = matmul(chunks, lw["cmp_wab"], min(512, chunks.shape[0])).reshape(bsz, nc, 1024)
    else:
        npages = past["npages"]
        pg = min(16, npages)
        p0 = npages * PAGE_SIZE
        nc = p0 // CMP_STRIDE
        cacb = cmp_paged(past["cache_t"], past["pt_flat"], lw["cmp_wab"], bsz, npages, min(32, npages))
    kcmp, vcmp = cmp_post(cacb, lw["cmp_bias"], lw["cmp_w2"], lw["qkn"][1])
    n_sel = -(-(p0 + t) // SEL_BLOCK)
    nselp = _round_up(n_sel, LANE)
    tq = min(t, 128)
    nsr = _round_up(n_sel, 8) if prompt else 0
    ocmp, selm = cmpsel(qn, kcmp, vcmp, tabs["cmp"], _overlap(nc, n_sel, nselp), bsz, t, tq, p0, n_sel, nsr)
    if prompt:
        oc = prompt_attn(qn, selx, selm, ocmp, y, winx, tabs["attn"], bsz, t, min(t, 256), 2)
        lw_ = min(WINDOW, t)
        new_win = winrows.reshape(bsz, t, 2 * KV_W)[:, t - lw_:]
    else:
        osel = decode_attn(qn, past["cache_t"], tabs["sel"], rows, (2, 3), tabs["new"], bsz, t, pg * PAGE_SIZE,
                           selm=selm, paged=(past["pt_flat"], npages, pg))
        win = past["win"]
        owin = decode_attn(qn, win, tabs["win"], winrows, (0, 1), tabs["new"], bsz, t, win.shape[1])
        oc = combine(y, ocmp, osel, owin)
        new_win = jnp.concatenate([win, winrows.reshape(bsz, t, 2 * KV_W)], axis=1)[:, t:]

    x1 = merge(x, y, oa, ob, oc, lw["w_branch"], lw["w_out"], tm)
    x2 = ffn(x1, lw["norm_ffn"], lw["w_gate_up"], lw["w_down"], min(1024, m), 1408)
    return x2, (rows, new_win, s_hg, s5r, s5i)


def layer_weights(l, norm_mix, w_in, lower_bounds, hg_out_norm, s5_a_re, s5_a_im, s5_log_dt, s5_b_re, s5_b_im,
                  s5_c_re, s5_c_im, s5_d, s5_w_glu, nsa_qk_norm, cmp_pe, cmp_w1, cmp_w2, w_branch, w_out, norm_ffn,
                  w_gate_up, w_down):
    wab, cbias, w2blk = cmp_params(cmp_pe[l], cmp_w1[l], cmp_w2[l])
    s5p = s5_params(s5_a_re[l], s5_a_im[l], s5_log_dt[l], s5_b_re[l], s5_b_im[l], s5_c_re[l], s5_c_im[l],
                    s5_d[l], s5_w_glu[l], S5_SUB)
    return {"norm_mix": norm_mix[l].reshape(1, D_MODEL), "w_in": prep_w_in(w_in[l]),
            "lb": lower_bounds[l].reshape(1, HG_WIDTH), "hg_norm": hg_out_norm[l].reshape(1, HG_DIM),
            "s5": s5p, "qkn": nsa_qk_norm[l], "cmp_wab": wab, "cmp_bias": cbias, "cmp_w2": w2blk,
            "w_branch": w_branch[l].astype(BF16), "w_out": w_out[l].astype(BF16),
            "norm_ffn": norm_ffn[l].reshape(1, D_MODEL), "w_gate_up": w_gate_up[l].astype(BF16),
            "w_down": w_down[l].astype(BF16)}


def kernel(x_prompt, x_sample, cache_nsa_kv, cache_win_kv, state_hgrn, state_s5_re, state_s5_im, page_table,
           norm_mix, w_in, hg_lb_logits, hg_out_norm, s5_a_re, s5_a_im, s5_log_dt, s5_b_re, s5_b_im,
           s5_c_re, s5_c_im, s5_d, s5_w_glu, nsa_qk_norm, cmp_pe, cmp_w1, cmp_w2, rel_bias,
           w_branch, w_out, norm_ffn, w_gate_up, w_down):
    depth = w_in.shape[0]
    bp, tp, d = x_prompt.shape
    bs, tsm, _ = x_sample.shape
    n_phys = cache_nsa_kv.shape[1]
    npages = page_table.shape[1]
    lb_sm = jax.nn.softmax(hg_lb_logits.astype(F32), axis=0)
    lower_bounds = jnp.cumsum(lb_sm, axis=0) - lb_sm[0]
    hp = x_prompt.reshape(bp * tp, d)
    hs = x_sample.reshape(bs * tsm, d)
    cache_t = jnp.transpose(cache_nsa_kv, (0, 1, 3, 4, 5, 2)).reshape(depth * n_phys, 4 * KV_W, PAGE_SIZE)
    tabs_p = position_tables(rel_bias, tp, 0, 0)
    tabs_s = position_tables(rel_bias, tsm, npages * PAGE_SIZE, cache_win_kv.shape[2])
    st_p, st_s = [], []
    for l in range(depth):
        lw = layer_weights(l, norm_mix, w_in, lower_bounds, hg_out_norm, s5_a_re, s5_a_im,
                           s5_log_dt, s5_b_re, s5_b_im, s5_c_re, s5_c_im, s5_d, s5_w_glu, nsa_qk_norm, cmp_pe,
                           cmp_w1, cmp_w2, w_branch, w_out, norm_ffn, w_gate_up, w_down)
        past = {"cache_t": cache_t, "pt_flat": (page_table + l * n_phys).reshape(-1).astype(jnp.int32),
                "npages": npages, "win": cache_win_kv[l].reshape(bs, -1, 2 * KV_W), "hgrn": state_hgrn[l],
                "s5r": state_s5_re[l], "s5i": state_s5_im[l]}
        hp, sp = layer(hp, None, lw, tabs_p, bp, tp)
        hs, ss = layer(hs, past, lw, tabs_s, bs, tsm)
        st_p.append(sp)
        st_s.append(ss)

    def stack(states, k, shape):
        return jnp.stack([s[k].reshape(shape) for s in states])

    kvs = (4, NSA_KV, HEAD_DIM)
    return (hp.reshape(bp, tp, d), hs.reshape(bs, tsm, d),
            stack(st_p, 0, (bp, tp) + kvs), stack(st_s, 0, (bs, tsm) + kvs),
            stack(st_p, 1, (bp, -1, 2, NSA_KV, HEAD_DIM)), stack(st_s, 1, (bs, -1, 2, NSA_KV, HEAD_DIM)),
            stack(st_p, 2, (bp, HG_HEADS, HG_DIM, HG_DIM)), stack(st_s, 2, (bs, HG_HEADS, HG_DIM, HG_DIM)),
            stack(st_p, 3, (bp, S5_GROUPS, S5_STATE)), stack(st_p, 4, (bp, S5_GROUPS, S5_STATE)),
            stack(st_s, 3, (bs, S5_GROUPS, S5_STATE)), stack(st_s, 4, (bs, S5_GROUPS, S5_STATE)))
```

```python
import functools
import math

import numpy as np
import jax
import jax.numpy as jnp
from jax import lax
from jax.experimental import pallas as pl
from jax.experimental.pallas import tpu as pltpu

F32 = jnp.float32
BF16 = jnp.bfloat16

D_MODEL = 1024
HG_HEADS = 4
HG_DIM = 128
HG_WIDTH = HG_HEADS * HG_DIM
HG_CHUNK = 64
LB_FLOOR = 1e-30
S5_GROUP = 16
S5_GROUPS = 32
S5_WIDTH = S5_GROUP * S5_GROUPS
S5_STATE = 64
S5_CH = S5_GROUPS * S5_STATE
NSA_HEADS = 8
NSA_KV = 2
NSA_REP = NSA_HEADS // NSA_KV
HEAD_DIM = 64
NSA_WIDTH = NSA_HEADS * HEAD_DIM
KV_W = NSA_KV * HEAD_DIM
CMP_LEN = 32
CMP_STRIDE = 16
CMP_HIDDEN = 128
SEL_BLOCK = 64
SEL_TOPN = 16
WINDOW = 512
N_BUCKETS = 32
MAX_DIST = 128
PAGE_SIZE = 128
D_FF = 2816
NEG = -1e30
FORCE = 1e9
EPS = 1e-6

COL_GA, COL_GB, COL_GC = 0, 1024, 2048
COL_HQ, COL_HF, COL_HI, COL_HG = 3072, 3584, 4096, 4608
COL_SU, COL_NQ, COL_KV4, COL_WKV, COL_NGATE = 5120, 5632, 6144, 6656, 6912
N_IN_PAD = 7040
N_IN = 6936

LANE = 128
VMEM_LIMIT = 56 * 1024 * 1024


def _cp(*sem):
    return pltpu.CompilerParams(dimension_semantics=sem, vmem_limit_bytes=VMEM_LIMIT)


def _dot(a, b):
    return jnp.dot(a.astype(BF16), b.astype(BF16), preferred_element_type=F32)


def _dot_nt(a, b):
    return lax.dot_general(a.astype(BF16), b.astype(BF16), (((1,), (1,)), ((), ())), preferred_element_type=F32)


def _dot_tn(a, b):
    return lax.dot_general(a.astype(BF16), b.astype(BF16), (((0,), (0,)), ((), ())), preferred_element_type=F32)


def _split2(x):
    hi = x.astype(BF16)
    lo = (x - hi.astype(F32)).astype(BF16)
    return hi, lo


def _dot2(x, w):
    hi, lo = _split2(x)
    return jnp.dot(hi, w, preferred_element_type=F32) + jnp.dot(lo, w, preferred_element_type=F32)


def _silu(x):
    return x * jax.nn.sigmoid(x)


def _gelu_tanh(x):
    return 0.5 * x * (1.0 + jnp.tanh(math.sqrt(2.0 / math.pi) * (x + 0.044715 * (x * x * x))))


def _rms_mm_kernel(x_ref, g_ref, w_ref, o_ref, xn_ref):
    @pl.when(pl.program_id(1) == 0)
    def _():
        x = x_ref[...]
        ms = jnp.mean(x * x, axis=-1, keepdims=True)
        xn_ref[...] = (x * lax.rsqrt(ms + EPS) * g_ref[...]).astype(BF16)

    o_ref[...] = jnp.dot(xn_ref[...], w_ref[...], preferred_element_type=F32)


def rms_matmul(x, g, w, tm, tn):
    m, k = x.shape
    n = w.shape[1]
    return pl.pallas_call(
        _rms_mm_kernel,
        grid=(m // tm, n // tn),
        in_specs=[pl.BlockSpec((tm, k), lambda i, j: (i, 0)),
                  pl.BlockSpec((1, k), lambda i, j: (0, 0)),
                  pl.BlockSpec((k, tn), lambda i, j: (0, j))],
        out_specs=pl.BlockSpec((tm, tn), lambda i, j: (i, j)),
        out_shape=jax.ShapeDtypeStruct((m, n), F32),
        scratch_shapes=[pltpu.VMEM((tm, k), BF16)],
        compiler_params=_cp("parallel", "arbitrary"),
        name="rms_in_proj",
    )(x, g, w)


def _mm_kernel(x_ref, w_ref, o_ref):
    o_ref[...] = jnp.dot(x_ref[...], w_ref[...], preferred_element_type=F32)


def matmul(x, w, tm):
    m, k = x.shape
    n = w.shape[1]
    return pl.pallas_call(
        _mm_kernel,
        grid=(m // tm,),
        in_specs=[pl.BlockSpec((tm, k), lambda i: (i, 0)),
                  pl.BlockSpec((k, n), lambda i: (0, 0))],
        out_specs=pl.BlockSpec((tm, n), lambda i: (i, 0)),
        out_shape=jax.ShapeDtypeStruct((m, n), F32),
        compiler_params=_cp("parallel"),
        name="cmp_matmul",
    )(x, w)


def _merge_kernel(x_ref, ga_ref, gb_ref, gc_ref, oa_ref, ob_ref, oc_ref, wb_ref, wo_ref, o_ref):
    m = jax.nn.sigmoid(ga_ref[...]) * jnp.dot(oa_ref[...], wb_ref[0], preferred_element_type=F32)
    m = m + jax.nn.sigmoid(gb_ref[...]) * jnp.dot(ob_ref[...], wb_ref[1], preferred_element_type=F32)
    m = m + jax.nn.sigmoid(gc_ref[...]) * jnp.dot(oc_ref[...], wb_ref[2], preferred_element_type=F32)
    o_ref[...] = x_ref[...] + jnp.dot(m.astype(BF16), wo_ref[...], preferred_element_type=F32)


def merge(x, y, oa, ob, oc, wb, wo, tm):
    m, d = x.shape
    return pl.pallas_call(
        _merge_kernel,
        grid=(m // tm,),
        in_specs=[pl.BlockSpec((tm, d), lambda i: (i, 0)),
                  pl.BlockSpec((tm, d), lambda i: (i, COL_GA // D_MODEL)),
                  pl.BlockSpec((tm, d), lambda i: (i, COL_GB // D_MODEL)),
                  pl.BlockSpec((tm, d), lambda i: (i, COL_GC // D_MODEL)),
                  pl.BlockSpec((tm, 512), lambda i: (i, 0)),
                  pl.BlockSpec((tm, 512), lambda i: (i, 0)),
                  pl.BlockSpec((tm, 512), lambda i: (i, 0)),
                  pl.BlockSpec((3, 512, d), lambda i: (0, 0, 0)),
                  pl.BlockSpec((d, d), lambda i: (0, 0))],
        out_specs=pl.BlockSpec((tm, d), lambda i: (i, 0)),
        out_shape=jax.ShapeDtypeStruct((m, d), F32),
        compiler_params=_cp("parallel"),
        name="merge_out_proj",
    )(x, y, y, y, oa, ob, oc, wb, wo)


def _ffn_kernel(x_ref, g_ref, wg_ref, wu_ref, wd_ref, o_ref, xn_ref):
    @pl.when(pl.program_id(1) == 0)
    def _():
        x = x_ref[...]
        ms = jnp.mean(x * x, axis=-1, keepdims=True)
        xn_ref[...] = (x * lax.rsqrt(ms + EPS) * g_ref[...]).astype(BF16)
        o_ref[...] = x

    xn = xn_ref[...]
    gate = jnp.dot(xn, wg_ref[...], preferred_element_type=F32)
    up = jnp.dot(xn, wu_ref[...], preferred_element_type=F32)
    h = (_silu(gate) * up).astype(BF16)
    o_ref[...] += jnp.dot(h, wd_ref[...], preferred_element_type=F32)


def ffn(x, g, wgu, wd, tm, tf):
    m, d = x.shape
    nf = D_FF // tf
    return pl.pallas_call(
        _ffn_kernel,
        grid=(m // tm, nf),
        in_specs=[pl.BlockSpec((tm, d), lambda i, f: (i, 0)),
                  pl.BlockSpec((1, d), lambda i, f: (0, 0)),
                  pl.BlockSpec((d, tf), lambda i, f: (0, f)),
                  pl.BlockSpec((d, tf), lambda i, f: (0, nf + f)),
                  pl.BlockSpec((tf, d), lambda i, f: (f, 0))],
        out_specs=pl.BlockSpec((tm, d), lambda i, f: (i, 0)),
        out_shape=jax.ShapeDtypeStruct((m, d), F32),
        scratch_shapes=[pltpu.VMEM((tm, d), BF16)],
        compiler_params=_cp("parallel", "arbitrary"),
        name="swiglu_ffn",
    )(x, g, wgu, wgu, wd)


def _hgrn_levels(c):
    levels = [m for m in (32, 16, 8, 4, 2, 1) if m < c]
    return [m for m in levels if m < 8], [m for m in levels if m >= 8]


def _hgrn_consts(c):
    mats = [np.tril(np.ones((c, c), np.float32))]
    r = np.arange(c)
    for m in _hgrn_levels(c)[0]:
        pos = r % (2 * m)
        mid = (r // (2 * m)) * 2 * m + m
        up = np.zeros((c, c), np.float32)
        lo = np.zeros((c, c), np.float32)
        for t in range(c):
            if pos[t] >= m:
                up[t, mid[t]:t + 1] = 1.0
            else:
                lo[t, t + 1:mid[t]] = 1.0
        mats += [up, lo]
    return np.concatenate(mats, axis=0)


def _hgrn_kernel(hq_ref, hf_ref, hi_ref, hg_ref, lb_ref, gn_ref, s0_ref, mst_ref, o_ref, sout_ref, st_ref,
                 *, c, nchunk):
    tb = pl.program_id(2)

    @pl.when(tb == 0)
    def _():
        st_ref[...] = s0_ref[0, 0].T

    small, big = _hgrn_levels(c)
    rid = lax.broadcasted_iota(jnp.int32, (c, HG_DIM), 0)
    lb = lb_ref[...]
    log_lb = jnp.log(jnp.maximum(lb, LB_FLOOR))
    log_1m = jnp.log1p(-lb)
    one_m = 1.0 - lb
    gn = gn_ref[...]
    mst = mst_ref[...]
    ti = lax.broadcasted_iota(jnp.int32, (c, c), 0)
    si = lax.broadcasted_iota(jnp.int32, (c, c), 1)
    txs = jnp.bitwise_xor(ti, si)
    lower = ti > si

    for ch in range(nchunk):
        rows = pl.ds(ch * c, c)
        fp = hf_ref[rows, :]
        hq = hq_ref[rows, :]
        v = hi_ref[rows, :]
        hg = hg_ref[rows, :]
        log_sig = -(jnp.maximum(-fp, 0.0) + jnp.log1p(jnp.exp(-jnp.abs(fp))))
        b = log_1m + log_sig
        g = jnp.maximum(log_lb, b) + jnp.log1p(jnp.exp(-jnp.abs(log_lb - b)))
        kin = one_m * jax.nn.sigmoid(-fp)
        q = _silu(hq)

        g1 = g.astype(BF16)
        r1 = g - g1.astype(F32)
        g2 = r1.astype(BF16)
        g3 = (r1 - g2.astype(F32)).astype(BF16)
        gs = jnp.concatenate([g1, g2, g3], axis=1)
        rr = jnp.dot(mst, gs, preferred_element_type=F32)
        rr = rr[:, 0:LANE] + rr[:, LANE:2 * LANE] + rr[:, 2 * LANE:3 * LANE]
        gcum = rr[0:c]

        vb = v.astype(BF16)
        att = jnp.where(ti == si, _dot_nt(q, kin), 0.0)
        for li, m in enumerate(small + big):
            if m in small:
                dq = rr[(1 + 2 * li) * c:(2 + 2 * li) * c]
                ek = rr[(2 + 2 * li) * c:(3 + 2 * li) * c]
            else:
                gb = gcum[m - 1:m, :]
                for blk in range(1, c // (2 * m)):
                    gb = jnp.where(rid >= blk * 2 * m, gcum[blk * 2 * m + m - 1:blk * 2 * m + m, :], gb)
                dq = jnp.minimum(gcum - gb, 0.0)
                ek = jnp.minimum(gb - gcum, 0.0)
            pair = _dot_nt(q * jnp.exp(dq), kin * jnp.exp(ek))
            sel = lower & (jnp.right_shift(txs, int(math.log2(m))) == 1)
            att = jnp.where(sel, pair, att)
        st = st_ref[...]
        o = _dot(att, vb) + _dot_nt(q * jnp.exp(gcum), st)
        g_end = gcum[c - 1:c, :]
        kd = kin * jnp.exp(g_end - gcum)
        st_ref[...] = st * jnp.exp(g_end) + _dot_tn(vb, kd)

        ms = jnp.mean(o * o, axis=-1, keepdims=True)
        on = o * lax.rsqrt(ms + EPS) * gn
        o_ref[rows, :] = (on * _silu(hg)).astype(BF16)

    @pl.when(tb == pl.num_programs(2) - 1)
    def _():
        sout_ref[0, 0] = st_ref[...].T


def hgrn(y, lb, gn, s0, bsz, t, c, tbk):
    ntb = t // tbk
    mst = jnp.asarray(_hgrn_consts(c), BF16)

    def col(base):
        return pl.BlockSpec((tbk, HG_DIM), lambda b, h, k: (b * ntb + k, base // HG_DIM + h))

    return pl.pallas_call(
        functools.partial(_hgrn_kernel, c=c, nchunk=tbk // c),
        grid=(bsz, HG_HEADS, ntb),
        in_specs=[col(COL_HQ), col(COL_HF), col(COL_HI), col(COL_HG),
                  pl.BlockSpec((1, HG_DIM), lambda b, h, k: (0, h)),
                  pl.BlockSpec((1, HG_DIM), lambda b, h, k: (0, 0)),
                  pl.BlockSpec((1, 1, HG_DIM, HG_DIM), lambda b, h, k: (b, h, 0, 0)),
                  pl.BlockSpec(mst.shape, lambda b, h, k: (0, 0))],
        out_specs=[pl.BlockSpec((tbk, HG_DIM), lambda b, h, k: (b * ntb + k, h)),
                   pl.BlockSpec((1, 1, HG_DIM, HG_DIM), lambda b, h, k: (b, h, 0, 0))],
        out_shape=[jax.ShapeDtypeStruct((bsz * t, HG_WIDTH), BF16),
                   jax.ShapeDtypeStruct((bsz, HG_HEADS, HG_DIM, HG_DIM), F32)],
        scratch_shapes=[pltpu.VMEM((HG_DIM, HG_DIM), F32)],
        compiler_params=_cp("parallel", "parallel", "arbitrary"),
        name="hgrn2_scan",
    )(y, y, y, y, lb, gn, s0, mst)


S5_SUB = 8


def _shift_in_group(x, d):
    n, w = x.shape
    x3 = x.reshape(n // S5_SUB, S5_SUB, w)
    rid = lax.broadcasted_iota(jnp.int32, x3.shape, 1)
    return jnp.where(rid >= d, pltpu.roll(x3, d, axis=1), 0.0).reshape(n, w)


def _s5_kernel(u_ref, bre_ref, bim_ref, adr_ref, adi_ref, atr_ref, ati_ref, h0r_ref, h0i_ref,
               cre_ref, cim_ref, d_ref, wg_ref, o_ref, hr_out, hi_out, cr_ref, ci_ref, *, ts):
    k = pl.program_id(1)

    @pl.when(k == 0)
    def _():
        cr_ref[...] = h0r_ref[0]
        ci_ref[...] = h0i_ref[0]

    u = u_ref[...]
    ub = u.astype(BF16)
    xr = jnp.dot(ub, bre_ref[...], preferred_element_type=F32)
    xi = jnp.dot(ub, bim_ref[...], preferred_element_type=F32)
    lev = 0
    d = 1
    while d < S5_SUB:
        ar = adr_ref[lev:lev + 1, :]
        ai = adi_ref[lev:lev + 1, :]
        sr = _shift_in_group(xr, d)
        si = _shift_in_group(xi, d)
        xr, xi = xr + ar * sr - ai * si, xi + ar * si + ai * sr
        d *= 2
        lev += 1
    cr = cr_ref[...]
    ci = ci_ref[...]
    atr = atr_ref[...]
    ati = ati_ref[...]
    hrs, his = [], []
    for r in range(ts // S5_SUB):
        gr = xr[r * S5_SUB:(r + 1) * S5_SUB]
        gi = xi[r * S5_SUB:(r + 1) * S5_SUB]
        gr, gi = gr + atr * cr - ati * ci, gi + atr * ci + ati * cr
        hrs.append(gr)
        his.append(gi)
        cr, ci = gr[S5_SUB - 1:S5_SUB], gi[S5_SUB - 1:S5_SUB]
    hr = jnp.concatenate(hrs, axis=0)
    hi = jnp.concatenate(his, axis=0)
    cr_ref[...] = cr
    ci_ref[...] = ci
    y = jnp.dot(hr.astype(BF16), cre_ref[...], preferred_element_type=F32) \
        - jnp.dot(hi.astype(BF16), cim_ref[...], preferred_element_type=F32)
    y = _gelu_tanh(y + d_ref[...] * u)
    o_ref[...] = (y * jax.nn.sigmoid(jnp.dot(y.astype(BF16), wg_ref[...], preferred_element_type=F32))).astype(BF16)

    @pl.when(k == pl.num_programs(1) - 1)
    def _():
        hr_out[0] = cr
        hi_out[0] = ci


def s5(y, prm, h0r, h0i, bsz, t, ts):
    nts = t // ts
    nlev = prm["adr"].shape[0]
    const = lambda shape: pl.BlockSpec(shape, lambda b, k: tuple(0 for _ in shape))
    state = pl.BlockSpec((1, 1, S5_CH), lambda b, k: (b, 0, 0))
    return pl.pallas_call(
        functools.partial(_s5_kernel, ts=ts),
        grid=(bsz, nts),
        in_specs=[pl.BlockSpec((ts, S5_WIDTH), lambda b, k: (b * nts + k, COL_SU // S5_WIDTH)),
                  const((S5_WIDTH, S5_CH)), const((S5_WIDTH, S5_CH)),
                  const((nlev, S5_CH)), const((nlev, S5_CH)),
                  const((S5_SUB, S5_CH)), const((S5_SUB, S5_CH)),
                  state, state,
                  const((S5_CH, S5_WIDTH)), const((S5_CH, S5_WIDTH)),
                  const((1, S5_WIDTH)), const((S5_WIDTH, S5_WIDTH))],
        out_specs=[pl.BlockSpec((ts, S5_WIDTH), lambda b, k: (b * nts + k, 0)), state, state],
        out_shape=[jax.ShapeDtypeStruct((bsz * t, S5_WIDTH), BF16),
                   jax.ShapeDtypeStruct((bsz, 1, S5_CH), F32),
                   jax.ShapeDtypeStruct((bsz, 1, S5_CH), F32)],
        scratch_shapes=[pltpu.VMEM((1, S5_CH), F32), pltpu.VMEM((1, S5_CH), F32)],
        compiler_params=_cp("parallel", "arbitrary"),
        name="s5_scan",
    )(y, prm["bre"], prm["bim"], prm["adr"], prm["adi"], prm["atr"], prm["ati"], h0r, h0i,
      prm["cre"], prm["cim"], prm["d"], prm["wglu"])


def _cmul(ar, ai, br, bi):
    return ar * br - ai * bi, ar * bi + ai * br


def s5_params(a_re, a_im, log_dt, b_re, b_im, c_re, c_im, d, w_glu, ts):
    step = jnp.exp(log_dt)[:, None]
    mag = jnp.exp(a_re * step)
    ab_re, ab_im = mag * jnp.cos(a_im * step), mag * jnp.sin(a_im * step)
    den = a_re * a_re + a_im * a_im
    z_re = ((ab_re - 1.0) * a_re + ab_im * a_im) / den
    z_im = (ab_im * a_re - (ab_re - 1.0) * a_im) / den
    bb_re = z_re[..., None] * b_re - z_im[..., None] * b_im
    bb_im = z_re[..., None] * b_im + z_im[..., None] * b_re
    eye = jnp.eye(S5_GROUPS, dtype=F32)
    bd_in = lambda w: jnp.einsum("gpn,gh->gnhp", w, eye).reshape(S5_WIDTH, S5_CH).astype(BF16)
    bd_out = lambda w: jnp.einsum("gnp,gh->gphn", w, eye).reshape(S5_CH, S5_WIDTH).astype(BF16)
    ar, ai = ab_re.reshape(1, S5_CH), ab_im.reshape(1, S5_CH)
    adr, adi = [ar], [ai]
    d2 = 2
    while d2 < ts:
        r, i = _cmul(adr[-1], adi[-1], adr[-1], adi[-1])
        adr.append(r)
        adi.append(i)
        d2 *= 2
    atr, ati = ar, ai
    n = 1
    lev = 0
    while n < ts:
        pr, pi = _cmul(atr, ati, adr[lev], adi[lev])
        atr, ati = jnp.concatenate([atr, pr], 0), jnp.concatenate([ati, pi], 0)
        n *= 2
        lev += 1
    return {"bre": bd_in(bb_re), "bim": bd_in(bb_im), "cre": bd_out(c_re), "cim": bd_out(c_im),
            "adr": jnp.concatenate(adr, 0), "adi": jnp.concatenate(adi, 0), "atr": atr, "ati": ati,
            "d": d.reshape(1, S5_WIDTH), "wglu": w_glu.astype(BF16)}


def prep_w_in(w):
    pad = jnp.zeros((w.shape[0], N_IN_PAD - N_IN), w.dtype)
    return jnp.concatenate([w[:, 3864:6936], w[:, 0:3840], w[:, 3840:3864], pad], axis=1).astype(BF16)


def _seg_rms(x, ones, w):
    ss = _dot2(x * x, ones)
    return x * lax.rsqrt(ss * (1.0 / HEAD_DIM) + EPS) * w


def _kv_operands(k, v):
    tm = k.shape[0]
    zero, one = jnp.zeros((tm, HEAD_DIM), F32), jnp.ones((tm, HEAD_DIM), F32)
    parts = []
    for g in range(NSA_KV):
        parts += [k[:, g * HEAD_DIM:(g + 1) * HEAD_DIM], zero]
    for g in range(NSA_KV):
        parts += [v[:, g * HEAD_DIM:(g + 1) * HEAD_DIM], one]
    return jnp.concatenate(parts, axis=1).astype(BF16)


def _nsa_prep_kernel(nq_ref, kv_ref, wk_ref, wq_ref, wks_ref, wkw_ref, ones_ref,
                     qn_ref, rows_ref, win_ref, cv_ref, selx_ref, winx_ref):
    qn_ref[...] = _seg_rms(nq_ref[...], ones_ref[...], wq_ref[...]).astype(BF16)
    ones1 = ones_ref[0:LANE, 0:LANE]
    kv = kv_ref[...]
    ksn = _seg_rms(kv[:, 2 * KV_W:3 * KV_W], ones1, wks_ref[...])
    rows_ref[...] = jnp.concatenate([kv[:, 0:2 * KV_W], ksn, kv[:, 3 * KV_W:4 * KV_W]], axis=1)
    cv_ref[...] = kv[:, 0:2 * KV_W].astype(BF16)
    selx_ref[...] = _kv_operands(ksn, kv[:, 3 * KV_W:4 * KV_W])
    wk = wk_ref[...]
    kwn = _seg_rms(wk[:, 0:KV_W], ones1, wkw_ref[...])
    win_ref[...] = jnp.concatenate([kwn, wk[:, KV_W:2 * KV_W]], axis=1)
    winx_ref[...] = _kv_operands(kwn, wk[:, KV_W:2 * KV_W])


def _block_ones(n, blk):
    r = np.arange(n) // blk
    return jnp.asarray((r[:, None] == r[None, :]).astype(np.float32), BF16)


def nsa_prep(y, qkn, tm):
    m = y.shape[0]
    wq = (jnp.tile(qkn[0], NSA_HEADS) * HEAD_DIM ** -0.5).reshape(1, NSA_WIDTH)
    wks = jnp.tile(qkn[2], NSA_KV).reshape(1, KV_W)
    wkw = jnp.tile(qkn[3], NSA_KV).reshape(1, KV_W)
    ones = _block_ones(NSA_WIDTH, HEAD_DIM)
    const = lambda shape: pl.BlockSpec(shape, lambda i: (0, 0))
    row = lambda w: pl.BlockSpec((tm, w), lambda i: (i, 0))
    return pl.pallas_call(
        _nsa_prep_kernel,
        grid=(m // tm,),
        in_specs=[pl.BlockSpec((tm, 512), lambda i: (i, COL_NQ // 512)),
                  pl.BlockSpec((tm, 512), lambda i: (i, COL_KV4 // 512)),
                  pl.BlockSpec((tm, 256), lambda i: (i, COL_WKV // 256)),
                  const((1, NSA_WIDTH)), const((1, KV_W)), const((1, KV_W)), const((NSA_WIDTH, NSA_WIDTH))],
        out_specs=[row(512), row(512), row(256), row(256), row(512), row(512)],
        out_shape=[jax.ShapeDtypeStruct((m, NSA_WIDTH), BF16), jax.ShapeDtypeStruct((m, 4 * KV_W), F32),
                   jax.ShapeDtypeStruct((m, 2 * KV_W), F32), jax.ShapeDtypeStruct((m, 2 * KV_W), BF16),
                   jax.ShapeDtypeStruct((m, 4 * LANE), BF16), jax.ShapeDtypeStruct((m, 4 * LANE), BF16)],
        compiler_params=_cp("parallel"),
        name="nsa_prep",
    )(y, y, y, wq, wks, wkw, ones)


def _cmp_post_kernel(c_ref, cb_ref, w2_ref, wk_ref, ones_ref, k_ref, v_ref):
    c = c_ref[0]
    nc = c.shape[0]
    ca, cb = c[:, 0:512], c[:, 512:1024]
    rid = lax.broadcasted_iota(jnp.int32, cb.shape, 0)
    cb_next = jnp.where(rid < nc - 1, pltpu.roll(cb, nc - 1, axis=0), 0.0)
    hid = _gelu_tanh(ca + cb_next + cb_ref[...])
    out = _dot(hid, w2_ref[...])
    k_ref[0] = _seg_rms(out[:, 0:KV_W], ones_ref[...], wk_ref[...]).T.astype(BF16)
    v_ref[0] = out[:, KV_W:2 * KV_W].T.astype(BF16)


def cmp_post(cacb, cbias, w2blk, qkn1):
    bsz, nc, _ = cacb.shape
    wk = jnp.tile(qkn1, NSA_KV).reshape(1, KV_W)
    ones = _block_ones(KV_W, HEAD_DIM)
    const = lambda a: pl.BlockSpec(a.shape, lambda b: (0, 0))
    out = pl.BlockSpec((1, KV_W, nc), lambda b: (b, 0, 0))
    return pl.pallas_call(
        _cmp_post_kernel,
        grid=(bsz,),
        in_specs=[pl.BlockSpec((1,) + cacb.shape[1:], lambda b: (b, 0, 0)),
                  const(cbias), const(w2blk), const(wk), const(ones)],
        out_specs=[out, out],
        out_shape=[jax.ShapeDtypeStruct((bsz, KV_W, nc), BF16)] * 2,
        compiler_params=_cp("parallel"),
        name="cmp_post",
    )(cacb, cbias, w2blk, wk, ones)


def _cmp_paged_kernel(pt_ref, *refs, pg):
    pages = refs[:pg]
    wt_ref, perm_ref, o_ref, tok_ref = refs[pg:]
    cpp = PAGE_SIZE // CMP_STRIDE
    for k in range(pg):
        tok = _dot_nt(perm_ref[...], pages[k][0])
        for j in range(CMP_STRIDE):
            tok_ref[j, k * cpp:(k + 1) * cpp, :] = tok[j * cpp:(j + 1) * cpp, :]
    nch = pg * cpp
    nblk = 2 * KV_W // HEAD_DIM
    accs = [jnp.zeros((nch, 2 * CMP_HIDDEN), F32) for _ in range(nblk)]
    for j in range(CMP_STRIDE):
        rj = tok_ref[j]
        for c in range(nblk):
            accs[c] = accs[c] + _dot(rj[:, c * HEAD_DIM:(c + 1) * HEAD_DIM], wt_ref[j, c])
    o_ref[0] = jnp.concatenate([accs[c][:, half * CMP_HIDDEN:(half + 1) * CMP_HIDDEN]
                                for half in range(2) for c in range(nblk)], axis=1)


def cmp_paged(cache_t, pt_flat, wab, bsz, npages, pg):
    nch = pg * PAGE_SIZE // CMP_STRIDE
    nc = npages * PAGE_SIZE // CMP_STRIDE
    nblk = 2 * KV_W // HEAD_DIM
    w6 = wab.reshape(CMP_STRIDE, nblk, HEAD_DIM, 2, nblk, CMP_HIDDEN)
    wt = jnp.stack([w6[:, c, :, :, c, :] for c in range(nblk)], axis=1).reshape(CMP_STRIDE, nblk, HEAD_DIM, 2 * CMP_HIDDEN)
    cpp = PAGE_SIZE // CMP_STRIDE
    r = np.arange(PAGE_SIZE)
    src = (r % cpp) * CMP_STRIDE + r // cpp
    perm = jnp.asarray((src[:, None] == r[None, :]).astype(np.float32), BF16)

    def page_spec(k):
        return pl.BlockSpec((1, 2 * KV_W, PAGE_SIZE), lambda b, p, pt: (pt[b * npages + p * pg + k], 0, 0))

    return pl.pallas_call(
        functools.partial(_cmp_paged_kernel, pg=pg),
        grid_spec=pltpu.PrefetchScalarGridSpec(
            num_scalar_prefetch=1, grid=(bsz, npages // pg),
            in_specs=[page_spec(k) for k in range(pg)] +
                     [pl.BlockSpec(wt.shape, lambda b, p, pt: (0, 0, 0, 0)), pl.BlockSpec(perm.shape, lambda b, p, pt: (0, 0))],
            out_specs=pl.BlockSpec((1, nch, 1024), lambda b, p, pt: (b, p, 0)),
            scratch_shapes=[pltpu.VMEM((CMP_STRIDE, nch, 2 * KV_W), F32)]),
        out_shape=jax.ShapeDtypeStruct((bsz, nc, 1024), F32),
        compiler_params=_cp("parallel", "arbitrary"),
        name="cmp_paged",
    )(pt_flat, *([cache_t] * pg), wt, perm)


def cmp_params(pe, w1, w2):
    w1r = w1.reshape(2, 2, 16, HEAD_DIM, CMP_HIDDEN)
    eye2 = jnp.eye(2, dtype=F32)
    wab = jnp.einsum("khjdc,kl,gm->jkgdhlmc", w1r, eye2, eye2).reshape(16 * 256, 1024).astype(BF16)
    cb = jnp.einsum("kf,kfc->kc", pe.reshape(2, CMP_LEN * HEAD_DIM), w1)
    cbias = jnp.broadcast_to(cb[:, None, :], (2, NSA_KV, CMP_HIDDEN)).reshape(1, 512)
    w2blk = jnp.einsum("kcd,kl,gm->kgclmd", w2, eye2, eye2).reshape(512, 256).astype(BF16)
    return wab, cbias, w2blk


def _cmpsel_kernel(qn_ref, kc_ref, vc_ref, bias_ref, ov_ref, ocmp_ref, selm_ref, *, tq, p0, n_sel, nselp, nsr):
    i = pl.program_id(0)
    q = qn_ref[...]
    kc = kc_ref[0]
    vc = vc_ref[0]
    nc = kc.shape[1]
    qpos = p0 + i * tq + lax.broadcasted_iota(jnp.int32, (tq, 1), 0)
    kend = lax.broadcasted_iota(jnp.int32, (1, nc), 1) * CMP_STRIDE + (CMP_LEN - 1)
    valid = qpos >= kend
    row_valid = (qpos >= CMP_LEN - 1).astype(F32)
    imp = [None, None]
    outs = []
    for h in range(NSA_HEADS):
        g = h // NSA_REP
        s = _dot(q[:, h * HEAD_DIM:(h + 1) * HEAD_DIM], kc[g * HEAD_DIM:(g + 1) * HEAD_DIM, :]) + bias_ref[h]
        s = jnp.where(valid, s, NEG)
        e = jnp.exp(s - jnp.max(s, axis=-1, keepdims=True))
        p = e / jnp.sum(e, axis=-1, keepdims=True) * row_valid
        outs.append(_dot_nt(p, vc[g * HEAD_DIM:(g + 1) * HEAD_DIM, :]))
        imp[g] = p if imp[g] is None else imp[g] + p
    ocmp_ref[...] = jnp.concatenate(outs, axis=1)

    jid = lax.broadcasted_iota(jnp.int32, (1, nselp), 1)
    cur = jnp.right_shift(qpos, int(math.log2(SEL_BLOCK)))
    forced = (jid == 0) | (jid == cur) | (jid == cur - 1)
    future = jid * SEL_BLOCK > qpos
    topn = min(SEL_TOPN, n_sel)
    for g in range(NSA_KV):
        score = _dot2(imp[g], ov_ref[...])
        score = jnp.where(forced, FORCE, jnp.where(future, -FORCE, score))
        if nsr:
            st = score.T[0:nsr]
            rid = lax.broadcasted_iota(jnp.int32, (nsr, tq), 0)
            cnt = jnp.zeros((nsr, tq), F32)
            for c in range(n_sel):
                tie = jnp.where(rid > c, 1.0, 0.0)
                row = st[c:c + 1, :]
                cnt = cnt + jnp.where(row > st, 1.0, jnp.where(row == st, tie, 0.0))
            sel = jnp.where(rid < n_sel, jnp.where(cnt < topn, 1.0, 0.0), 0.0)
            sel = jnp.concatenate([sel, jnp.zeros((nselp - nsr, tq), F32)], axis=0).T
            selm_ref[:, g * nselp:(g + 1) * nselp] = sel.astype(BF16)
        else:
            cnt = jnp.zeros((tq, nselp), F32)
            for c in range(n_sel):
                col = score[:, c:c + 1]
                tie = (jid > c).astype(F32)
                cnt = cnt + jnp.where(col > score, 1.0, jnp.where(col == score, tie, 0.0))
            sel = jnp.where(jid < n_sel, jnp.where(cnt < topn, 1.0, 0.0), 0.0)
            selm_ref[:, g * nselp:(g + 1) * nselp] = sel.astype(BF16)


def cmpsel(qn, kcmp, vcmp, biasc, overlap, bsz, t, tq, p0, n_sel, nsr=0):
    nt = t // tq
    nc = kcmp.shape[2]
    nselp = overlap.shape[1]
    assert CMP_STRIDE * (nc - 1) + CMP_LEN - 1 > p0 + t - 1
    assert not nsr or (tq == LANE and nselp == LANE)
    sel_spec = pl.BlockSpec((tq, NSA_KV * nselp), lambda i, b: (b * nt + i, 0))
    sel_shape = jax.ShapeDtypeStruct((bsz * t, NSA_KV * nselp), BF16)
    return pl.pallas_call(
        functools.partial(_cmpsel_kernel, tq=tq, p0=p0, n_sel=n_sel, nselp=nselp, nsr=nsr),
        grid=(nt, bsz),
        in_specs=[pl.BlockSpec((tq, NSA_WIDTH), lambda i, b: (b * nt + i, 0)),
                  pl.BlockSpec((1, KV_W, nc), lambda i, b: (b, 0, 0)),
                  pl.BlockSpec((1, KV_W, nc), lambda i, b: (b, 0, 0)),
                  pl.BlockSpec((NSA_HEADS, tq, nc), lambda i, b: (0, i, 0)),
                  pl.BlockSpec((nc, nselp), lambda i, b: (0, 0))],
        out_specs=[pl.BlockSpec((tq, NSA_WIDTH), lambda i, b: (b * nt + i, 0)), sel_spec],
        out_shape=[jax.ShapeDtypeStruct((bsz * t, NSA_WIDTH), F32), sel_shape],
        compiler_params=_cp("arbitrary", "arbitrary"),
        name="cmp_attn_select",
    )(qn, kcmp, vcmp, biasc, overlap)


def _stack_heads(q, g):
    return jnp.concatenate([q[:, (NSA_REP * g + r) * HEAD_DIM:(NSA_REP * g + r + 1) * HEAD_DIM]
                            for r in range(NSA_REP)], axis=0)


def _unstack_heads(o_groups, tq):
    return jnp.concatenate([o[r * tq:(r + 1) * tq] for o in o_groups for r in range(NSA_REP)], axis=1)


def _online_update(carry, s, v, v_transposed=False):
    m, l, acc = carry
    m_new = jnp.maximum(m, jnp.max(s, axis=-1, keepdims=True))
    a = jnp.exp(m - m_new)
    p = jnp.exp(s - m_new)
    pv = _dot_nt(p, v) if v_transposed else _dot(p, v)
    return m_new, a * l + jnp.sum(p, axis=-1, keepdims=True), a * acc + pv


def _gated_sum(ng, ge, ocmp, osel, owin):
    gx = _dot2(jax.nn.sigmoid(ng), ge)
    return gx[:, 0:512] * ocmp + gx[:, 512:1024] * osel + gx[:, 1024:1536] * owin


def _online_update_fused(carry, s, vx):
    m, acc = carry
    m_new = jnp.maximum(m, jnp.max(s, axis=-1, keepdims=True))
    p = jnp.exp(s - m_new)
    return m_new, jnp.exp(m - m_new) * acc + jnp.dot(p.astype(BF16), vx, preferred_element_type=F32)


def _finish_fused(carry):
    acc = carry[1]
    return acc[:, 0:HEAD_DIM] / acc[:, HEAD_DIM:HEAD_DIM + 1]


def _prompt_attn_kernel(qn_ref, sx_ref, selm_ref, e_ref, bnear_ref, wx_ref, bw_ref,
                        ocmp_ref, ng_ref, ge_ref, oc_ref, madd_ref, *, tq, nselp, nprev, kt):
    i = pl.program_id(1)
    q = qn_ref[...]
    rows = NSA_REP * tq
    n_far = jnp.maximum(i - 1, 0)
    n_macro = n_far // kt
    init1 = (jnp.full((rows, 1), -jnp.inf, F32), jnp.zeros((rows, LANE), F32))
    init = (init1,) * NSA_KV
    heads = [slice(NSA_REP * g, NSA_REP * (g + 1)) for g in range(NSA_KV)]
    kcol = [slice(g * LANE, (g + 1) * LANE) for g in range(NSA_KV)]
    vcol = [slice((NSA_KV + g) * LANE, (NSA_KV + g + 1) * LANE) for g in range(NSA_KV)]
    zpad = jnp.zeros((rows, LANE - HEAD_DIM), BF16)
    qgs = [jnp.concatenate([_stack_heads(q, g), zpad], axis=1) for g in range(NSA_KV)]
    for g in range(NSA_KV):
        hit = jnp.dot(selm_ref[:, g * nselp:(g + 1) * nselp], e_ref[...], preferred_element_type=F32)
        madd_ref[g] = (hit - 1.0) * (-NEG)

    def step(r0, width, carries, near):
        out = []
        for g in range(NSA_KV):
            add = madd_ref[g, :, pl.ds(r0, width)][None]
            if near is not None:
                add = add + bnear_ref[heads[g], :, near]
            s = _dot_nt(qgs[g], sx_ref[pl.ds(r0, width), kcol[g]]).reshape(NSA_REP, tq, width) + add
            out.append(_online_update_fused(carries[g], s.reshape(rows, width), sx_ref[pl.ds(r0, width), vcol[g]]))
        return tuple(out)

    def far(j, ntiles, c):
        return step(pl.multiple_of(j * tq, tq), ntiles * tq, c, None)

    carries = lax.fori_loop(0, n_macro, lambda j, c: far(j * kt, kt, c), init)
    done = n_macro * kt
    w = kt // 2
    while w >= 1:
        take = jnp.bitwise_and(n_far - done, w) != 0
        carries = lax.cond(take, functools.partial(far, done, w), lambda c: c, carries)
        done = done + jnp.where(take, w, 0)
        w //= 2
    carries = lax.cond(
        i >= 1,
        lambda c: step(pl.multiple_of((i - 1) * tq, tq), 2 * tq, c, slice(0, 2 * tq)),
        lambda c: step(0, tq, c, slice(tq, 2 * tq)),
        carries)
    o_sel = [_finish_fused(c) for c in carries]

    o_win = []
    nk = (nprev + 1) * tq
    r0 = pl.multiple_of(i * tq, tq)
    kval = lax.broadcasted_iota(jnp.int32, (1, 1, nk), 2) >= (nprev - i) * tq
    for g in range(NSA_KV):
        s = _dot_nt(qgs[g], wx_ref[0, pl.ds(r0, nk), kcol[g]]).reshape(NSA_REP, tq, nk) + bw_ref[heads[g]]
        s = jnp.where(kval, s, NEG).reshape(rows, nk)
        o_win.append(_finish_fused(_online_update_fused(init1, s, wx_ref[0, pl.ds(r0, nk), vcol[g]])))

    oc = _gated_sum(ng_ref[...], ge_ref[...], ocmp_ref[...], _unstack_heads(o_sel, tq), _unstack_heads(o_win, tq))
    oc_ref[...] = oc.astype(BF16)


def _rel_bucket(dist):
    n = jnp.maximum(dist, 0)
    exact = N_BUCKETS // 2
    nf = jnp.maximum(n, 1).astype(F32)
    large = exact + (jnp.log(nf / exact) / math.log(MAX_DIST / exact) * (N_BUCKETS - exact)).astype(jnp.int32)
    return jnp.where(n < exact, n, jnp.minimum(large, N_BUCKETS - 1))


def _bias_last(rel_bias, dist, valid=None):
    onehot = (_rel_bucket(dist)[..., None] == jnp.arange(N_BUCKETS)).astype(F32)
    b = jnp.einsum("...k,kh->...h", onehot, rel_bias.astype(F32), precision=lax.Precision.HIGHEST)
    if valid is not None:
        b = jnp.where(valid[..., None], b, NEG)
    return b


def _bias_table(rel_bias, dist, valid=None):
    return jnp.moveaxis(_bias_last(rel_bias, dist, valid), -1, 0)


def _bias_table_t(rel_bias, dist, valid=None):
    b = jnp.swapaxes(_bias_last(rel_bias, dist, valid), -1, -2)
    return b.reshape(b.shape[:-2] + (b.shape[-2] * b.shape[-1],))


def _gate_expand():
    ge = np.zeros((LANE, 3 * NSA_WIDTH), np.float32)
    for br in range(3):
        for h in range(NSA_HEADS):
            ge[br * NSA_HEADS + h, br * NSA_WIDTH + h * HEAD_DIM: br * NSA_WIDTH + (h + 1) * HEAD_DIM] = 1.0
    return jnp.asarray(ge, BF16)


def _block_expand(nselp, length):
    e = (np.arange(nselp)[:, None] == (np.arange(length) // SEL_BLOCK)[None, :]).astype(np.float32)
    return jnp.asarray(e, BF16)


def prompt_attn_tables(rel_bias, tq):
    nprev = WINDOW // tq
    nk = (nprev + 1) * tq
    ar = jnp.arange
    dn = tq + ar(tq)[:, None] - ar(2 * tq)[None, :]
    dw = ar(tq)[:, None] + nprev * tq - ar(nk)[None, :]
    assert tq + 1 >= MAX_DIST
    far = rel_bias[N_BUCKETS - 1].astype(F32)[:, None, None]
    return {"near": _bias_table(rel_bias, dn, dn >= 0) - far,
            "win": _bias_table(rel_bias, dw, (dw >= 0) & (dw < WINDOW))}


def prompt_attn(qn, selx, selm, ocmp, y, winx, tabs, bsz, t, tq, kt):
    nt = t // tq
    nselp = selm.shape[1] // NSA_KV
    nprev = WINDOW // tq
    nk = (nprev + 1) * tq
    winp = jnp.pad(winx.reshape(bsz, t, 4 * LANE), ((0, 0), (nprev * tq, 0), (0, 0)))
    const = lambda shape: pl.BlockSpec(shape, lambda b, i: tuple(0 for _ in shape))
    tile = lambda w, c: pl.BlockSpec((tq, w), lambda b, i: (b * nt + i, c))
    return pl.pallas_call(
        functools.partial(_prompt_attn_kernel, tq=tq, nselp=nselp, nprev=nprev, kt=kt),
        grid=(bsz, nt),
        in_specs=[tile(NSA_WIDTH, 0),
                  pl.BlockSpec((t, 4 * LANE), lambda b, i: (b, 0)),
                  tile(NSA_KV * nselp, 0),
                  const((nselp, t)), const((NSA_HEADS, tq, 2 * tq)),
                  pl.BlockSpec((1, t + nprev * tq, 4 * LANE), lambda b, i: (b, 0, 0)),
                  const((NSA_HEADS, tq, nk)),
                  tile(NSA_WIDTH, 0),
                  tile(LANE, COL_NGATE // LANE),
                  const((LANE, 3 * NSA_WIDTH))],
        out_specs=tile(NSA_WIDTH, 0),
        out_shape=jax.ShapeDtypeStruct((bsz * t, NSA_WIDTH), BF16),
        scratch_shapes=[pltpu.VMEM((NSA_KV, tq, t), F32)],
        compiler_params=_cp("parallel", "arbitrary"),
        name="prompt_sel_win_attn",
    )(qn, selx, selm, _block_expand(nselp, t), tabs["near"], winp, tabs["win"], ocmp, y, _gate_expand())


def _decode_attn_kernel(*refs, t, nselp, masked, pg):
    refs = list(refs[1:] if pg else refs)
    qn_ref = refs.pop(0)
    past = [refs.pop(0) for _ in range(pg if pg else 2)]
    bp_ref, kn_ref, vn_ref, bn_ref = (refs.pop(0) for _ in range(4))
    if masked:
        selm_ref, e_ref = refs.pop(0), refs.pop(0)
    o_ref, m_ref, l_ref, acc_ref = refs
    if pg:
        j, bi, last = pl.program_id(0), pl.program_id(1), pl.num_programs(0) - 1
    else:
        j, bi, last = pl.program_id(1), 0, pl.num_programs(1) - 1
    rows = NSA_REP * t

    @pl.when(j == 0)
    def _():
        m_ref[bi] = jnp.full(m_ref.shape[1:], -jnp.inf, F32)
        l_ref[bi] = jnp.zeros(l_ref.shape[1:], F32)
        acc_ref[bi] = jnp.zeros(acc_ref.shape[1:], F32)

    q = qn_ref[...]
    tk = bp_ref.shape[2]
    state = [(m_ref[bi, g], l_ref[bi, g], acc_ref[bi, g]) for g in range(NSA_KV)]
    for g in range(NSA_KV):
        hs = slice(NSA_REP * g, NSA_REP * (g + 1))
        cs = slice(g * HEAD_DIM, (g + 1) * HEAD_DIM)
        qg = _stack_heads(q, g)
        if pg:
            kt = jnp.concatenate([p[0, g * HEAD_DIM:(g + 1) * HEAD_DIM, :] for p in past], axis=1)
            vt = jnp.concatenate([p[0, KV_W + g * HEAD_DIM:KV_W + (g + 1) * HEAD_DIM, :] for p in past], axis=1)
            s = _dot(qg, kt)
        else:
            s = _dot_nt(qg, past[0][0][:, cs])
        s = s.reshape(NSA_REP, t, tk) + bp_ref[hs]
        if masked:
            msk = jnp.dot(selm_ref[0, 0, g], e_ref[...], preferred_element_type=F32) > 0.5
            s = jnp.where(msk[None], s, NEG)
        if pg:
            state[g] = _online_update(state[g], s.reshape(rows, tk), vt, v_transposed=True)
        else:
            state[g] = _online_update(state[g], s.reshape(rows, tk), past[1][0][:, cs])
    for g in range(NSA_KV):
        m_ref[bi, g], l_ref[bi, g], acc_ref[bi, g] = state[g]

    @pl.when(j < last)
    def _():
        o_ref[0] = jnp.zeros(o_ref.shape[1:], F32)

    @pl.when(j == last)
    def _():
        outs = []
        for g in range(NSA_KV):
            hs = slice(NSA_REP * g, NSA_REP * (g + 1))
            cs = slice(g * HEAD_DIM, (g + 1) * HEAD_DIM)
            qg = _stack_heads(q, g)
            s = _dot_nt(qg, kn_ref[:, cs]).reshape(NSA_REP, t, t) + bn_ref[hs]
            _, l, acc = _online_update((m_ref[bi, g], l_ref[bi, g], acc_ref[bi, g]), s.reshape(rows, t), vn_ref[:, cs])
            outs.append(acc / l)
        o_ref[0] = _unstack_heads(outs, t)


def decode_attn(qn, kv_past, bias_past, new_rows, new_cols, bias_new, bsz, t, tk, selm=None, paged=None):
    masked = selm is not None
    assert not masked or paged
    nselp = selm.shape[1] // NSA_KV if masked else 0
    if paged:
        pt_flat, npages, pg = paged
        assert tk == pg * PAGE_SIZE
        nkt = npages // pg
        ix = lambda f: (lambda j, b, pt: f(b, j))

        def page_spec(k):
            return pl.BlockSpec((1, 2 * KV_W, PAGE_SIZE), lambda j, b, pt: (pt[b * npages + j * pg + k], 1, 0))

        past_specs, past_args = [page_spec(k) for k in range(pg)], [kv_past] * pg
    else:
        pg = 0
        nkt = kv_past.shape[1] // tk
        ix = lambda f: f
        past_specs = [pl.BlockSpec((1, tk, KV_W), lambda b, j: (b, j, 0)),
                      pl.BlockSpec((1, tk, KV_W), lambda b, j: (b, j, 1))]
        past_args = [kv_past, kv_past]
    in_specs = ([pl.BlockSpec((t, NSA_WIDTH), ix(lambda b, j: (b, 0)))] + past_specs +
                [pl.BlockSpec((NSA_HEADS, t, tk), ix(lambda b, j: (0, 0, j))),
                 pl.BlockSpec((t, KV_W), ix(lambda b, j: (b, new_cols[0]))),
                 pl.BlockSpec((t, KV_W), ix(lambda b, j: (b, new_cols[1]))),
                 pl.BlockSpec((NSA_HEADS, t, t), ix(lambda b, j: (0, 0, 0)))])
    args = [qn] + past_args + [bias_past, new_rows, new_rows, bias_new]
    if masked:
        nbt = tk // SEL_BLOCK
        sel_t = jnp.stack([selm[:, g * nselp:g * nselp + nkt * nbt].reshape(bsz, t, nkt, nbt) for g in range(NSA_KV)])
        sel_t = sel_t.transpose(1, 3, 0, 2, 4)
        in_specs += [pl.BlockSpec((1, 1, NSA_KV, t, nbt), ix(lambda b, j: (b, j, 0, 0, 0))),
                     pl.BlockSpec((nbt, tk), ix(lambda b, j: (0, 0)))]
        args += [sel_t, _block_expand(nbt, tk)]
    rows = NSA_REP * t
    nslab = nkt if paged else 1
    out_spec = pl.BlockSpec((1, t, NSA_WIDTH), ix(lambda b, j: (j if paged else 0, b, 0)))
    nst = bsz if paged else 1
    scratch = [pltpu.VMEM((nst, NSA_KV, rows, 1), F32), pltpu.VMEM((nst, NSA_KV, rows, 1), F32),
               pltpu.VMEM((nst, NSA_KV, rows, HEAD_DIM), F32)]
    body = functools.partial(_decode_attn_kernel, t=t, nselp=nselp, masked=masked, pg=pg)
    common = dict(out_shape=jax.ShapeDtypeStruct((nslab, bsz * t, NSA_WIDTH), F32),
                  name="decode_sel_attn" if masked else "decode_win_attn")
    if paged:
        grid_spec = pltpu.PrefetchScalarGridSpec(num_scalar_prefetch=1, grid=(nkt, bsz), in_specs=in_specs,
                                                 out_specs=out_spec, scratch_shapes=scratch)
        out = pl.pallas_call(body, grid_spec=grid_spec, compiler_params=_cp("arbitrary", "arbitrary"),
                             **common)(pt_flat, *args)
    else:
        out = pl.pallas_call(body, grid=(bsz, nkt), in_specs=in_specs, out_specs=out_spec, scratch_shapes=scratch,
                             compiler_params=_cp("parallel", "arbitrary"), **common)(*args)
    return out[nslab - 1]


def _combine_kernel(ng_ref, ge_ref, ocmp_ref, osel_ref, owin_ref, oc_ref):
    oc_ref[...] = _gated_sum(ng_ref[...], ge_ref[...], ocmp_ref[...], osel_ref[...], owin_ref[...]).astype(BF16)


def combine(y, ocmp, osel, owin):
    m = ocmp.shape[0]
    full = pl.BlockSpec((m, NSA_WIDTH), lambda i: (0, 0))
    return pl.pallas_call(
        _combine_kernel,
        grid=(1,),
        in_specs=[pl.BlockSpec((m, LANE), lambda i: (0, COL_NGATE // LANE)),
                  pl.BlockSpec((LANE, 3 * NSA_WIDTH), lambda i: (0, 0)), full, full, full],
        out_specs=full,
        out_shape=jax.ShapeDtypeStruct((m, NSA_WIDTH), BF16),
        compiler_params=_cp("arbitrary"),
        name="nsa_combine",
    )(y, _gate_expand(), ocmp, osel, owin)


def _overlap(nc, n_sel, nselp):
    cs = np.arange(nc) * CMP_STRIDE
    ss = np.arange(nselp) * SEL_BLOCK
    ov = (cs[:, None] < ss[None, :] + SEL_BLOCK) & (cs[:, None] + CMP_LEN > ss[None, :])
    ov &= (np.arange(nc) < nc - 1)[:, None] & (np.arange(nselp) < n_sel)[None, :]
    return jnp.asarray(ov.astype(np.float32), BF16)


def _round_up(x, m):
    return -(-x // m) * m


def position_tables(rel_bias, t, p0, lwin):
    nc = (t if p0 == 0 else p0) // CMP_STRIDE
    ar = jnp.arange
    qpos = p0 + ar(t)
    kend = ar(nc) * CMP_STRIDE + CMP_LEN - 1
    tabs = {"cmp": _bias_table(rel_bias, qpos[:, None] - kend[None, :])}
    if p0 == 0:
        tabs["attn"] = prompt_attn_tables(rel_bias, min(t, 256))
    else:
        dn = ar(t)[:, None] - ar(t)[None, :]
        tabs["new"] = _bias_table(rel_bias, dn, dn >= 0)
        tabs["sel"] = _bias_table(rel_bias, qpos[:, None] - ar(p0)[None, :])
        dw = qpos[:, None] - (p0 - lwin + ar(lwin))[None, :]
        tabs["win"] = _bias_table(rel_bias, dw, (dw >= 0) & (dw < WINDOW))
    return tabs


def layer(x, past, lw, tabs, bsz, t):
    m = bsz * t
    prompt = past is None
    tm = min(512, m)
    y = rms_matmul(x, lw["norm_mix"], lw["w_in"], min(1024, m), 1408)

    c = math.gcd(t, HG_CHUNK)
    s0 = jnp.zeros((bsz, HG_HEADS, HG_DIM, HG_DIM), F32) if prompt else past["hgrn"]
    oa, s_hg = hgrn(y, lw["lb"], lw["hg_norm"], s0, bsz, t, c, min(t, 2048))

    ts = min(t, 256)
    if prompt:
        h0r = h0i = jnp.zeros((bsz, 1, S5_CH), F32)
    else:
        h0r, h0i = past["s5r"].reshape(bsz, 1, S5_CH), past["s5i"].reshape(bsz, 1, S5_CH)
    ob, s5r, s5i = s5(y, lw["s5"], h0r, h0i, bsz, t, ts)

    qn, rows, winrows, cv, selx, winx = nsa_prep(y, lw["qkn"], tm)
    if prompt:
        p0, nc = 0, t // CMP_STRIDE
        chunks = cv.reshape(m // CMP_STRIDE, CMP_STRIDE * 2 * KV_W)
        cacb = matmul(chunks, lw["cmp_wab"], min(512, chunks.shape[0])).reshape(bsz, nc, 1024)
    else:
        npages = past["npages"]
        pg = min(16, npages)
        p0 = npages * PAGE_SIZE
        nc = p0 // CMP_STRIDE
        cacb = cmp_paged(past["cache_t"], past["pt_flat"], lw["cmp_wab"], bsz, npages, min(32, npages))
    kcmp, vcmp = cmp_post(cacb, lw["cmp_bias"], lw["cmp_w2"], lw["qkn"][1])
    n_sel = -(-(p0 + t) // SEL_BLOCK)
    nselp = _round_up(n_sel, LANE)
    tq = min(t, 128)
    nsr = _round_up(n_sel, 8) if prompt else 0
    ocmp, selm = cmpsel(qn, kcmp, vcmp, tabs["cmp"], _overlap(nc, n_sel, nselp), bsz, t, tq, p0, n_sel, nsr)
    if prompt:
        oc = prompt_attn(qn, selx, selm, ocmp, y, winx, tabs["attn"], bsz, t, min(t, 256), 4)
        lw_ = min(WINDOW, t)
        new_win = winrows.reshape(bsz, t, 2 * KV_W)[:, t - lw_:]
    else:
        osel = decode_attn(qn, past["cache_t"], tabs["sel"], rows, (2, 3), tabs["new"], bsz, t, pg * PAGE_SIZE,
                           selm=selm, paged=(past["pt_flat"], npages, pg))
        win = past["win"]
        owin = decode_attn(qn, win, tabs["win"], winrows, (0, 1), tabs["new"], bsz, t, win.shape[1])
        oc = combine(y, ocmp, osel, owin)
        new_win = jnp.concatenate([win, winrows.reshape(bsz, t, 2 * KV_W)], axis=1)[:, t:]

    x1 = merge(x, y, oa, ob, oc, lw["w_branch"], lw["w_out"], tm)
    x2 = ffn(x1, lw["norm_ffn"], lw["w_gate_up"], lw["w_down"], min(1024, m), 1408)
    return x2, (rows, new_win, s_hg, s5r, s5i)


def layer_weights(l, norm_mix, w_in, lower_bounds, hg_out_norm, s5_a_re, s5_a_im, s5_log_dt, s5_b_re, s5_b_im,
                  s5_c_re, s5_c_im, s5_d, s5_w_glu, nsa_qk_norm, cmp_pe, cmp_w1, cmp_w2, w_branch, w_out, norm_ffn,
                  w_gate_up, w_down):
    wab, cbias, w2blk = cmp_params(cmp_pe[l], cmp_w1[l], cmp_w2[l])
    s5p = s5_params(s5_a_re[l], s5_a_im[l], s5_log_dt[l], s5_b_re[l], s5_b_im[l], s5_c_re[l], s5_c_im[l],
                    s5_d[l], s5_w_glu[l], S5_SUB)
    return {"norm_mix": norm_mix[l].reshape(1, D_MODEL), "w_in": prep_w_in(w_in[l]),
            "lb": lower_bounds[l].reshape(1, HG_WIDTH), "hg_norm": hg_out_norm[l].reshape(1, HG_DIM),
            "s5": s5p, "qkn": nsa_qk_norm[l], "cmp_wab": wab, "cmp_bias": cbias, "cmp_w2": w2blk,
            "w_branch": w_branch[l].astype(BF16), "w_out": w_out[l].astype(BF16),
            "norm_ffn": norm_ffn[l].reshape(1, D_MODEL), "w_gate_up": w_gate_up[l].astype(BF16),
            "w_down": w_down[l].astype(BF16)}


def kernel(x_prompt, x_sample, cache_nsa_kv, cache_win_kv, state_hgrn, state_s5_re, state_s5_im, page_table,
           norm_mix, w_in, hg_lb_logits, hg_out_norm, s5_a_re, s5_a_im, s5_log_dt, s5_b_re, s5_b_im,
           s5_c_re, s5_c_im, s5_d, s5_w_glu, nsa_qk_norm, cmp_pe, cmp_w1, cmp_w2, rel_bias,
           w_branch, w_out, norm_ffn, w_gate_up, w_down):
    depth = w_in.shape[0]
    bp, tp, d = x_prompt.shape
    bs, tsm, _ = x_sample.shape
    n_phys = cache_nsa_kv.shape[1]
    npages = page_table.shape[1]
    lb_sm = jax.nn.softmax(hg_lb_logits.astype(F32), axis=0)
    lower_bounds = jnp.cumsum(lb_sm, axis=0) - lb_sm[0]
    hp = x_prompt.reshape(bp * tp, d)
    hs = x_sample.reshape(bs * tsm, d)
    cache_t = jnp.transpose(cache_nsa_kv, (0, 1, 3, 4, 5, 2)).reshape(depth * n_phys, 4 * KV_W, PAGE_SIZE)
    tabs_p = position_tables(rel_bias, tp, 0, 0)
    tabs_s = position_tables(rel_bias, tsm, npages * PAGE_SIZE, cache_win_kv.shape[2])
    st_p, st_s = [], []
    for l in range(depth):
        lw = layer_weights(l, norm_mix, w_in, lower_bounds, hg_out_norm, s5_a_re, s5_a_im,
                           s5_log_dt, s5_b_re, s5_b_im, s5_c_re, s5_c_im, s5_d, s5_w_glu, nsa_qk_norm, cmp_pe,
                           cmp_w1, cmp_w2, w_branch, w_out, norm_ffn, w_gate_up, w_down)
        past = {"cache_t": cache_t, "pt_flat": (page_table + l * n_phys).reshape(-1).astype(jnp.int32),
                "npages": npages, "win": cache_win_kv[l].reshape(bs, -1, 2 * KV_W), "hgrn": state_hgrn[l],
                "s5r": state_s5_re[l], "s5i": state_s5_im[l]}
        hp, sp = layer(hp, None, lw, tabs_p, bp, tp)
        hs, ss = layer(hs, past, lw, tabs_s, bs, tsm)
        st_p.append(sp)
        st_s.append(ss)

    def stack(states, k, shape):
        return jnp.stack([s[k].reshape(shape) for s in states])

    kvs = (4, NSA_KV, HEAD_DIM)
    return (hp.reshape(bp, tp, d), hs.reshape(bs, tsm, d),
            stack(st_p, 0, (bp, tp) + kvs), stack(st_s, 0, (bs, tsm) + kvs),
            stack(st_p, 1, (bp, -1, 2, NSA_KV, HEAD_DIM)), stack(st_s, 1, (bs, -1, 2, NSA_KV, HEAD_DIM)),
            stack(st_p, 2, (bp, HG_HEADS, HG_DIM, HG_DIM)), stack(st_s, 2, (bs, HG_HEADS, HG_DIM, HG_DIM)),
            stack(st_p, 3, (bp, S5_GROUPS, S5_STATE)), stack(st_p, 4, (bp, S5_GROUPS, S5_STATE)),
            stack(st_s, 3, (bs, S5_GROUPS, S5_STATE)), stack(st_s, 4, (bs, S5_GROUPS, S5_STATE)))
```
